```python
import jax, jax.numpy as jnp
from jax import lax
import numpy as np

D_MODEL = 2048
BATCH = 32
SEQ = 256
DEPTH = 1
DEC_BATCH = 8
DEC_SEQ = 4096
PAST_LEN = 512

GRID_W = 64
HG_WIDTH = D_MODEL // 2
HG_DK = 128
HG_DV = 128
HG_HEADS = HG_WIDTH // HG_DK
POOL_WIDTH = D_MODEL - HG_WIDTH
POOL_WINDOWS = (2, 4, 8, 16)
N_POOL_GROUPS = len(POOL_WINDOWS)
POOL_GROUP = POOL_WIDTH // N_POOL_GROUPS
MIX_WIDTH = HG_WIDTH + POOL_WIDTH
IN_WIDTH = 5 * HG_WIDTH + POOL_WIDTH
D_FF = 5632
CONV_W = 3
CHUNK = 32
N_MOD = 6
EPS = 1e-6

kernel_name = "hybrid_hgrn2_pool_dit_step"


def rmsnorm(x, w):
    xf = x.astype(jnp.float32)
    y = xf * lax.rsqrt(jnp.mean(xf * xf, axis=-1, keepdims=True) + EPS)
    return (y * w.astype(jnp.float32)).astype(x.dtype)


def ada_mod(cvec, w_ada, b_ada):
    m = jax.nn.silu(cvec) @ w_ada + b_ada
    if m.ndim == 2:
        m = m[:, None, :]
    return jnp.split(m, N_MOD, axis=-1)


def to_heads(a):
    B, L, _ = a.shape
    return a.reshape(B, L, HG_HEADS, -1).transpose(0, 2, 1, 3)


def chunk_scan(q, k, v, log_f, s0):
    B, H, L, DK = q.shape
    n = L // CHUNK

    def to_chunks(a):
        return jnp.moveaxis(a.reshape(B, H, n, CHUNK, a.shape[-1]), 2, 0)

    mask = jnp.tril(jnp.ones((CHUNK, CHUNK), dtype=bool))[:, :, None]

    def step(S, inp):
        qc, kc, vc, gc = inp
        b = jnp.cumsum(gc, axis=2)
        diff = b[:, :, :, None, :] - b[:, :, None, :, :]
        decay = jnp.exp(jnp.where(mask, diff, -jnp.inf))
        scores = jnp.einsum('bhtd,bhsd,bhtsd->bhts', qc, kc, decay)
        o = (jnp.einsum('bhts,bhsv->bhtv', scores, vc)
             + jnp.einsum('bhtd,bhdv->bhtv', qc * jnp.exp(b), S))
        bl = b[:, :, -1:, :]
        S = (jnp.exp(bl[:, :, 0, :])[..., None] * S
             + jnp.einsum('bhsd,bhsv->bhdv', kc * jnp.exp(bl - b), vc))
        return S, o

    S, o = lax.scan(step, s0, (to_chunks(q), to_chunks(k), to_chunks(v), to_chunks(log_f)))
    o = jnp.moveaxis(o, 0, 2).reshape(B, H, L, v.shape[-1])
    return o, S


def hgrn2_bidir(q, zf, zb, v, lb, s0):
    f32 = jnp.float32
    q = jax.nn.silu(q.astype(f32))
    v = v.astype(f32)
    s0 = s0.astype(f32)

    def gates(z, lbd):
        lbd = lbd.reshape(HG_HEADS, 1, HG_DK)
        f = lbd + (1.0 - lbd) * jax.nn.sigmoid(z.astype(f32))
        return jnp.log(f), 1.0 - f

    logf_f, k_f = gates(zf, lb[0])
    logf_b, k_b = gates(zb, lb[1])
    rev = lambda a: jnp.flip(a, axis=2)
    o_f, S_f = chunk_scan(q, k_f, v, logf_f, s0[:, 0])
    o_b, S_b = chunk_scan(rev(q), rev(k_b), rev(v), rev(logf_b), s0[:, 1])
    return o_f + rev(o_b), jnp.stack([S_f, S_b], axis=1)


def centred_mean(a, w, axis):
    L = a.shape[axis]
    t = jnp.arange(L)
    lo = jnp.clip(t - w // 2, 0, L)
    hi = jnp.clip(t + (w - w // 2), 0, L)
    pad = [(0, 0)] * a.ndim
    pad[axis] = (1, 0)
    cs = jnp.pad(jnp.cumsum(a, axis=axis), pad)
    s = jnp.take(cs, hi, axis=axis) - jnp.take(cs, lo, axis=axis)
    shape = [1] * a.ndim
    shape[axis] = L
    cnt = (hi - lo).astype(a.dtype).reshape(shape)
    return s / cnt


def pool_mixer(p, w_pool, pool_scale, on_grid):
    B, L, _ = p.shape
    pf = p.astype(jnp.float32)
    outs = []
    for gi, w in enumerate(POOL_WINDOWS):
        pg = pf[..., gi * POOL_GROUP:(gi + 1) * POOL_GROUP]
        if on_grid:
            rows = L // GRID_W
            pgg = pg.reshape(B, rows, GRID_W, POOL_GROUP)
            m = centred_mean(centred_mean(pgg, w, 2), w, 1).reshape(B, L, POOL_GROUP)
        else:
            m = centred_mean(pg, w, 1)
        outs.append((m - pg).astype(p.dtype) @ w_pool[gi])
    return jnp.concatenate(outs, axis=-1) * pool_scale


def token_mixer(h, s0, lb, w_in, hg_norm_w, w_pool, pool_scale, w_out, on_grid):
    B, L, _ = h.shape
    proj = h @ w_in
    q, zf, zb, v, g, p = jnp.split(proj, [HG_WIDTH * i for i in range(1, 6)], axis=-1)
    o, S = hgrn2_bidir(to_heads(q), to_heads(zf), to_heads(zb), to_heads(v), lb, s0)
    o = o * lax.rsqrt(jnp.mean(o * o, axis=-1, keepdims=True) + EPS) * hg_norm_w.astype(jnp.float32)
    o = o * jax.nn.silu(to_heads(g).astype(jnp.float32))
    o = o.transpose(0, 2, 1, 3).reshape(B, L, HG_WIDTH).astype(h.dtype)
    pm = pool_mixer(p, w_pool, pool_scale, on_grid)
    return jnp.concatenate([o, pm], axis=-1) @ w_out, S


def conv_ffn(h, w_up, conv_w, conv_b, w_down):
    u = h @ w_up
    up = jnp.pad(u, ((0, 0), (1, 1), (0, 0)))
    u = up[:, :-2] * conv_w[0] + up[:, 1:-1] * conv_w[1] + up[:, 2:] * conv_w[2] + conv_b
    a, b = jnp.split(u, 2, axis=-1)
    return (jax.nn.silu(a) * b) @ w_down


def trunk_layer(x, mod, s0, lb, norm1_w, w_in, hg_norm_w, w_pool, pool_scale, w_out,
                norm2_w, w_up, conv_w, conv_b, w_down, on_grid):
    sh1, sc1, g1, sh2, sc2, g2 = mod
    h = rmsnorm(x, norm1_w) * (1.0 + sc1) + sh1
    m, S = token_mixer(h, s0, lb, w_in, hg_norm_w, w_pool, pool_scale, w_out, on_grid)
    x = x + g1 * m
    h = rmsnorm(x, norm2_w) * (1.0 + sc2) + sh2
    x = x + g2 * conv_ffn(h, w_up, conv_w, conv_b, w_down)
    return x, S


def setup_inputs(seed: int = 0) -> dict:
    key = jax.random.key(seed)
    ks = jax.random.split(key, 20)
    f32 = jnp.float32
    nrm = lambda k, shape, s: jax.random.normal(k, shape, f32) * s
    return {
        "x_prompt": nrm(ks[0], (BATCH, SEQ, D_MODEL), 1.0),
        "x_sample": nrm(ks[1], (DEC_BATCH, DEC_SEQ, D_MODEL), 1.0),
        "state_hgrn": nrm(ks[2], (DEC_BATCH, DEPTH, 2, HG_HEADS, HG_DK, HG_DV), 0.3),
        "c": nrm(ks[3], (DEC_BATCH, D_MODEL), 1.0),
        "c_ctx": nrm(ks[4], (D_MODEL,), 1.0),
        "w_ada": nrm(ks[5], (DEPTH, D_MODEL, N_MOD * D_MODEL), 0.5 * D_MODEL ** -0.5),
        "b_ada": nrm(ks[6], (DEPTH, N_MOD * D_MODEL), 0.02),
        "norm1_w": 1.0 + nrm(ks[7], (DEPTH, D_MODEL), 0.02),
        "w_in": nrm(ks[8], (DEPTH, D_MODEL, IN_WIDTH), D_MODEL ** -0.5),
        "lb_param": nrm(ks[9], (2, DEPTH + 1, HG_WIDTH), 0.5),
        "hg_norm_w": 1.0 + nrm(ks[10], (DEPTH, HG_DV), 0.02),
        "w_pool": nrm(ks[11], (DEPTH, N_POOL_GROUPS, POOL_GROUP, POOL_GROUP), POOL_GROUP ** -0.5),
        "pool_scale": 1.0 + nrm(ks[12], (DEPTH, POOL_WIDTH), 0.02),
        "w_out": nrm(ks[13], (DEPTH, MIX_WIDTH, D_MODEL), MIX_WIDTH ** -0.5),
        "norm2_w": 1.0 + nrm(ks[14], (DEPTH, D_MODEL), 0.02),
        "w_up": nrm(ks[15], (DEPTH, D_MODEL, 2 * D_FF), D_MODEL ** -0.5),
        "conv_w": nrm(ks[16], (DEPTH, CONV_W, 2 * D_FF), CONV_W ** -0.5),
        "conv_b": nrm(ks[17], (DEPTH, 2 * D_FF), 0.02),
        "w_down": nrm(ks[18], (DEPTH, D_FF, D_MODEL), D_FF ** -0.5),
        "final_norm_w": 1.0 + nrm(ks[19], (D_MODEL,), 0.02),
    }


def reference(x_prompt, x_sample, state_hgrn, c, c_ctx, w_ada, b_ada, norm1_w, w_in,
              lb_param, hg_norm_w, w_pool, pool_scale, w_out, norm2_w, w_up, conv_w,
              conv_b, w_down, final_norm_w):
    lb_all = jnp.cumsum(jax.nn.softmax(lb_param.astype(jnp.float32), axis=1), axis=1)
    xp, xs = x_prompt, x_sample
    ctx_states = []
    for l in range(DEPTH):
        lb = lb_all[:, l]
        weights = (norm1_w[l], w_in[l], hg_norm_w[l], w_pool[l], pool_scale[l], w_out[l],
                   norm2_w[l], w_up[l], conv_w[l], conv_b[l], w_down[l])
        s0 = jnp.zeros((xp.shape[0], 2, HG_HEADS, HG_DK, HG_DV), jnp.float32)
        xp, S_ctx = trunk_layer(xp, ada_mod(c_ctx, w_ada[l], b_ada[l]), s0, lb, *weights,
                                on_grid=False)
        ctx_states.append(S_ctx.astype(x_prompt.dtype))
        xs, _ = trunk_layer(xs, ada_mod(c, w_ada[l], b_ada[l]), state_hgrn[:, l], lb, *weights,
                            on_grid=True)
    y_prompt = rmsnorm(xp, final_norm_w)
    y_sample = rmsnorm(xs, final_norm_w)
    new_state_hgrn = jnp.stack(ctx_states, axis=1)
    return (y_prompt, y_sample, new_state_hgrn)
```

```python
import functools

import jax
import jax.numpy as jnp
from jax import lax
from jax.experimental import pallas as pl
from jax.experimental.pallas import tpu as pltpu

F32 = jnp.float32
BF16 = jnp.bfloat16
EPS = 1e-6

HEAD_DIM = 128
N_MOD = 6
POOL_WINDOWS = (2, 4, 8, 16)
GRID_W = 64
CONV_HALO = 8
VMEM_LIMIT = 56 * 1024 * 1024


def _params(sem):
    return pltpu.CompilerParams(dimension_semantics=sem, vmem_limit_bytes=VMEM_LIMIT)


def _silu(x):
    return x * jax.nn.sigmoid(x)


def _norm_mod(x, nw, sc, sh):
    y = x * lax.rsqrt(jnp.mean(x * x, axis=-1, keepdims=True) + EPS) * nw
    return y * (1.0 + sc) + sh


def _ada_kernel(c_ref, w_ref, b_ref, o_ref):
    s = _silu(c_ref[...]).astype(BF16)
    o_ref[...] = jnp.dot(s, w_ref[...].astype(BF16), preferred_element_type=F32) + b_ref[...]


def _ada(cvecs, w_ada, b_ada, tn=1024):
    r, d = cvecs.shape
    n = w_ada.shape[1]
    return pl.pallas_call(
        _ada_kernel,
        grid=(n // tn,),
        in_specs=[pl.BlockSpec((r, d), lambda j: (0, 0)),
                  pl.BlockSpec((d, tn), lambda j: (0, j)),
                  pl.BlockSpec((1, tn), lambda j: (0, j))],
        out_specs=pl.BlockSpec((r, tn), lambda j: (0, j)),
        out_shape=jax.ShapeDtypeStruct((r, n), F32),
        compiler_params=_params(("arbitrary",)),
        name="ada",
    )(cvecs, w_ada, b_ada.reshape(1, n))


def _in_proj_kernel(x_ref, mod_ref, nw_ref, w_ref, o_ref, h_ref):
    @pl.when(pl.program_id(1) == 0)
    def _():
        mod = mod_ref[0]
        h_ref[...] = _norm_mod(x_ref[...], nw_ref[...], mod[1:2], mod[0:1]).astype(BF16)

    o_ref[...] = jnp.dot(h_ref[...], w_ref[...], preferred_element_type=F32)


def _in_proj(x2, mod, rows_per_mod, nw, w, tm=1024, tn=1024):
    t, d = x2.shape
    n = w.shape[1]
    return pl.pallas_call(
        _in_proj_kernel,
        grid=(t // tm, n // tn),
        in_specs=[pl.BlockSpec((tm, d), lambda i, j: (i, 0)),
                  pl.BlockSpec((1, N_MOD, d), lambda i, j: ((i * tm) // rows_per_mod, 0, 0)),
                  pl.BlockSpec((1, d), lambda i, j: (0, 0)),
                  pl.BlockSpec((d, tn), lambda i, j: (0, j))],
        out_specs=pl.BlockSpec((tm, tn), lambda i, j: (i, j)),
        out_shape=jax.ShapeDtypeStruct((t, n), F32),
        scratch_shapes=[pltpu.VMEM((tm, d), BF16)],
        compiler_params=_params(("arbitrary", "arbitrary")),
        name="in_proj",
    )(x2, mod, nw.reshape(1, d), w)


def _gates(z, lbd):
    f = lbd + (1.0 - lbd) * jax.nn.sigmoid(z)
    return jnp.log(f), 1.0 - f


def _block_scans(logf, row, c, inclusive_prefix):
    zero = jnp.zeros_like(logf)
    tot = logf
    pre, suf = (logf, zero) if inclusive_prefix else (zero, logf)
    out = {1: (pre, suf)}
    m = 1
    while m < c:
        second = (row & m) != 0
        below = pltpu.roll(tot, m, 0)
        above = pltpu.roll(tot, c - m, 0)
        pre = pre + jnp.where(second, below, 0.0)
        suf = suf + jnp.where(second, 0.0, above)
        tot = tot + jnp.where(second, below, above)
        m *= 2
        out[m] = (pre, suf)
    return out, tot


def _dot_nt(a, b):
    return lax.dot_general(a, b, (((1,), (1,)), ((), ())), preferred_element_type=F32)


def _dot_tn(a, b):
    return lax.dot_general(a, b, (((0,), (0,)), ((), ())), preferred_element_type=F32)


def _scan_kernel(*refs, seq, c, has_s0, want_state):
    q_ref, zf_ref, zb_ref, v_ref, g_ref, lb_ref, nw_ref = refs[:7]
    pos = 7
    s0_ref = None
    if has_s0:
        s0_ref = refs[pos]
        pos += 1
    o_ref = refs[pos]
    pos += 1
    sout_ref = None
    if want_state:
        sout_ref = refs[pos]
        pos += 1
    lfb_scr, kb_scr, sb_scr = refs[pos:pos + 3]

    n_chunks = seq // c
    lb_f = lb_ref[0:1, :]
    lb_b = lb_ref[1:2, :]
    nw = nw_ref[...]
    row = lax.broadcasted_iota(jnp.int32, (c, HEAD_DIM), 0)
    pair = (lax.broadcasted_iota(jnp.int32, (c, c), 0)
            ^ lax.broadcasted_iota(jnp.int32, (c, c), 1))

    if has_s0:
        stf0 = s0_ref[0].T
        stb0 = s0_ref[1].T
    else:
        stf0 = jnp.zeros((HEAD_DIM, HEAD_DIM), F32)
        stb0 = stf0

    def sweep_b(i, stb):
        ci = n_chunks - 1 - i
        rows = pl.ds(pl.multiple_of(ci * c, c), c)
        logf_b, k_b = _gates(zb_ref[rows, :], lb_b)
        lfb_scr[rows, :] = logf_b
        kb_scr[rows, :] = k_b
        sb_scr[ci] = stb
        lv, tot = _block_scans(logf_b, row, c, inclusive_prefix=False)
        k_hat = (k_b * jnp.exp(lv[c][0])).astype(BF16)
        v_bf = v_ref[rows, :].astype(BF16)
        return stb * jnp.exp(tot[0:1, :]) + _dot_tn(v_bf, k_hat)

    stb_final = lax.fori_loop(0, n_chunks, sweep_b, stb0)

    def sweep_f(ci, stf):
        rows = pl.ds(pl.multiple_of(ci * c, c), c)
        qs = _silu(q_ref[rows, :])
        v = v_ref[rows, :]
        v_bf = v.astype(BF16)
        logf_f, k_f = _gates(zf_ref[rows, :], lb_f)
        logf_b = lfb_scr[rows, :]
        k_b = kb_scr[rows, :]
        lv_f, tot_f = _block_scans(logf_f, row, c, inclusive_prefix=True)
        lv_b, _ = _block_scans(logf_b, row, c, inclusive_prefix=False)

        a = jnp.zeros((c, c), F32)
        m = 1
        shift = 0
        while m < c:
            second = (row & m) != 0
            ipre_f, esuf_f = lv_f[m]
            epre_b, isuf_b = lv_b[m]
            e_q = jnp.exp(jnp.where(second, ipre_f, isuf_b))
            e_k = jnp.exp(jnp.where(second, epre_b, esuf_f))
            x = (qs * e_q).astype(BF16)
            y = (jnp.where(second, k_b, k_f) * e_k).astype(BF16)
            a = jnp.where(lax.shift_right_logical(pair, shift) == 1, _dot_nt(x, y), a)
            m *= 2
            shift += 1

        o = jnp.dot(a.astype(BF16), v_bf, preferred_element_type=F32)
        o = o + jnp.sum(qs * (k_f + k_b), axis=-1, keepdims=True) * v
        q_f = (qs * jnp.exp(lv_f[c][0])).astype(BF16)
        q_b = (qs * jnp.exp(lv_b[c][1])).astype(BF16)
        o = o + _dot_nt(q_f, stf.astype(BF16)) + _dot_nt(q_b, sb_scr[ci].astype(BF16))

        o = o * lax.rsqrt(jnp.mean(o * o, axis=-1, keepdims=True) + EPS) * nw
        o = o * _silu(g_ref[rows, :])
        o_ref[rows, :] = o.astype(o_ref.dtype)

        k_hat = (k_f * jnp.exp(lv_f[c][1])).astype(BF16)
        return stf * jnp.exp(tot_f[0:1, :]) + _dot_tn(v_bf, k_hat)

    stf_final = lax.fori_loop(0, n_chunks, sweep_f, stf0)

    if want_state:
        sout_ref[0] = stf_final.T
        sout_ref[1] = stb_final.T


def _scan(proj, lb, hg_nw, s0, layer, batch, seq, want_state, c=64):
    t = proj.shape[0]
    n_heads = lb.shape[1] // HEAD_DIM
    has_s0 = s0 is not None

    def sec(k):
        return pl.BlockSpec((seq, HEAD_DIM), lambda b, h, k=k: (b, k * n_heads + h))

    state_block = (None, None, 2, None, HEAD_DIM, HEAD_DIM)
    state_map = lambda b, h: (b, layer, 0, h, 0, 0)
    in_specs = [sec(0), sec(1), sec(2), sec(3), sec(4),
                pl.BlockSpec((2, HEAD_DIM), lambda b, h: (0, h)),
                pl.BlockSpec((1, HEAD_DIM), lambda b, h: (0, 0))]
    args = [proj, proj, proj, proj, proj, lb, hg_nw.reshape(1, HEAD_DIM)]
    if has_s0:
        in_specs.append(pl.BlockSpec(state_block, state_map))
        args.append(s0)
    out_specs = [pl.BlockSpec((seq, HEAD_DIM), lambda b, h: (b, h))]
    out_shape = [jax.ShapeDtypeStruct((t, n_heads * HEAD_DIM), BF16)]
    if want_state:
        out_specs.append(pl.BlockSpec((None, 2, None, HEAD_DIM, HEAD_DIM),
                                      lambda b, h: (b, 0, h, 0, 0)))
        out_shape.append(jax.ShapeDtypeStruct((batch, 2, n_heads, HEAD_DIM, HEAD_DIM), F32))
    res = pl.pallas_call(
        functools.partial(_scan_kernel, seq=seq, c=c, has_s0=has_s0, want_state=want_state),
        grid=(batch, n_heads),
        in_specs=in_specs,
        out_specs=out_specs,
        out_shape=out_shape,
        scratch_shapes=[pltpu.VMEM((seq, HEAD_DIM), F32),
                        pltpu.VMEM((seq, HEAD_DIM), F32),
                        pltpu.VMEM((seq // c, HEAD_DIM, HEAD_DIM), F32)],
        compiler_params=_params(("arbitrary", "arbitrary")),
        name="scan",
    )(*args)
    return (res[0], res[1]) if want_state else (res[0], None)


def _window_mean(a, w, pos, n, stride):
    rows = a.shape[0]
    half = w // 2
    fwd = a
    ln = 1
    while ln < half:
        fwd = fwd + jnp.where(pos < n - ln, pltpu.roll(fwd, rows - ln * stride, 0), 0.0)
        ln *= 2
    bwd = jnp.where(pos >= 1, pltpu.roll(a, stride, 0), 0.0)
    ln = 1
    while ln < half:
        bwd = bwd + jnp.where(pos >= ln, pltpu.roll(bwd, ln * stride, 0), 0.0)
        ln *= 2
    cnt = jnp.minimum(pos + (w - half), n) - jnp.maximum(pos - half, 0)
    return (fwd + bwd) / cnt.astype(F32)


def _pool_kernel(p_ref, w_ref, sc_ref, o_ref, *, seq, on_grid):
    grp = pl.program_id(1)
    for gi, w in enumerate(POOL_WINDOWS):
        @pl.when(grp == gi)
        def _(w=w):
            pg = p_ref[...]
            row = lax.broadcasted_iota(jnp.int32, pg.shape, 0)
            if on_grid:
                m = _window_mean(pg, w, row % GRID_W, GRID_W, 1)
                m = _window_mean(m, w, row // GRID_W, seq // GRID_W, GRID_W)
            else:
                m = _window_mean(pg, w, row, seq, 1)
            d = (m - pg).astype(BF16)
            y = jnp.dot(d, w_ref[...], preferred_element_type=F32) * sc_ref[...]
            o_ref[...] = y.astype(o_ref.dtype)


def _pool(proj, w_pool_l, pool_scale_l, col0, batch, seq, on_grid):
    t = proj.shape[0]
    n_grp, gw, _ = w_pool_l.shape
    blk0 = col0 // gw
    return pl.pallas_call(
        functools.partial(_pool_kernel, seq=seq, on_grid=on_grid),
        grid=(batch, n_grp),
        in_specs=[pl.BlockSpec((seq, gw), lambda b, g: (b, blk0 + g)),
                  pl.BlockSpec((None, gw, gw), lambda b, g: (g, 0, 0)),
                  pl.BlockSpec((1, gw), lambda b, g: (0, g))],
        out_specs=pl.BlockSpec((seq, gw), lambda b, g: (b, g)),
        out_shape=jax.ShapeDtypeStruct((t, n_grp * gw), BF16),
        compiler_params=_params(("arbitrary", "arbitrary")),
        name="pool",
    )(proj, w_pool_l, pool_scale_l.reshape(1, n_grp * gw))


def _mix_kernel(o_ref, pm_ref, x_ref, mod_ref, w_ref, out_ref):
    k = o_ref.shape[1]
    mix = (jnp.dot(o_ref[...], w_ref[0:k, :], preferred_element_type=F32)
           + jnp.dot(pm_ref[...], w_ref[k:, :], preferred_element_type=F32))
    out_ref[...] = x_ref[...] + mod_ref[0][2:3] * mix


def _mix(o, pm, x2, mod, rows_per_mod, w_out, tm=512):
    t, d = x2.shape
    return pl.pallas_call(
        _mix_kernel,
        grid=(t // tm,),
        in_specs=[pl.BlockSpec((tm, o.shape[1]), lambda i: (i, 0)),
                  pl.BlockSpec((tm, pm.shape[1]), lambda i: (i, 0)),
                  pl.BlockSpec((tm, d), lambda i: (i, 0)),
                  pl.BlockSpec((1, N_MOD, d), lambda i: ((i * tm) // rows_per_mod, 0, 0)),
                  pl.BlockSpec(w_out.shape, lambda i: (0, 0))],
        out_specs=pl.BlockSpec((tm, d), lambda i: (i, 0)),
        out_shape=jax.ShapeDtypeStruct((t, d), F32),
        compiler_params=_params(("arbitrary",)),
        name="mix",
    )(o, pm, x2, mod, w_out)


def _ffn_kernel(x_ref, xp_ref, xn_ref, mod_ref, nw_ref, wa_ref, wb_ref, cwa_ref, cwb_ref,
                cba_ref, cbb_ref, wd_ref, fnw_ref, o_ref, h_ref, acc_ref,
                *, tm, seq, final_norm):
    i = pl.program_id(0)
    f = pl.program_id(1)
    mod = mod_ref[0]

    @pl.when(f == 0)
    def _():
        nw = nw_ref[...]
        sc, sh = mod[4:5], mod[3:4]
        h_ref[0:tm, :] = _norm_mod(x_ref[...], nw, sc, sh).astype(BF16)
        h_ref[tm:tm + CONV_HALO, :] = _norm_mod(xp_ref[...], nw, sc, sh).astype(BF16)
        h_ref[tm + CONV_HALO:, :] = _norm_mod(xn_ref[...], nw, sc, sh).astype(BF16)
        acc_ref[...] = jnp.zeros_like(acc_ref)

    h = h_ref[...]
    tf = wa_ref.shape[1]
    local = lax.broadcasted_iota(jnp.int32, (tm, tf), 0)
    in_seq = (i * tm + local) % seq

    def conv(w_ref, cw_ref, cb_ref):
        u = jnp.dot(h, w_ref[...], preferred_element_type=F32)
        um = u[0:tm]
        before = u[tm + CONV_HALO - 1:tm + CONV_HALO]
        after = u[tm + CONV_HALO:tm + CONV_HALO + 1]
        up = jnp.where(local == 0, before, pltpu.roll(um, 1, 0))
        up = jnp.where(in_seq == 0, 0.0, up)
        un = jnp.where(local == tm - 1, after, pltpu.roll(um, tm - 1, 0))
        un = jnp.where(in_seq == seq - 1, 0.0, un)
        cw = cw_ref[...]
        return up * cw[0:1] + um * cw[1:2] + un * cw[2:3] + cb_ref[...]

    a = conv(wa_ref, cwa_ref, cba_ref)
    b = conv(wb_ref, cwb_ref, cbb_ref)
    act = (_silu(a) * b).astype(BF16)
    acc_ref[...] += jnp.dot(act, wd_ref[...], preferred_element_type=F32)

    @pl.when(f == pl.num_programs(1) - 1)
    def _():
        y = x_ref[...] + mod[5:6] * acc_ref[...]
        if final_norm:
            y = y * lax.rsqrt(jnp.mean(y * y, axis=-1, keepdims=True) + EPS) * fnw_ref[...]
        o_ref[...] = y


def _ffn(x2, mod, rows_per_mod, seq, nw, w_up, conv_w, conv_b, w_down, fnw, final_norm,
         tm=512, tf=512):
    t, d = x2.shape
    dff = w_down.shape[0]
    nf = dff // tf
    hb = tm // CONV_HALO
    last_hb = t // CONV_HALO - 1
    cb = conv_b.reshape(1, 2 * dff)
    return pl.pallas_call(
        functools.partial(_ffn_kernel, tm=tm, seq=seq, final_norm=final_norm),
        grid=(t // tm, nf),
        in_specs=[pl.BlockSpec((tm, d), lambda i, f: (i, 0)),
                  pl.BlockSpec((CONV_HALO, d), lambda i, f: (jnp.maximum(i * hb - 1, 0), 0)),
                  pl.BlockSpec((CONV_HALO, d), lambda i, f: (jnp.minimum((i + 1) * hb, last_hb), 0)),
                  pl.BlockSpec((1, N_MOD, d), lambda i, f: ((i * tm) // rows_per_mod, 0, 0)),
                  pl.BlockSpec((1, d), lambda i, f: (0, 0)),
                  pl.BlockSpec((d, tf), lambda i, f: (0, f)),
                  pl.BlockSpec((d, tf), lambda i, f: (0, f + nf)),
                  pl.BlockSpec((3, tf), lambda i, f: (0, f)),
                  pl.BlockSpec((3, tf), lambda i, f: (0, f + nf)),
                  pl.BlockSpec((1, tf), lambda i, f: (0, f)),
                  pl.BlockSpec((1, tf), lambda i, f: (0, f + nf)),
                  pl.BlockSpec((tf, d), lambda i, f: (f, 0)),
                  pl.BlockSpec((1, d), lambda i, f: (0, 0))],
        out_specs=pl.BlockSpec((tm, d), lambda i, f: (i, 0)),
        out_shape=jax.ShapeDtypeStruct((t, d), F32),
        scratch_shapes=[pltpu.VMEM((tm + 2 * CONV_HALO, d), BF16),
                        pltpu.VMEM((tm, d), F32)],
        compiler_params=_params(("arbitrary", "arbitrary")),
        name="ffn",
    )(x2, x2, x2, mod, nw.reshape(1, d), w_up, w_up, conv_w, conv_w, cb, cb, w_down,
      fnw.reshape(1, d))


def kernel(x_prompt, x_sample, state_hgrn, c, c_ctx, w_ada, b_ada, norm1_w, w_in, lb_param,
           hg_norm_w, w_pool, pool_scale, w_out, norm2_w, w_up, conv_w, conv_b, w_down,
           final_norm_w):
    depth = w_in.shape[0]
    bp, lp, d = x_prompt.shape
    bs, ls, _ = x_sample.shape
    hg_width = lb_param.shape[2]

    lb_all = jnp.cumsum(jax.nn.softmax(lb_param.astype(F32), axis=1), axis=1)
    n_c = 1 + bs
    pad = (-n_c) % 8
    cvecs = jnp.concatenate([c_ctx[None, :], c, jnp.zeros((pad, d), F32)], axis=0)

    xp = x_prompt.reshape(bp * lp, d)
    xs = x_sample.reshape(bs * ls, d)
    ctx_states = []
    for l in range(depth):
        last = l == depth - 1
        mods = _ada(cvecs, w_ada[l], b_ada[l])
        mod_p = mods[0:1].reshape(1, N_MOD, d)
        mod_s = mods[1:n_c].reshape(bs, N_MOD, d)
        w_in_l = w_in[l].astype(BF16)
        w_pool_l = w_pool[l].astype(BF16)
        w_out_l = w_out[l].astype(BF16)
        w_up_l = w_up[l].astype(BF16)
        w_down_l = w_down[l].astype(BF16)
        lb = lb_all[:, l]

        def layer(x2, mod, rows_per_mod, batch, seq, s0, on_grid, want_state):
            proj = _in_proj(x2, mod, rows_per_mod, norm1_w[l], w_in_l)
            o, st = _scan(proj, lb, hg_norm_w[l], s0, l, batch, seq, want_state)
            pm = _pool(proj, w_pool_l, pool_scale[l], 5 * hg_width, batch, seq, on_grid)
            x2 = _mix(o, pm, x2, mod, rows_per_mod, w_out_l)
            x2 = _ffn(x2, mod, rows_per_mod, seq, norm2_w[l], w_up_l, conv_w[l], conv_b[l],
                      w_down_l, final_norm_w, last)
            return x2, st

        xp, st = layer(xp, mod_p, bp * lp, bp, lp, None, False, True)
        ctx_states.append(st)
        xs, _ = layer(xs, mod_s, ls, bs, ls, state_hgrn, True, False)

    y_prompt = xp.reshape(bp, lp, d)
    y_sample = xs.reshape(bs, ls, d)
    new_state = jnp.stack(ctx_states, axis=1)
    return (y_prompt, y_sample, new_state)
```

```python
import functools

import jax
import jax.numpy as jnp
import numpy as np
from jax import lax
from jax.experimental import pallas as pl
from jax.experimental.pallas import tpu as pltpu

F32 = jnp.float32
BF16 = jnp.bfloat16
EPS = 1e-6

HEAD_DIM = 128
N_MOD = 6
POOL_WINDOWS = (2, 4, 8, 16)
GRID_W = 64
CONV_HALO = 8
VMEM_LIMIT = 56 * 1024 * 1024


def _params(sem):
    return pltpu.CompilerParams(dimension_semantics=sem, vmem_limit_bytes=VMEM_LIMIT)


def _silu(x):
    return x * jax.nn.sigmoid(x)


def _norm_mod(x, nw, sc, sh):
    y = x * lax.rsqrt(jnp.mean(x * x, axis=-1, keepdims=True) + EPS) * nw
    return y * (1.0 + sc) + sh


def _ada_kernel(c_ref, w_ref, b_ref, o_ref):
    s = _silu(c_ref[...]).astype(BF16)
    o_ref[...] = jnp.dot(s, w_ref[...].astype(BF16), preferred_element_type=F32) + b_ref[...]


def _ada(cvecs, w_ada, b_ada, tn=1024):
    r, d = cvecs.shape
    n = w_ada.shape[1]
    return pl.pallas_call(
        _ada_kernel,
        grid=(n // tn,),
        in_specs=[pl.BlockSpec((r, d), lambda j: (0, 0)),
                  pl.BlockSpec((d, tn), lambda j: (0, j)),
                  pl.BlockSpec((1, tn), lambda j: (0, j))],
        out_specs=pl.BlockSpec((r, tn), lambda j: (0, j)),
        out_shape=jax.ShapeDtypeStruct((r, n), F32),
        compiler_params=_params(("arbitrary",)),
        name="ada",
    )(cvecs, w_ada, b_ada.reshape(1, n))


def _in_proj_kernel(x_ref, mod_ref, nw_ref, w_ref, o_ref, h_ref):
    @pl.when(pl.program_id(1) == 0)
    def _():
        mod = mod_ref[0]
        h_ref[...] = _norm_mod(x_ref[...], nw_ref[...], mod[1:2], mod[0:1]).astype(BF16)

    o_ref[...] = jnp.dot(h_ref[...], w_ref[...], preferred_element_type=F32)


def _in_proj(x2, mod, rows_per_mod, nw, w, tm=1024, tn=1024):
    t, d = x2.shape
    n = w.shape[1]
    return pl.pallas_call(
        _in_proj_kernel,
        grid=(t // tm, n // tn),
        in_specs=[pl.BlockSpec((tm, d), lambda i, j: (i, 0)),
                  pl.BlockSpec((1, N_MOD, d), lambda i, j: ((i * tm) // rows_per_mod, 0, 0)),
                  pl.BlockSpec((1, d), lambda i, j: (0, 0)),
                  pl.BlockSpec((d, tn), lambda i, j: (0, j))],
        out_specs=pl.BlockSpec((tm, tn), lambda i, j: (i, j)),
        out_shape=jax.ShapeDtypeStruct((t, n), F32),
        scratch_shapes=[pltpu.VMEM((tm, d), BF16)],
        compiler_params=_params(("arbitrary", "arbitrary")),
        name="in_proj",
    )(x2, mod, nw.reshape(1, d), w)


LOG2E = 1.4426950408889634
TOTAL_ROWS = 16


def _decay_matrices(c):
    t = np.arange(c)
    blocks = []
    m = 1
    while m < c:
        mid = (t // (2 * m)) * (2 * m) + m
        second = (t & m) != 0
        qm = np.zeros((c, 2 * c), np.float32)
        km = np.zeros((c, 2 * c), np.float32)
        for i in range(c):
            if second[i]:
                qm[i, mid[i]:i + 1] = 1
                km[i, c + mid[i]:c + i] = 1
            else:
                qm[i, c + i:c + mid[i]] = 1
                km[i, i + 1:mid[i]] = 1
        blocks += [qm, km]
        m *= 2
    qf = np.zeros((c, 2 * c), np.float32)
    qb = np.zeros((c, 2 * c), np.float32)
    kf = np.zeros((c, 2 * c), np.float32)
    kb = np.zeros((c, c), np.float32)
    for i in range(c):
        qf[i, 0:i + 1] = 1
        qb[i, c + i:2 * c] = 1
        kf[i, i + 1:c] = 1
        kb[i, 0:i] = 1
    tf = np.zeros((TOTAL_ROWS, 2 * c), np.float32)
    tf[:, 0:c] = 1
    tb = np.ones((TOTAL_ROWS, c), np.float32)
    sweep = np.concatenate(blocks + [qf, qb, kf, tf], axis=0)
    back = np.concatenate([kb, tb], axis=0)
    return (jnp.asarray(np.concatenate([sweep, sweep], axis=1), BF16),
            jnp.asarray(np.concatenate([back, back], axis=1), BF16))


def _gates(z, lbd):
    f = lbd + (1.0 - lbd) * jax.nn.sigmoid(z)
    return jnp.log(f) * LOG2E, 1.0 - f


def _split_bf16(x):
    hi = x.astype(BF16)
    return hi, (x - hi.astype(F32)).astype(BF16)


def _dot_nt(a, b):
    return lax.dot_general(a, b, (((1,), (1,)), ((), ())), preferred_element_type=F32)


def _dot_tn(a, b):
    return lax.dot_general(a, b, (((0,), (0,)), ((), ())), preferred_element_type=F32)


def _scan_kernel(*refs, seq, c, grp, has_s0, want_state):
    q_ref, zf_ref, zb_ref, v_ref, g_ref, lb_ref, nw_ref, ms_ref, mb_ref = refs[:9]
    pos = 9
    s0_ref = None
    if has_s0:
        s0_ref = refs[pos]
        pos += 1
    o_ref = refs[pos]
    pos += 1
    sout_ref = None
    if want_state:
        sout_ref = refs[pos]
        pos += 1
    lfb_scr, kb_scr, sb_scr = refs[pos:pos + 3]

    gc = grp * c
    n_iters = seq // gc
    n_levels = c.bit_length() - 1
    lb_f = lb_ref[0:1, :]
    lb_b = lb_ref[1:2, :]
    nw = nw_ref[...]
    row = lax.broadcasted_iota(jnp.int32, (c, HEAD_DIM), 0)
    pair = (lax.broadcasted_iota(jnp.int32, (c, c), 0)
            ^ lax.broadcasted_iota(jnp.int32, (c, c), 1))
    level = jnp.full((c, c), -1, jnp.int32)
    for k in range(n_levels):
        level = level + (pair >= (1 << k)).astype(jnp.int32)

    if has_s0:
        stf0 = s0_ref[0].T
        stb0 = s0_ref[1].T
    else:
        stf0 = jnp.zeros((HEAD_DIM, HEAD_DIM), F32)
        stb0 = stf0

    def chunk_rows(x, gi):
        return x[gi * c:(gi + 1) * c]

    def chunk_lanes(x, gi):
        return x[:, gi * HEAD_DIM:(gi + 1) * HEAD_DIM]

    def stack_chunks(parts):
        return jnp.concatenate(
            [jnp.concatenate([chunk_rows(p, gi) for p in parts], axis=0) for gi in range(grp)],
            axis=1)

    def sweep_b(i, stb):
        it = n_iters - 1 - i
        rows = pl.ds(pl.multiple_of(it * gc, gc), gc)
        l2f_b, k_b = _gates(zb_ref[rows, :], lb_b)
        lfb_scr[rows, :] = l2f_b
        kb_scr[rows, :] = k_b
        v_bf = v_ref[rows, :].astype(BF16)
        e = jnp.exp2(jnp.dot(mb_ref[...], stack_chunks(_split_bf16(l2f_b)),
                             preferred_element_type=F32))
        for gi in reversed(range(grp)):
            eg = chunk_lanes(e, gi)
            sb_scr[it * grp + gi] = stb
            k_hat = (chunk_rows(k_b, gi) * eg[0:c]).astype(BF16)
            stb = stb * eg[c:c + 1] + _dot_tn(chunk_rows(v_bf, gi), k_hat)
        return stb

    stb_final = lax.fori_loop(0, n_iters, sweep_b, stb0)

    def sweep_f(it, stf):
        rows = pl.ds(pl.multiple_of(it * gc, gc), gc)
        qs_all = _silu(q_ref[rows, :])
        v_all = v_ref[rows, :]
        vbf_all = v_all.astype(BF16)
        gate_all = _silu(g_ref[rows, :])
        l2f_f, kf_all = _gates(zf_ref[rows, :], lb_f)
        kb_all = kb_scr[rows, :]
        fh, fl = _split_bf16(l2f_f)
        bh, bl = _split_bf16(lfb_scr[rows, :])
        e_all = jnp.exp2(jnp.dot(ms_ref[...], stack_chunks((fh, bh, fl, bl)),
                                 preferred_element_type=F32))
        outs = []
        for gi in range(grp):
            e = chunk_lanes(e_all, gi)
            qs = chunk_rows(qs_all, gi)
            v = chunk_rows(v_all, gi)
            v_bf = chunk_rows(vbf_all, gi)
            k_f = chunk_rows(kf_all, gi)
            k_b = chunk_rows(kb_all, gi)

            a = jnp.zeros((c, c), F32)
            for lv in range(n_levels):
                second = (row & (1 << lv)) != 0
                x = (qs * e[2 * lv * c:(2 * lv + 1) * c]).astype(BF16)
                y = (jnp.where(second, k_b, k_f) * e[(2 * lv + 1) * c:(2 * lv + 2) * c]).astype(BF16)
                a = jnp.where(level == lv, _dot_nt(x, y), a)
            base = 2 * n_levels * c
            q_f = (qs * e[base:base + c]).astype(BF16)
            q_b = (qs * e[base + c:base + 2 * c]).astype(BF16)
            k_hat = (k_f * e[base + 2 * c:base + 3 * c]).astype(BF16)
            dec = e[base + 3 * c:base + 3 * c + 1]

            o = jnp.dot(a.astype(BF16), v_bf, preferred_element_type=F32)
            o = o + jnp.sum(qs * (k_f + k_b), axis=-1, keepdims=True) * v
            o = o + _dot_nt(q_f, stf.astype(BF16))
            o = o + _dot_nt(q_b, sb_scr[it * grp + gi].astype(BF16))
            o = o * lax.rsqrt(jnp.mean(o * o, axis=-1, keepdims=True) + EPS) * nw
            outs.append(o * chunk_rows(gate_all, gi))
            stf = stf * dec + _dot_tn(v_bf, k_hat)
        o_ref[rows, :] = jnp.concatenate(outs, axis=0).astype(o_ref.dtype)
        return stf

    stf_final = lax.fori_loop(0, n_iters, sweep_f, stf0)

    if want_state:
        sout_ref[0] = stf_final.T
        sout_ref[1] = stb_final.T


def _scan(proj, lb, hg_nw, s0, layer, batch, seq, want_state, c=64, grp=4):
    t = proj.shape[0]
    n_heads = lb.shape[1] // HEAD_DIM
    has_s0 = s0 is not None
    m_sweep, m_back = _decay_matrices(c)

    def sec(k):
        return pl.BlockSpec((seq, HEAD_DIM), lambda b, h, k=k: (b, k * n_heads + h))

    state_block = (None, None, 2, None, HEAD_DIM, HEAD_DIM)
    state_map = lambda b, h: (b, layer, 0, h, 0, 0)
    in_specs = [sec(0), sec(1), sec(2), sec(3), sec(4),
                pl.BlockSpec((2, HEAD_DIM), lambda b, h: (0, h)),
                pl.BlockSpec((1, HEAD_DIM), lambda b, h: (0, 0)),
                pl.BlockSpec(m_sweep.shape, lambda b, h: (0, 0)),
                pl.BlockSpec(m_back.shape, lambda b, h: (0, 0))]
    args = [proj, proj, proj, proj, proj, lb, hg_nw.reshape(1, HEAD_DIM), m_sweep, m_back]
    if has_s0:
        in_specs.append(pl.BlockSpec(state_block, state_map))
        args.append(s0)
    out_specs = [pl.BlockSpec((seq, HEAD_DIM), lambda b, h: (b, h))]
    out_shape = [jax.ShapeDtypeStruct((t, n_heads * HEAD_DIM), BF16)]
    if want_state:
        out_specs.append(pl.BlockSpec((None, 2, None, HEAD_DIM, HEAD_DIM),
                                      lambda b, h: (b, 0, h, 0, 0)))
        out_shape.append(jax.ShapeDtypeStruct((batch, 2, n_heads, HEAD_DIM, HEAD_DIM), F32))
    res = pl.pallas_call(
        functools.partial(_scan_kernel, seq=seq, c=c, grp=grp, has_s0=has_s0,
                          want_state=want_state),
        grid=(batch, n_heads),
        in_specs=in_specs,
        out_specs=out_specs,
        out_shape=out_shape,
        scratch_shapes=[pltpu.VMEM((seq, HEAD_DIM), F32),
                        pltpu.VMEM((seq, HEAD_DIM), F32),
                        pltpu.VMEM((seq // c, HEAD_DIM, HEAD_DIM), F32)],
        compiler_params=_params(("arbitrary", "arbitrary")),
        name="scan",
    )(*args)
    return (res[0], res[1]) if want_state else (res[0], None)


def _window_mean(a, w, pos, n, stride):
    rows = a.shape[0]
    half = w // 2
    fwd = a
    ln = 1
    while ln < half:
        fwd = fwd + jnp.where(pos < n - ln, pltpu.roll(fwd, rows - ln * stride, 0), 0.0)
        ln *= 2
    bwd = jnp.where(pos >= 1, pltpu.roll(a, stride, 0), 0.0)
    ln = 1
    while ln < half:
        bwd = bwd + jnp.where(pos >= ln, pltpu.roll(bwd, ln * stride, 0), 0.0)
        ln *= 2
    cnt = jnp.minimum(pos + (w - half), n) - jnp.maximum(pos - half, 0)
    return (fwd + bwd) / cnt.astype(F32)


def _pool_kernel(p_ref, w_ref, sc_ref, o_ref, *, seq, on_grid):
    grp = pl.program_id(1)
    for gi, w in enumerate(POOL_WINDOWS):
        @pl.when(grp == gi)
        def _(w=w):
            pg = p_ref[...]
            row = lax.broadcasted_iota(jnp.int32, pg.shape, 0)
            if on_grid:
                m = _window_mean(pg, w, row % GRID_W, GRID_W, 1)
                m = _window_mean(m, w, row // GRID_W, seq // GRID_W, GRID_W)
            else:
                m = _window_mean(pg, w, row, seq, 1)
            d = (m - pg).astype(BF16)
            y = jnp.dot(d, w_ref[...], preferred_element_type=F32) * sc_ref[...]
            o_ref[...] = y.astype(o_ref.dtype)


def _pool(proj, w_pool_l, pool_scale_l, col0, batch, seq, on_grid):
    t = proj.shape[0]
    n_grp, gw, _ = w_pool_l.shape
    blk0 = col0 // gw
    return pl.pallas_call(
        functools.partial(_pool_kernel, seq=seq, on_grid=on_grid),
        grid=(batch, n_grp),
        in_specs=[pl.BlockSpec((seq, gw), lambda b, g: (b, blk0 + g)),
                  pl.BlockSpec((None, gw, gw), lambda b, g: (g, 0, 0)),
                  pl.BlockSpec((1, gw), lambda b, g: (0, g))],
        out_specs=pl.BlockSpec((seq, gw), lambda b, g: (b, g)),
        out_shape=jax.ShapeDtypeStruct((t, n_grp * gw), BF16),
        compiler_params=_params(("arbitrary", "arbitrary")),
        name="pool",
    )(proj, w_pool_l, pool_scale_l.reshape(1, n_grp * gw))


def _mix_kernel(o_ref, pm_ref, x_ref, mod_ref, w_ref, out_ref):
    k = o_ref.shape[1]
    mix = (jnp.dot(o_ref[...], w_ref[0:k, :], preferred_element_type=F32)
           + jnp.dot(pm_ref[...], w_ref[k:, :], preferred_element_type=F32))
    out_ref[...] = x_ref[...] + mod_ref[0][2:3] * mix


def _mix(o, pm, x2, mod, rows_per_mod, w_out, tm=512):
    t, d = x2.shape
    return pl.pallas_call(
        _mix_kernel,
        grid=(t // tm,),
        in_specs=[pl.BlockSpec((tm, o.shape[1]), lambda i: (i, 0)),
                  pl.BlockSpec((tm, pm.shape[1]), lambda i: (i, 0)),
                  pl.BlockSpec((tm, d), lambda i: (i, 0)),
                  pl.BlockSpec((1, N_MOD, d), lambda i: ((i * tm) // rows_per_mod, 0, 0)),
                  pl.BlockSpec(w_out.shape, lambda i: (0, 0))],
        out_specs=pl.BlockSpec((tm, d), lambda i: (i, 0)),
        out_shape=jax.ShapeDtypeStruct((t, d), F32),
        compiler_params=_params(("arbitrary",)),
        name="mix",
    )(o, pm, x2, mod, w_out)


def _ffn_kernel(x_ref, xp_ref, xn_ref, mod_ref, nw_ref, wa_ref, wb_ref, cwa_ref, cwb_ref,
                cba_ref, cbb_ref, wd_ref, fnw_ref, o_ref, h_ref, acc_ref,
                *, tm, seq, final_norm):
    i = pl.program_id(0)
    f = pl.program_id(1)
    mod = mod_ref[0]

    @pl.when(f == 0)
    def _():
        nw = nw_ref[...]
        sc, sh = mod[4:5], mod[3:4]
        h_ref[0:tm, :] = _norm_mod(x_ref[...], nw, sc, sh).astype(BF16)
        h_ref[tm:tm + CONV_HALO, :] = _norm_mod(xp_ref[...], nw, sc, sh).astype(BF16)
        h_ref[tm + CONV_HALO:, :] = _norm_mod(xn_ref[...], nw, sc, sh).astype(BF16)
        acc_ref[...] = jnp.zeros_like(acc_ref)

    h = h_ref[...]
    tf = wa_ref.shape[1]
    local = lax.broadcasted_iota(jnp.int32, (tm, tf), 0)
    in_seq = (i * tm + local) % seq

    def conv(w_ref, cw_ref, cb_ref):
        u = jnp.dot(h, w_ref[...], preferred_element_type=F32)
        um = u[0:tm]
        before = u[tm + CONV_HALO - 1:tm + CONV_HALO]
        after = u[tm + CONV_HALO:tm + CONV_HALO + 1]
        up = jnp.where(local == 0, before, pltpu.roll(um, 1, 0))
        up = jnp.where(in_seq == 0, 0.0, up)
        un = jnp.where(local == tm - 1, after, pltpu.roll(um, tm - 1, 0))
        un = jnp.where(in_seq == seq - 1, 0.0, un)
        cw = cw_ref[...]
        return up * cw[0:1] + um * cw[1:2] + un * cw[2:3] + cb_ref[...]

    a = conv(wa_ref, cwa_ref, cba_ref)
    b = conv(wb_ref, cwb_ref, cbb_ref)
    act = (_silu(a) * b).astype(BF16)
    acc_ref[...] += jnp.dot(act, wd_ref[...], preferred_element_type=F32)

    @pl.when(f == pl.num_programs(1) - 1)
    def _():
        y = x_ref[...] + mod[5:6] * acc_ref[...]
        if final_norm:
            y = y * lax.rsqrt(jnp.mean(y * y, axis=-1, keepdims=True) + EPS) * fnw_ref[...]
        o_ref[...] = y


def _ffn(x2, mod, rows_per_mod, seq, nw, w_up, conv_w, conv_b, w_down, fnw, final_norm,
         tm=512, tf=512):
    t, d = x2.shape
    dff = w_down.shape[0]
    nf = dff // tf
    hb = tm // CONV_HALO
    last_hb = t // CONV_HALO - 1
    cb = conv_b.reshape(1, 2 * dff)
    return pl.pallas_call(
        functools.partial(_ffn_kernel, tm=tm, seq=seq, final_norm=final_norm),
        grid=(t // tm, nf),
        in_specs=[pl.BlockSpec((tm, d), lambda i, f: (i, 0)),
                  pl.BlockSpec((CONV_HALO, d), lambda i, f: (jnp.maximum(i * hb - 1, 0), 0)),
                  pl.BlockSpec((CONV_HALO, d), lambda i, f: (jnp.minimum((i + 1) * hb, last_hb), 0)),
                  pl.BlockSpec((1, N_MOD, d), lambda i, f: ((i * tm) // rows_per_mod, 0, 0)),
                  pl.BlockSpec((1, d), lambda i, f: (0, 0)),
                  pl.BlockSpec((d, tf), lambda i, f: (0, f)),
                  pl.BlockSpec((d, tf), lambda i, f: (0, f + nf)),
                  pl.BlockSpec((3, tf), lambda i, f: (0, f)),
                  pl.BlockSpec((3, tf), lambda i, f: (0, f + nf)),
                  pl.BlockSpec((1, tf), lambda i, f: (0, f)),
                  pl.BlockSpec((1, tf), lambda i, f: (0, f + nf)),
                  pl.BlockSpec((tf, d), lambda i, f: (f, 0)),
                  pl.BlockSpec((1, d), lambda i, f: (0, 0))],
        out_specs=pl.BlockSpec((tm, d), lambda i, f: (i, 0)),
        out_shape=jax.ShapeDtypeStruct((t, d), F32),
        scratch_shapes=[pltpu.VMEM((tm + 2 * CONV_HALO, d), BF16),
                        pltpu.VMEM((tm, d), F32)],
        compiler_params=_params(("arbitrary", "arbitrary")),
        name="ffn",
    )(x2, x2, x2, mod, nw.reshape(1, d), w_up, w_up, conv_w, conv_w, cb, cb, w_down,
      fnw.reshape(1, d))


def kernel(x_prompt, x_sample, state_hgrn, c, c_ctx, w_ada, b_ada, norm1_w, w_in, lb_param,
           hg_norm_w, w_pool, pool_scale, w_out, norm2_w, w_up, conv_w, conv_b, w_down,
           final_norm_w):
    depth = w_in.shape[0]
    bp, lp, d = x_prompt.shape
    bs, ls, _ = x_sample.shape
    hg_width = lb_param.shape[2]

    lb_all = jnp.cumsum(jax.nn.softmax(lb_param.astype(F32), axis=1), axis=1)
    n_c = 1 + bs
    pad = (-n_c) % 8
    cvecs = jnp.concatenate([c_ctx[None, :], c, jnp.zeros((pad, d), F32)], axis=0)

    xp = x_prompt.reshape(bp * lp, d)
    xs = x_sample.reshape(bs * ls, d)
    ctx_states = []
    for l in range(depth):
        last = l == depth - 1
        mods = _ada(cvecs, w_ada[l], b_ada[l])
        mod_p = mods[0:1].reshape(1, N_MOD, d)
        mod_s = mods[1:n_c].reshape(bs, N_MOD, d)
        w_in_l = w_in[l].astype(BF16)
        w_pool_l = w_pool[l].astype(BF16)
        w_out_l = w_out[l].astype(BF16)
        w_up_l = w_up[l].astype(BF16)
        w_down_l = w_down[l].astype(BF16)
        lb = lb_all[:, l]

        def layer(x2, mod, rows_per_mod, batch, seq, s0, on_grid, want_state):
            proj = _in_proj(x2, mod, rows_per_mod, norm1_w[l], w_in_l)
            o, st = _scan(proj, lb, hg_norm_w[l], s0, l, batch, seq, want_state)
            pm = _pool(proj, w_pool_l, pool_scale[l], 5 * hg_width, batch, seq, on_grid)
            x2 = _mix(o, pm, x2, mod, rows_per_mod, w_out_l)
            x2 = _ffn(x2, mod, rows_per_mod, seq, norm2_w[l], w_up_l, conv_w[l], conv_b[l],
                      w_down_l, final_norm_w, last)
            return x2, st

        xp, st = layer(xp, mod_p, bp * lp, bp, lp, None, False, True)
        ctx_states.append(st)
        xs, _ = layer(xs, mod_s, ls, bs, ls, state_hgrn, True, False)

    y_prompt = xp.reshape(bp, lp, d)
    y_sample = xs.reshape(bs, ls, d)
    new_state = jnp.stack(ctx_states, axis=1)
    return (y_prompt, y_sample, new_state)
```

```python
import functools

import jax
import jax.numpy as jnp
import numpy as np
from jax import lax
from jax.experimental import pallas as pl
from jax.experimental.pallas import tpu as pltpu

F32 = jnp.float32
BF16 = jnp.bfloat16
EPS = 1e-6

HEAD_DIM = 128
N_MOD = 6
POOL_WINDOWS = (2, 4, 8, 16)
GRID_W = 64
CONV_HALO = 8
VMEM_LIMIT = 56 * 1024 * 1024


def _params(sem):
    return pltpu.CompilerParams(dimension_semantics=sem, vmem_limit_bytes=VMEM_LIMIT)


def _silu(x):
    return x * jax.nn.sigmoid(x)


def _norm_mod(x, nw, sc, sh):
    y = x * lax.rsqrt(jnp.mean(x * x, axis=-1, keepdims=True) + EPS) * nw
    return y * (1.0 + sc) + sh


def _ada_kernel(c_ref, w_ref, b_ref, o_ref):
    s = _silu(c_ref[...]).astype(BF16)
    o_ref[...] = jnp.dot(s, w_ref[...].astype(BF16), preferred_element_type=F32) + b_ref[...]


def _ada(cvecs, w_ada, b_ada, tn=1024):
    r, d = cvecs.shape
    n = w_ada.shape[1]
    return pl.pallas_call(
        _ada_kernel,
        grid=(n // tn,),
        in_specs=[pl.BlockSpec((r, d), lambda j: (0, 0)),
                  pl.BlockSpec((d, tn), lambda j: (0, j)),
                  pl.BlockSpec((1, tn), lambda j: (0, j))],
        out_specs=pl.BlockSpec((r, tn), lambda j: (0, j)),
        out_shape=jax.ShapeDtypeStruct((r, n), F32),
        compiler_params=_params(("arbitrary",)),
        name="ada",
    )(cvecs, w_ada, b_ada.reshape(1, n))


def _in_proj_kernel(x_ref, mod_ref, nw_ref, w_ref, o_ref, h_ref):
    @pl.when(pl.program_id(1) == 0)
    def _():
        mod = mod_ref[0]
        h_ref[...] = _norm_mod(x_ref[...], nw_ref[...], mod[1:2], mod[0:1]).astype(BF16)

    o_ref[...] = jnp.dot(h_ref[...], w_ref[...], preferred_element_type=F32)


def _in_proj(x2, mod, rows_per_mod, nw, w, tm=1024, tn=1024):
    t, d = x2.shape
    n = w.shape[1]
    return pl.pallas_call(
        _in_proj_kernel,
        grid=(t // tm, n // tn),
        in_specs=[pl.BlockSpec((tm, d), lambda i, j: (i, 0)),
                  pl.BlockSpec((1, N_MOD, d), lambda i, j: ((i * tm) // rows_per_mod, 0, 0)),
                  pl.BlockSpec((1, d), lambda i, j: (0, 0)),
                  pl.BlockSpec((d, tn), lambda i, j: (0, j))],
        out_specs=pl.BlockSpec((tm, tn), lambda i, j: (i, j)),
        out_shape=jax.ShapeDtypeStruct((t, n), F32),
        scratch_shapes=[pltpu.VMEM((tm, d), BF16)],
        compiler_params=_params(("arbitrary", "arbitrary")),
        name="in_proj",
    )(x2, mod, nw.reshape(1, d), w)


LOG2E = 1.4426950408889634
TOTAL_ROWS = 16


def _decay_matrices(c):
    t = np.arange(c)
    blocks = []
    m = 1
    while m < c:
        mid = (t // (2 * m)) * (2 * m) + m
        second = (t & m) != 0
        qm = np.zeros((c, 2 * c), np.float32)
        km = np.zeros((c, 2 * c), np.float32)
        for i in range(c):
            if second[i]:
                qm[i, mid[i]:i + 1] = 1
                km[i, c + mid[i]:c + i] = 1
            else:
                qm[i, c + i:c + mid[i]] = 1
                km[i, i + 1:mid[i]] = 1
        blocks += [qm, km]
        m *= 2
    qf = np.zeros((c, 2 * c), np.float32)
    qb = np.zeros((c, 2 * c), np.float32)
    kf = np.zeros((c, 2 * c), np.float32)
    kb = np.zeros((c, c), np.float32)
    for i in range(c):
        qf[i, 0:i + 1] = 1
        qb[i, c + i:2 * c] = 1
        kf[i, i + 1:c] = 1
        kb[i, 0:i] = 1
    tf = np.zeros((TOTAL_ROWS, 2 * c), np.float32)
    tf[:, 0:c] = 1
    tb = np.ones((TOTAL_ROWS, c), np.float32)
    sweep = np.concatenate(blocks + [qf, qb, kf, tf], axis=0)
    back = np.concatenate([kb, tb], axis=0)
    return (jnp.asarray(np.concatenate([sweep, sweep], axis=1), BF16),
            jnp.asarray(np.concatenate([back, back], axis=1), BF16))


def _gates(z, lbd):
    f = lbd + (1.0 - lbd) * jax.nn.sigmoid(z)
    return jnp.log(f) * LOG2E, 1.0 - f


def _split_bf16(x):
    hi = x.astype(BF16)
    return hi, (x - hi.astype(F32)).astype(BF16)


def _dot_nt(a, b):
    return lax.dot_general(a, b, (((1,), (1,)), ((), ())), preferred_element_type=F32)


def _dot_tn(a, b):
    return lax.dot_general(a, b, (((0,), (0,)), ((), ())), preferred_element_type=F32)


def _scan_kernel(*refs, seq, c, grp, has_s0, want_state):
    q_ref, zf_ref, zb_ref, v_ref, g_ref, lb_ref, nw_ref, ms_ref, mb_ref = refs[:9]
    pos = 9
    s0_ref = None
    if has_s0:
        s0_ref = refs[pos]
        pos += 1
    o_ref = refs[pos]
    pos += 1
    sout_ref = None
    if want_state:
        sout_ref = refs[pos]
        pos += 1
    lfb_scr, kb_scr, sb_scr = refs[pos:pos + 3]

    gc = grp * c
    n_iters = seq // gc
    n_levels = c.bit_length() - 1
    lb_f = lb_ref[0:1, :]
    lb_b = lb_ref[1:2, :]
    nw = nw_ref[...]
    row = lax.broadcasted_iota(jnp.int32, (c, HEAD_DIM), 0)
    pair = (lax.broadcasted_iota(jnp.int32, (c, c), 0)
            ^ lax.broadcasted_iota(jnp.int32, (c, c), 1))
    level = jnp.full((c, c), -1, jnp.int32)
    for k in range(n_levels):
        level = level + (pair >= (1 << k)).astype(jnp.int32)

    if has_s0:
        stf0 = s0_ref[0].T
        stb0 = s0_ref[1].T
    else:
        stf0 = jnp.zeros((HEAD_DIM, HEAD_DIM), F32)
        stb0 = stf0

    def chunk_rows(x, gi):
        return x[gi * c:(gi + 1) * c]

    def chunk_lanes(x, gi):
        return x[:, gi * HEAD_DIM:(gi + 1) * HEAD_DIM]

    def stack_chunks(parts):
        return jnp.concatenate(
            [jnp.concatenate([chunk_rows(p, gi) for p in parts], axis=0) for gi in range(grp)],
            axis=1)

    def sweep_b(i, stb):
        it = n_iters - 1 - i
        rows = pl.ds(pl.multiple_of(it * gc, gc), gc)
        l2f_b, k_b = _gates(zb_ref[rows, :], lb_b)
        lfb_scr[rows, :] = l2f_b
        kb_scr[rows, :] = k_b
        v_bf = v_ref[rows, :].astype(BF16)
        e = jnp.exp2(jnp.dot(mb_ref[...], stack_chunks(_split_bf16(l2f_b)),
                             preferred_element_type=F32))
        for gi in reversed(range(grp)):
            eg = chunk_lanes(e, gi)
            sb_scr[it * grp + gi] = stb
            k_hat = (chunk_rows(k_b, gi) * eg[0:c]).astype(BF16)
            stb = stb * eg[c:c + 1] + _dot_tn(chunk_rows(v_bf, gi), k_hat)
        return stb

    stb_final = lax.fori_loop(0, n_iters, sweep_b, stb0)

    def sweep_f(it, stf):
        rows = pl.ds(pl.multiple_of(it * gc, gc), gc)
        qs_all = _silu(q_ref[rows, :])
        v_all = v_ref[rows, :]
        vbf_all = v_all.astype(BF16)
        gate_all = _silu(g_ref[rows, :])
        l2f_f, kf_all = _gates(zf_ref[rows, :], lb_f)
        kb_all = kb_scr[rows, :]
        fh, fl = _split_bf16(l2f_f)
        bh, bl = _split_bf16(lfb_scr[rows, :])
        e_all = jnp.exp2(jnp.dot(ms_ref[...], stack_chunks((fh, bh, fl, bl)),
                                 preferred_element_type=F32))
        outs = []
        for gi in range(grp):
            e = chunk_lanes(e_all, gi)
            qs = chunk_rows(qs_all, gi)
            v = chunk_rows(v_all, gi)
            v_bf = chunk_rows(vbf_all, gi)
            k_f = chunk_rows(kf_all, gi)
            k_b = chunk_rows(kb_all, gi)

            a = jnp.zeros((c, c), F32)
            for lv in range(n_levels):
                second = (row & (1 << lv)) != 0
                x = (qs * e[2 * lv * c:(2 * lv + 1) * c]).astype(BF16)
                y = (jnp.where(second, k_b, k_f) * e[(2 * lv + 1) * c:(2 * lv + 2) * c]).astype(BF16)
                a = jnp.where(level == lv, _dot_nt(x, y), a)
            base = 2 * n_levels * c
            q_f = (qs * e[base:base + c]).astype(BF16)
            q_b = (qs * e[base + c:base + 2 * c]).astype(BF16)
            k_hat = (k_f * e[base + 2 * c:base + 3 * c]).astype(BF16)
            dec = e[base + 3 * c:base + 3 * c + 1]

            o = jnp.dot(a.astype(BF16), v_bf, preferred_element_type=F32)
            o = o + jnp.sum(qs * (k_f + k_b), axis=-1, keepdims=True) * v
            o = o + _dot_nt(q_f, stf.astype(BF16))
            o = o + _dot_nt(q_b, sb_scr[it * grp + gi].astype(BF16))
            o = o * lax.rsqrt(jnp.mean(o * o, axis=-1, keepdims=True) + EPS) * nw
            outs.append(o * chunk_rows(gate_all, gi))
            stf = stf * dec + _dot_tn(v_bf, k_hat)
        o_ref[rows, :] = jnp.concatenate(outs, axis=0).astype(o_ref.dtype)
        return stf

    stf_final = lax.fori_loop(0, n_iters, sweep_f, stf0)

    if want_state:
        sout_ref[0] = stf_final.T
        sout_ref[1] = stb_final.T


def _scan(proj, lb, hg_nw, s0, layer, batch, seq, want_state, c=64, grp=4):
    t = proj.shape[0]
    n_heads = lb.shape[1] // HEAD_DIM
    has_s0 = s0 is not None
    m_sweep, m_back = _decay_matrices(c)

    def sec(k):
        return pl.BlockSpec((seq, HEAD_DIM), lambda b, h, k=k: (b, k * n_heads + h))

    state_block = (None, None, 2, None, HEAD_DIM, HEAD_DIM)
    state_map = lambda b, h: (b, layer, 0, h, 0, 0)
    in_specs = [sec(0), sec(1), sec(2), sec(3), sec(4),
                pl.BlockSpec((2, HEAD_DIM), lambda b, h: (0, h)),
                pl.BlockSpec((1, HEAD_DIM), lambda b, h: (0, 0)),
                pl.BlockSpec(m_sweep.shape, lambda b, h: (0, 0)),
                pl.BlockSpec(m_back.shape, lambda b, h: (0, 0))]
    args = [proj, proj, proj, proj, proj, lb, hg_nw.reshape(1, HEAD_DIM), m_sweep, m_back]
    if has_s0:
        in_specs.append(pl.BlockSpec(state_block, state_map))
        args.append(s0)
    out_specs = [pl.BlockSpec((seq, HEAD_DIM), lambda b, h: (b, h))]
    out_shape = [jax.ShapeDtypeStruct((t, n_heads * HEAD_DIM), BF16)]
    if want_state:
        out_specs.append(pl.BlockSpec((None, 2, None, HEAD_DIM, HEAD_DIM),
                                      lambda b, h: (b, 0, h, 0, 0)))
        out_shape.append(jax.ShapeDtypeStruct((batch, 2, n_heads, HEAD_DIM, HEAD_DIM), F32))
    res = pl.pallas_call(
        functools.partial(_scan_kernel, seq=seq, c=c, grp=grp, has_s0=has_s0,
                          want_state=want_state),
        grid=(batch, n_heads),
        in_specs=in_specs,
        out_specs=out_specs,
        out_shape=out_shape,
        scratch_shapes=[pltpu.VMEM((seq, HEAD_DIM), F32),
                        pltpu.VMEM((seq, HEAD_DIM), F32),
                        pltpu.VMEM((seq // c, HEAD_DIM, HEAD_DIM), F32)],
        compiler_params=_params(("arbitrary", "arbitrary")),
        name="scan",
    )(*args)
    return (res[0], res[1]) if want_state else (res[0], None)


def _window_mean(a, w, pos, n, stride):
    rows = a.shape[0]
    half = w // 2
    fwd = a
    ln = 1
    while ln < half:
        fwd = fwd + jnp.where(pos < n - ln, pltpu.roll(fwd, rows - ln * stride, 0), 0.0)
        ln *= 2
    bwd = jnp.where(pos >= 1, pltpu.roll(a, stride, 0), 0.0)
    ln = 1
    while ln < half:
        bwd = bwd + jnp.where(pos >= ln, pltpu.roll(bwd, ln * stride, 0), 0.0)
        ln *= 2
    cnt = jnp.minimum(pos + (w - half), n) - jnp.maximum(pos - half, 0)
    return (fwd + bwd) / cnt.astype(F32)


def _pool_kernel(p_ref, w_ref, sc_ref, o_ref, *, seq, on_grid):
    grp = pl.program_id(1)
    for gi, w in enumerate(POOL_WINDOWS):
        @pl.when(grp == gi)
        def _(w=w):
            pg = p_ref[...]
            row = lax.broadcasted_iota(jnp.int32, pg.shape, 0)
            if on_grid:
                m = _window_mean(pg, w, row % GRID_W, GRID_W, 1)
                m = _window_mean(m, w, row // GRID_W, seq // GRID_W, GRID_W)
            else:
                m = _window_mean(pg, w, row, seq, 1)
            d = (m - pg).astype(BF16)
            y = jnp.dot(d, w_ref[...], preferred_element_type=F32) * sc_ref[...]
            o_ref[...] = y.astype(o_ref.dtype)


def _pool(proj, w_pool_l, pool_scale_l, col0, batch, seq, on_grid):
    t = proj.shape[0]
    n_grp, gw, _ = w_pool_l.shape
    blk0 = col0 // gw
    return pl.pallas_call(
        functools.partial(_pool_kernel, seq=seq, on_grid=on_grid),
        grid=(batch, n_grp),
        in_specs=[pl.BlockSpec((seq, gw), lambda b, g: (b, blk0 + g)),
                  pl.BlockSpec((None, gw, gw), lambda b, g: (g, 0, 0)),
                  pl.BlockSpec((1, gw), lambda b, g: (0, g))],
        out_specs=pl.BlockSpec((seq, gw), lambda b, g: (b, g)),
        out_shape=jax.ShapeDtypeStruct((t, n_grp * gw), BF16),
        compiler_params=_params(("arbitrary", "arbitrary")),
        name="pool",
    )(proj, w_pool_l, pool_scale_l.reshape(1, n_grp * gw))


def _mix_kernel(o_ref, pm_ref, x_ref, mod_ref, w_ref, out_ref):
    k = o_ref.shape[1]
    mix = (jnp.dot(o_ref[...], w_ref[0:k, :], preferred_element_type=F32)
           + jnp.dot(pm_ref[...], w_ref[k:, :], preferred_element_type=F32))
    out_ref[...] = x_ref[...] + mod_ref[0][2:3] * mix


def _mix(o, pm, x2, mod, rows_per_mod, w_out, tm=512):
    t, d = x2.shape
    return pl.pallas_call(
        _mix_kernel,
        grid=(t // tm,),
        in_specs=[pl.BlockSpec((tm, o.shape[1]), lambda i: (i, 0)),
                  pl.BlockSpec((tm, pm.shape[1]), lambda i: (i, 0)),
                  pl.BlockSpec((tm, d), lambda i: (i, 0)),
                  pl.BlockSpec((1, N_MOD, d), lambda i: ((i * tm) // rows_per_mod, 0, 0)),
                  pl.BlockSpec(w_out.shape, lambda i: (0, 0))],
        out_specs=pl.BlockSpec((tm, d), lambda i: (i, 0)),
        out_shape=jax.ShapeDtypeStruct((t, d), F32),
        compiler_params=_params(("arbitrary",)),
        name="mix",
    )(o, pm, x2, mod, w_out)


def _tile_up_weights(w_up, conv_w, conv_b, tf):
    d, two_dff = w_up.shape
    nf = two_dff // 2 // tf

    def regroup(a):
        r = a.shape[0]
        return a.reshape(r, 2, nf, tf).transpose(2, 0, 1, 3).reshape(nf, r, 2 * tf)

    return regroup(w_up.astype(BF16)), regroup(conv_w), regroup(conv_b.reshape(1, two_dff))


def _ffn_up_kernel(x_ref, xp_ref, xn_ref, mod_ref, nw_ref, w_ref, cw_ref, cb_ref, o_ref, h_ref,
                   *, tm, seq):
    i = pl.program_id(0)

    @pl.when(pl.program_id(1) == 0)
    def _():
        mod = mod_ref[0]
        nw = nw_ref[...]
        sc, sh = mod[4:5], mod[3:4]
        starts_seq = (i * tm) % seq == 0
        ends_seq = ((i + 1) * tm) % seq == 0
        above = jnp.where(starts_seq, 0.0, _norm_mod(xp_ref[...], nw, sc, sh))
        below = jnp.where(ends_seq, 0.0, _norm_mod(xn_ref[...], nw, sc, sh))
        h_ref[0:CONV_HALO, :] = above.astype(BF16)
        h_ref[CONV_HALO:CONV_HALO + tm, :] = _norm_mod(x_ref[...], nw, sc, sh).astype(BF16)
        h_ref[CONV_HALO + tm:, :] = below.astype(BF16)

    ext = tm + 2 * CONV_HALO
    u = jnp.dot(h_ref[...], w_ref[...], preferred_element_type=F32)
    um = u[CONV_HALO:CONV_HALO + tm]
    up = pltpu.roll(u, 1, 0)[CONV_HALO:CONV_HALO + tm]
    un = pltpu.roll(u, ext - 1, 0)[CONV_HALO:CONV_HALO + tm]
    if tm > seq:
        in_seq = lax.broadcasted_iota(jnp.int32, um.shape, 0) % seq
        up = jnp.where(in_seq == 0, 0.0, up)
        un = jnp.where(in_seq == seq - 1, 0.0, un)
    cw = cw_ref[...]
    y = up * cw[0:1] + um * cw[1:2] + un * cw[2:3] + cb_ref[...]
    tf = o_ref.shape[1]
    o_ref[...] = (_silu(y[:, :tf]) * y[:, tf:]).astype(o_ref.dtype)


def _ffn_down_kernel(a_ref, x_ref, mod_ref, w_ref, fnw_ref, o_ref, *, final_norm):
    y = x_ref[...] + mod_ref[0][5:6] * jnp.dot(a_ref[...], w_ref[...],
                                               preferred_element_type=F32)
    if final_norm:
        y = y * lax.rsqrt(jnp.mean(y * y, axis=-1, keepdims=True) + EPS) * fnw_ref[...]
    o_ref[...] = y


def _ffn(x2, mod, rows_per_mod, seq, nw, up_tiles, w_down, fnw, final_norm, tm=512, tm_down=256):
    t, d = x2.shape
    w_up_t, cw_t, cb_t = up_tiles
    nf, _, tf2 = w_up_t.shape
    tf = tf2 // 2
    dff = nf * tf
    assert seq % tm == 0 or tm % seq == 0
    hb = tm // CONV_HALO
    last_hb = t // CONV_HALO - 1
    mod_spec = lambda tile: pl.BlockSpec((1, N_MOD, d),
                                         lambda i, *_: ((i * tile) // rows_per_mod, 0, 0))
    act = pl.pallas_call(
        functools.partial(_ffn_up_kernel, tm=tm, seq=seq),
        grid=(t // tm, nf),
        in_specs=[pl.BlockSpec((tm, d), lambda i, f: (i, 0)),
                  pl.BlockSpec((CONV_HALO, d), lambda i, f: (jnp.maximum(i * hb - 1, 0), 0)),
                  pl.BlockSpec((CONV_HALO, d), lambda i, f: (jnp.minimum((i + 1) * hb, last_hb), 0)),
                  mod_spec(tm),
                  pl.BlockSpec((1, d), lambda i, f: (0, 0)),
                  pl.BlockSpec((None, d, tf2), lambda i, f: (f, 0, 0)),
                  pl.BlockSpec((None, 3, tf2), lambda i, f: (f, 0, 0)),
                  pl.BlockSpec((None, 1, tf2), lambda i, f: (f, 0, 0))],
        out_specs=pl.BlockSpec((tm, tf), lambda i, f: (i, f)),
        out_shape=jax.ShapeDtypeStruct((t, dff), BF16),
        scratch_shapes=[pltpu.VMEM((tm + 2 * CONV_HALO, d), BF16)],
        compiler_params=_params(("arbitrary", "arbitrary")),
        name="ffn_up",
    )(x2, x2, x2, mod, nw.reshape(1, d), w_up_t, cw_t, cb_t)
    return pl.pallas_call(
        functools.partial(_ffn_down_kernel, final_norm=final_norm),
        grid=(t // tm_down,),
        in_specs=[pl.BlockSpec((tm_down, dff), lambda i: (i, 0)),
                  pl.BlockSpec((tm_down, d), lambda i: (i, 0)),
                  mod_spec(tm_down),
                  pl.BlockSpec((dff, d), lambda i: (0, 0), pipeline_mode=pl.Buffered(1)),
                  pl.BlockSpec((1, d), lambda i: (0, 0))],
        out_specs=pl.BlockSpec((tm_down, d), lambda i: (i, 0)),
        out_shape=jax.ShapeDtypeStruct((t, d), F32),
        compiler_params=_params(("arbitrary",)),
        name="ffn_down",
    )(act, x2, mod, w_down, fnw.reshape(1, d))


def kernel(x_prompt, x_sample, state_hgrn, c, c_ctx, w_ada, b_ada, norm1_w, w_in, lb_param,
           hg_norm_w, w_pool, pool_scale, w_out, norm2_w, w_up, conv_w, conv_b, w_down,
           final_norm_w):
    depth = w_in.shape[0]
    bp, lp, d = x_prompt.shape
    bs, ls, _ = x_sample.shape
    hg_width = lb_param.shape[2]

    lb_all = jnp.cumsum(jax.nn.softmax(lb_param.astype(F32), axis=1), axis=1)
    n_c = 1 + bs
    pad = (-n_c) % 8
    cvecs = jnp.concatenate([c_ctx[None, :], c, jnp.zeros((pad, d), F32)], axis=0)

    xp = x_prompt.reshape(bp * lp, d)
    xs = x_sample.reshape(bs * ls, d)
    ctx_states = []
    for l in range(depth):
        last = l == depth - 1
        mods = _ada(cvecs, w_ada[l], b_ada[l])
        mod_p = mods[0:1].reshape(1, N_MOD, d)
        mod_s = mods[1:n_c].reshape(bs, N_MOD, d)
        w_in_l = w_in[l].astype(BF16)
        w_pool_l = w_pool[l].astype(BF16)
        w_out_l = w_out[l].astype(BF16)
        up_tiles = _tile_up_weights(w_up[l], conv_w[l], conv_b[l], 512)
        w_down_l = w_down[l].astype(BF16)
        lb = lb_all[:, l]

        def layer(x2, mod, rows_per_mod, batch, seq, s0, on_grid, want_state):
            proj = _in_proj(x2, mod, rows_per_mod, norm1_w[l], w_in_l)
            o, st = _scan(proj, lb, hg_norm_w[l], s0, l, batch, seq, want_state)
            pm = _pool(proj, w_pool_l, pool_scale[l], 5 * hg_width, batch, seq, on_grid)
            x2 = _mix(o, pm, x2, mod, rows_per_mod, w_out_l)
            x2 = _ffn(x2, mod, rows_per_mod, seq, norm2_w[l], up_tiles, w_down_l, final_norm_w,
                      last)
            return x2, st

        xp, st = layer(xp, mod_p, bp * lp, bp, lp, None, False, True)
        ctx_states.append(st)
        xs, _ = layer(xs, mod_s, ls, bs, ls, state_hgrn, True, False)

    y_prompt = xp.reshape(bp, lp, d)
    y_sample = xs.reshape(bs, ls, d)
    new_state = jnp.stack(ctx_states, axis=1)
    return (y_prompt, y_sample, new_state)
```

```python
import functools

import jax
import jax.numpy as jnp
import numpy as np
from jax import lax
from jax.experimental import pallas as pl
from jax.experimental.pallas import tpu as pltpu

F32 = jnp.float32
BF16 = jnp.bfloat16
EPS = 1e-6

HEAD_DIM = 128
N_MOD = 6
POOL_WINDOWS = (2, 4, 8, 16)
GRID_W = 64
CONV_HALO = 8
VMEM_LIMIT = 56 * 1024 * 1024


def _params(sem):
    return pltpu.CompilerParams(dimension_semantics=sem, vmem_limit_bytes=VMEM_LIMIT)


def _silu(x):
    return x * jax.nn.sigmoid(x)


def _norm_mod(x, nw, sc, sh):
    y = x * lax.rsqrt(jnp.mean(x * x, axis=-1, keepdims=True) + EPS) * nw
    return y * (1.0 + sc) + sh


def _ada_kernel(c_ref, w_ref, b_ref, o_ref):
    s = _silu(c_ref[...]).astype(BF16)
    o_ref[...] = jnp.dot(s, w_ref[...].astype(BF16), preferred_element_type=F32) + b_ref[...]


def _ada(cvecs, w_ada, b_ada, tn=1024):
    r, d = cvecs.shape
    n = w_ada.shape[1]
    return pl.pallas_call(
        _ada_kernel,
        grid=(n // tn,),
        in_specs=[pl.BlockSpec((r, d), lambda j: (0, 0)),
                  pl.BlockSpec((d, tn), lambda j: (0, j)),
                  pl.BlockSpec((1, tn), lambda j: (0, j))],
        out_specs=pl.BlockSpec((r, tn), lambda j: (0, j)),
        out_shape=jax.ShapeDtypeStruct((r, n), F32),
        compiler_params=_params(("arbitrary",)),
        name="ada",
    )(cvecs, w_ada, b_ada.reshape(1, n))


def _in_proj_kernel(x_ref, mod_ref, nw_ref, w_ref, o_ref):
    mod = mod_ref[0]
    h = _norm_mod(x_ref[...], nw_ref[...], mod[1:2], mod[0:1]).astype(BF16)
    o_ref[...] = jnp.dot(h, w_ref[...], preferred_element_type=F32)


def _in_proj(x2, mod, rows_per_mod, nw, w, tm=256):
    t, d = x2.shape
    n = w.shape[1]
    return pl.pallas_call(
        _in_proj_kernel,
        grid=(t // tm,),
        in_specs=[pl.BlockSpec((tm, d), lambda i: (i, 0)),
                  pl.BlockSpec((1, N_MOD, d), lambda i: ((i * tm) // rows_per_mod, 0, 0)),
                  pl.BlockSpec((1, d), lambda i: (0, 0)),
                  pl.BlockSpec((d, n), lambda i: (0, 0), pipeline_mode=pl.Buffered(1))],
        out_specs=pl.BlockSpec((tm, n), lambda i: (i, 0)),
        out_shape=jax.ShapeDtypeStruct((t, n), F32),
        compiler_params=_params(("arbitrary",)),
        name="in_proj",
    )(x2, mod, nw.reshape(1, d), w)


LOG2E = 1.4426950408889634
TOTAL_ROWS = 16


def _decay_matrices(c):
    t = np.arange(c)
    blocks = []
    m = 1
    while m < c:
        mid = (t // (2 * m)) * (2 * m) + m
        second = (t & m) != 0
        qm = np.zeros((c, 2 * c), np.float32)
        km = np.zeros((c, 2 * c), np.float32)
        for i in range(c):
            if second[i]:
                qm[i, mid[i]:i + 1] = 1
                km[i, c + mid[i]:c + i] = 1
            else:
                qm[i, c + i:c + mid[i]] = 1
                km[i, i + 1:mid[i]] = 1
        blocks += [qm, km]
        m *= 2
    qf = np.zeros((c, 2 * c), np.float32)
    qb = np.zeros((c, 2 * c), np.float32)
    kf = np.zeros((c, 2 * c), np.float32)
    kb = np.zeros((c, c), np.float32)
    for i in range(c):
        qf[i, 0:i + 1] = 1
        qb[i, c + i:2 * c] = 1
        kf[i, i + 1:c] = 1
        kb[i, 0:i] = 1
    tf = np.zeros((TOTAL_ROWS, 2 * c), np.float32)
    tf[:, 0:c] = 1
    tb = np.ones((TOTAL_ROWS, c), np.float32)
    sweep = np.concatenate(blocks + [qf, qb, kf, tf], axis=0)
    back = np.concatenate([kb, tb], axis=0)
    return (jnp.asarray(np.concatenate([sweep, sweep], axis=1), BF16),
            jnp.asarray(np.concatenate([back, back], axis=1), BF16))


def _gates(z, lbd):
    f = lbd + (1.0 - lbd) * jax.nn.sigmoid(z)
    return jnp.log(f) * LOG2E, 1.0 - f


def _split_bf16(x):
    hi = x.astype(BF16)
    return hi, (x - hi.astype(F32)).astype(BF16)


def _dot_nt(a, b):
    return lax.dot_general(a, b, (((1,), (1,)), ((), ())), preferred_element_type=F32)


def _dot_tn(a, b):
    return lax.dot_general(a, b, (((0,), (0,)), ((), ())), preferred_element_type=F32)


def _scan_kernel(*refs, seq, c, grp, has_s0, want_state):
    q_ref, zf_ref, zb_ref, v_ref, g_ref, lb_ref, nw_ref, ms_ref, mb_ref = refs[:9]
    pos = 9
    s0_ref = None
    if has_s0:
        s0_ref = refs[pos]
        pos += 1
    o_ref = refs[pos]
    pos += 1
    sout_ref = None
    if want_state:
        sout_ref = refs[pos]
        pos += 1
    lfb_scr, kb_scr, sb_scr = refs[pos:pos + 3]

    gc = grp * c
    n_iters = seq // gc
    n_levels = c.bit_length() - 1
    lb_f = lb_ref[0:1, :]
    lb_b = lb_ref[1:2, :]
    nw = nw_ref[...]
    row = lax.broadcasted_iota(jnp.int32, (c, HEAD_DIM), 0)
    pair = (lax.broadcasted_iota(jnp.int32, (c, c), 0)
            ^ lax.broadcasted_iota(jnp.int32, (c, c), 1))
    level = jnp.full((c, c), -1, jnp.int32)
    for k in range(n_levels):
        level = level + (pair >= (1 << k)).astype(jnp.int32)

    if has_s0:
        stf0 = s0_ref[0].T
        stb0 = s0_ref[1].T
    else:
        stf0 = jnp.zeros((HEAD_DIM, HEAD_DIM), F32)
        stb0 = stf0

    def chunk_rows(x, gi):
        return x[gi * c:(gi + 1) * c]

    def chunk_lanes(x, gi):
        return x[:, gi * HEAD_DIM:(gi + 1) * HEAD_DIM]

    def stack_chunks(parts):
        return jnp.concatenate(
            [jnp.concatenate([chunk_rows(p, gi) for p in parts], axis=0) for gi in range(grp)],
            axis=1)

    def sweep_b(i, stb):
        it = n_iters - 1 - i
        rows = pl.ds(pl.multiple_of(it * gc, gc), gc)
        l2f_b, k_b = _gates(zb_ref[rows, :], lb_b)
        lfb_scr[rows, :] = l2f_b
        kb_scr[rows, :] = k_b
        v_bf = v_ref[rows, :].astype(BF16)
        e = jnp.exp2(jnp.dot(mb_ref[...], stack_chunks(_split_bf16(l2f_b)),
                             preferred_element_type=F32))
        for gi in reversed(range(grp)):
            eg = chunk_lanes(e, gi)
            sb_scr[it * grp + gi] = stb
            k_hat = (chunk_rows(k_b, gi) * eg[0:c]).astype(BF16)
            stb = stb * eg[c:c + 1] + _dot_tn(chunk_rows(v_bf, gi), k_hat)
        return stb

    stb_final = lax.fori_loop(0, n_iters, sweep_b, stb0)

    def sweep_f(it, stf):
        rows = pl.ds(pl.multiple_of(it * gc, gc), gc)
        qs_all = _silu(q_ref[rows, :])
        v_all = v_ref[rows, :]
        vbf_all = v_all.astype(BF16)
        gate_all = _silu(g_ref[rows, :])
        l2f_f, kf_all = _gates(zf_ref[rows, :], lb_f)
        kb_all = kb_scr[rows, :]
        fh, fl = _split_bf16(l2f_f)
        bh, bl = _split_bf16(lfb_scr[rows, :])
        e_all = jnp.exp2(jnp.dot(ms_ref[...], stack_chunks((fh, bh, fl, bl)),
                                 preferred_element_type=F32))
        outs = []
        for gi in range(grp):
            e = chunk_lanes(e_all, gi)
            qs = chunk_rows(qs_all, gi)
            v = chunk_rows(v_all, gi)
            v_bf = chunk_rows(vbf_all, gi)
            k_f = chunk_rows(kf_all, gi)
            k_b = chunk_rows(kb_all, gi)

            a = jnp.zeros((c, c), F32)
            for lv in range(n_levels):
                second = (row & (1 << lv)) != 0
                x = (qs * e[2 * lv * c:(2 * lv + 1) * c]).astype(BF16)
                y = (jnp.where(second, k_b, k_f) * e[(2 * lv + 1) * c:(2 * lv + 2) * c]).astype(BF16)
                a = jnp.where(level == lv, _dot_nt(x, y), a)
            base = 2 * n_levels * c
            q_f = (qs * e[base:base + c]).astype(BF16)
            q_b = (qs * e[base + c:base + 2 * c]).astype(BF16)
            k_hat = (k_f * e[base + 2 * c:base + 3 * c]).astype(BF16)
            dec = e[base + 3 * c:base + 3 * c + 1]

            o = jnp.dot(a.astype(BF16), v_bf, preferred_element_type=F32)
            o = o + jnp.sum(qs * (k_f + k_b), axis=-1, keepdims=True) * v
            o = o + _dot_nt(q_f, stf.astype(BF16))
            o = o + _dot_nt(q_b, sb_scr[it * grp + gi].astype(BF16))
            o = o * lax.rsqrt(jnp.mean(o * o, axis=-1, keepdims=True) + EPS) * nw
            outs.append(o * chunk_rows(gate_all, gi))
            stf = stf * dec + _dot_tn(v_bf, k_hat)
        o_ref[rows, :] = jnp.concatenate(outs, axis=0).astype(o_ref.dtype)
        return stf

    stf_final = lax.fori_loop(0, n_iters, sweep_f, stf0)

    if want_state:
        sout_ref[0] = stf_final.T
        sout_ref[1] = stb_final.T


def _scan(proj, lb, hg_nw, s0, layer, batch, seq, want_state, c=64, grp=4):
    t = proj.shape[0]
    n_heads = lb.shape[1] // HEAD_DIM
    has_s0 = s0 is not None
    m_sweep, m_back = _decay_matrices(c)

    def sec(k):
        return pl.BlockSpec((seq, HEAD_DIM), lambda b, h, k=k: (b, k * n_heads + h))

    state_block = (None, None, 2, None, HEAD_DIM, HEAD_DIM)
    state_map = lambda b, h: (b, layer, 0, h, 0, 0)
    in_specs = [sec(0), sec(1), sec(2), sec(3), sec(4),
                pl.BlockSpec((2, HEAD_DIM), lambda b, h: (0, h)),
                pl.BlockSpec((1, HEAD_DIM), lambda b, h: (0, 0)),
                pl.BlockSpec(m_sweep.shape, lambda b, h: (0, 0)),
                pl.BlockSpec(m_back.shape, lambda b, h: (0, 0))]
    args = [proj, proj, proj, proj, proj, lb, hg_nw.reshape(1, HEAD_DIM), m_sweep, m_back]
    if has_s0:
        in_specs.append(pl.BlockSpec(state_block, state_map))
        args.append(s0)
    out_specs = [pl.BlockSpec((seq, HEAD_DIM), lambda b, h: (b, h))]
    out_shape = [jax.ShapeDtypeStruct((t, n_heads * HEAD_DIM), BF16)]
    if want_state:
        out_specs.append(pl.BlockSpec((None, 2, None, HEAD_DIM, HEAD_DIM),
                                      lambda b, h: (b, 0, h, 0, 0)))
        out_shape.append(jax.ShapeDtypeStruct((batch, 2, n_heads, HEAD_DIM, HEAD_DIM), F32))
    res = pl.pallas_call(
        functools.partial(_scan_kernel, seq=seq, c=c, grp=grp, has_s0=has_s0,
                          want_state=want_state),
        grid=(batch, n_heads),
        in_specs=in_specs,
        out_specs=out_specs,
        out_shape=out_shape,
        scratch_shapes=[pltpu.VMEM((seq, HEAD_DIM), F32),
                        pltpu.VMEM((seq, HEAD_DIM), F32),
                        pltpu.VMEM((seq // c, HEAD_DIM, HEAD_DIM), F32)],
        compiler_params=_params(("arbitrary", "arbitrary")),
        name="scan",
    )(*args)
    return (res[0], res[1]) if want_state else (res[0], None)


def _window_mean(a, w, pos, n, stride):
    rows = a.shape[0]
    half = w // 2
    fwd = a
    ln = 1
    while ln < half:
        fwd = fwd + jnp.where(pos < n - ln, pltpu.roll(fwd, rows - ln * stride, 0), 0.0)
        ln *= 2
    bwd = jnp.where(pos >= 1, pltpu.roll(a, stride, 0), 0.0)
    ln = 1
    while ln < half:
        bwd = bwd + jnp.where(pos >= ln, pltpu.roll(bwd, ln * stride, 0), 0.0)
        ln *= 2
    cnt = jnp.minimum(pos + (w - half), n) - jnp.maximum(pos - half, 0)
    return (fwd + bwd) / cnt.astype(F32)


def _pool_kernel(p_ref, w_ref, sc_ref, o_ref, *, seq, on_grid):
    grp = pl.program_id(1)
    for gi, w in enumerate(POOL_WINDOWS):
        @pl.when(grp == gi)
        def _(w=w):
            pg = p_ref[...]
            row = lax.broadcasted_iota(jnp.int32, pg.shape, 0)
            if on_grid:
                m = _window_mean(pg, w, row % GRID_W, GRID_W, 1)
                m = _window_mean(m, w, row // GRID_W, seq // GRID_W, GRID_W)
            else:
                m = _window_mean(pg, w, row, seq, 1)
            d = (m - pg).astype(BF16)
            y = jnp.dot(d, w_ref[...], preferred_element_type=F32) * sc_ref[...]
            o_ref[...] = y.astype(o_ref.dtype)


def _pool(proj, w_pool_l, pool_scale_l, col0, batch, seq, on_grid):
    t = proj.shape[0]
    n_grp, gw, _ = w_pool_l.shape
    blk0 = col0 // gw
    return pl.pallas_call(
        functools.partial(_pool_kernel, seq=seq, on_grid=on_grid),
        grid=(batch, n_grp),
        in_specs=[pl.BlockSpec((seq, gw), lambda b, g: (b, blk0 + g)),
                  pl.BlockSpec((None, gw, gw), lambda b, g: (g, 0, 0)),
                  pl.BlockSpec((1, gw), lambda b, g: (0, g))],
        out_specs=pl.BlockSpec((seq, gw), lambda b, g: (b, g)),
        out_shape=jax.ShapeDtypeStruct((t, n_grp * gw), BF16),
        compiler_params=_params(("arbitrary", "arbitrary")),
        name="pool",
    )(proj, w_pool_l, pool_scale_l.reshape(1, n_grp * gw))


def _mix_kernel(o_ref, pm_ref, x_ref, mod_ref, w_ref, out_ref):
    k = o_ref.shape[1]
    mix = (jnp.dot(o_ref[...], w_ref[0:k, :], preferred_element_type=F32)
           + jnp.dot(pm_ref[...], w_ref[k:, :], preferred_element_type=F32))
    out_ref[...] = x_ref[...] + mod_ref[0][2:3] * mix


def _mix(o, pm, x2, mod, rows_per_mod, w_out, tm=512):
    t, d = x2.shape
    return pl.pallas_call(
        _mix_kernel,
        grid=(t // tm,),
        in_specs=[pl.BlockSpec((tm, o.shape[1]), lambda i: (i, 0)),
                  pl.BlockSpec((tm, pm.shape[1]), lambda i: (i, 0)),
                  pl.BlockSpec((tm, d), lambda i: (i, 0)),
                  pl.BlockSpec((1, N_MOD, d), lambda i: ((i * tm) // rows_per_mod, 0, 0)),
                  pl.BlockSpec(w_out.shape, lambda i: (0, 0))],
        out_specs=pl.BlockSpec((tm, d), lambda i: (i, 0)),
        out_shape=jax.ShapeDtypeStruct((t, d), F32),
        compiler_params=_params(("arbitrary",)),
        name="mix",
    )(o, pm, x2, mod, w_out)


def _ffn_up_kernel(x_ref, xp_ref, xn_ref, mod_ref, nw_ref, wa_ref, wb_ref, cwa_ref, cwb_ref,
                   cba_ref, cbb_ref, o_ref, h_ref, *, tm, seq):
    i = pl.program_id(0)

    @pl.when(pl.program_id(1) == 0)
    def _():
        mod = mod_ref[0]
        nw = nw_ref[...]
        sc, sh = mod[4:5], mod[3:4]
        starts_seq = (i * tm) % seq == 0
        ends_seq = ((i + 1) * tm) % seq == 0
        above = jnp.where(starts_seq, 0.0, _norm_mod(xp_ref[...], nw, sc, sh))
        below = jnp.where(ends_seq, 0.0, _norm_mod(xn_ref[...], nw, sc, sh))
        h_ref[0:CONV_HALO, :] = above.astype(BF16)
        h_ref[CONV_HALO:CONV_HALO + tm, :] = _norm_mod(x_ref[...], nw, sc, sh).astype(BF16)
        h_ref[CONV_HALO + tm:, :] = below.astype(BF16)

    ext = tm + 2 * CONV_HALO
    h = h_ref[...]

    def conv(w_ref, cw_ref, cb_ref):
        u = jnp.dot(h, w_ref[...], preferred_element_type=F32)
        um = u[CONV_HALO:CONV_HALO + tm]
        up = pltpu.roll(u, 1, 0)[CONV_HALO:CONV_HALO + tm]
        un = pltpu.roll(u, ext - 1, 0)[CONV_HALO:CONV_HALO + tm]
        if tm > seq:
            in_seq = lax.broadcasted_iota(jnp.int32, um.shape, 0) % seq
            up = jnp.where(in_seq == 0, 0.0, up)
            un = jnp.where(in_seq == seq - 1, 0.0, un)
        cw = cw_ref[...]
        return up * cw[0:1] + um * cw[1:2] + un * cw[2:3] + cb_ref[...]

    a = conv(wa_ref, cwa_ref, cba_ref)
    b = conv(wb_ref, cwb_ref, cbb_ref)
    o_ref[...] = (_silu(a) * b).astype(o_ref.dtype)


def _ffn_down_kernel(a_ref, x_ref, mod_ref, w_ref, fnw_ref, o_ref, *, final_norm):
    y = x_ref[...] + mod_ref[0][5:6] * jnp.dot(a_ref[...], w_ref[...],
                                               preferred_element_type=F32)
    if final_norm:
        y = y * lax.rsqrt(jnp.mean(y * y, axis=-1, keepdims=True) + EPS) * fnw_ref[...]
    o_ref[...] = y


def _ffn(x2, mod, rows_per_mod, seq, nw, w_up, conv_w, conv_b, w_down, fnw, final_norm,
         tm=1024, tf=512, tm_down=256):
    t, d = x2.shape
    dff = w_down.shape[0]
    nf = dff // tf
    assert seq % tm == 0 or tm % seq == 0
    hb = tm // CONV_HALO
    last_hb = t // CONV_HALO - 1
    cb = conv_b.reshape(1, 2 * dff)
    mod_spec = lambda tile: pl.BlockSpec((1, N_MOD, d),
                                         lambda i, *_: ((i * tile) // rows_per_mod, 0, 0))
    act = pl.pallas_call(
        functools.partial(_ffn_up_kernel, tm=tm, seq=seq),
        grid=(t // tm, nf),
        in_specs=[pl.BlockSpec((tm, d), lambda i, f: (i, 0)),
                  pl.BlockSpec((CONV_HALO, d), lambda i, f: (jnp.maximum(i * hb - 1, 0), 0)),
                  pl.BlockSpec((CONV_HALO, d), lambda i, f: (jnp.minimum((i + 1) * hb, last_hb), 0)),
                  mod_spec(tm),
                  pl.BlockSpec((1, d), lambda i, f: (0, 0)),
                  pl.BlockSpec((d, tf), lambda i, f: (0, f)),
                  pl.BlockSpec((d, tf), lambda i, f: (0, f + nf)),
                  pl.BlockSpec((3, tf), lambda i, f: (0, f)),
                  pl.BlockSpec((3, tf), lambda i, f: (0, f + nf)),
                  pl.BlockSpec((1, tf), lambda i, f: (0, f)),
                  pl.BlockSpec((1, tf), lambda i, f: (0, f + nf))],
        out_specs=pl.BlockSpec((tm, tf), lambda i, f: (i, f)),
        out_shape=jax.ShapeDtypeStruct((t, dff), BF16),
        scratch_shapes=[pltpu.VMEM((tm + 2 * CONV_HALO, d), BF16)],
        compiler_params=_params(("arbitrary", "arbitrary")),
        name="ffn_up",
    )(x2, x2, x2, mod, nw.reshape(1, d), w_up, w_up, conv_w, conv_w, cb, cb)
    return pl.pallas_call(
        functools.partial(_ffn_down_kernel, final_norm=final_norm),
        grid=(t // tm_down,),
        in_specs=[pl.BlockSpec((tm_down, dff), lambda i: (i, 0)),
                  pl.BlockSpec((tm_down, d), lambda i: (i, 0)),
                  mod_spec(tm_down),
                  pl.BlockSpec((dff, d), lambda i: (0, 0), pipeline_mode=pl.Buffered(1)),
                  pl.BlockSpec((1, d), lambda i: (0, 0))],
        out_specs=pl.BlockSpec((tm_down, d), lambda i: (i, 0)),
        out_shape=jax.ShapeDtypeStruct((t, d), F32),
        compiler_params=_params(("arbitrary",)),
        name="ffn_down",
    )(act, x2, mod, w_down, fnw.reshape(1, d))


def kernel(x_prompt, x_sample, state_hgrn, c, c_ctx, w_ada, b_ada, norm1_w, w_in, lb_param,
           hg_norm_w, w_pool, pool_scale, w_out, norm2_w, w_up, conv_w, conv_b, w_down,
           final_norm_w):
    depth = w_in.shape[0]
    bp, lp, d = x_prompt.shape
    bs, ls, _ = x_sample.shape
    hg_width = lb_param.shape[2]

    lb_all = jnp.cumsum(jax.nn.softmax(lb_param.astype(F32), axis=1), axis=1)
    n_c = 1 + bs
    pad = (-n_c) % 8
    cvecs = jnp.concatenate([c_ctx[None, :], c, jnp.zeros((pad, d), F32)], axis=0)

    xp = x_prompt.reshape(bp * lp, d)
    xs = x_sample.reshape(bs * ls, d)
    ctx_states = []
    for l in range(depth):
        last = l == depth - 1
        mods = _ada(cvecs, w_ada[l], b_ada[l])
        mod_p = mods[0:1].reshape(1, N_MOD, d)
        mod_s = mods[1:n_c].reshape(bs, N_MOD, d)
        w_in_l = w_in[l].astype(BF16)
        w_pool_l = w_pool[l].astype(BF16)
        w_out_l = w_out[l].astype(BF16)
        w_up_l = w_up[l].astype(BF16)
        w_down_l = w_down[l].astype(BF16)
        lb = lb_all[:, l]

        def layer(x2, mod, rows_per_mod, batch, seq, s0, on_grid, want_state):
            proj = _in_proj(x2, mod, rows_per_mod, norm1_w[l], w_in_l)
            o, st = _scan(proj, lb, hg_norm_w[l], s0, l, batch, seq, want_state)
            pm = _pool(proj, w_pool_l, pool_scale[l], 5 * hg_width, batch, seq, on_grid)
            x2 = _mix(o, pm, x2, mod, rows_per_mod, w_out_l)
            x2 = _ffn(x2, mod, rows_per_mod, seq, norm2_w[l], w_up_l, conv_w[l], conv_b[l],
                      w_down_l, final_norm_w, last)
            return x2, st

        xp, st = layer(xp, mod_p, bp * lp, bp, lp, None, False, True)
        ctx_states.append(st)
        xs, _ = layer(xs, mod_s, ls, bs, ls, state_hgrn, True, False)

    y_prompt = xp.reshape(bp, lp, d)
    y_sample = xs.reshape(bs, ls, d)
    new_state = jnp.stack(ctx_states, axis=1)
    return (y_prompt, y_sample, new_state)
```

```python
import functools

import jax
import jax.numpy as jnp
import numpy as np
from jax import lax
from jax.experimental import pallas as pl
from jax.experimental.pallas import tpu as pltpu

F32 = jnp.float32
BF16 = jnp.bfloat16
EPS = 1e-6

HEAD_DIM = 128
N_MOD = 6
POOL_WINDOWS = (2, 4, 8, 16)
GRID_W = 64
CONV_HALO = 8
VMEM_LIMIT = 56 * 1024 * 1024


def _params(sem):
    return pltpu.CompilerParams(dimension_semantics=sem, vmem_limit_bytes=VMEM_LIMIT)


def _silu(x):
    return x * jax.nn.sigmoid(x)


def _norm_mod(x, nw, sc, sh):
    y = x * lax.rsqrt(jnp.mean(x * x, axis=-1, keepdims=True) + EPS) * nw
    return y * (1.0 + sc) + sh


def _ada_kernel(c_ref, w_ref, b_ref, o_ref):
    s = _silu(c_ref[...]).astype(BF16)
    o_ref[...] = jnp.dot(s, w_ref[...].astype(BF16), preferred_element_type=F32) + b_ref[...]


def _ada(cvecs, w_ada, b_ada, tn=1024):
    r, d = cvecs.shape
    n = w_ada.shape[1]
    return pl.pallas_call(
        _ada_kernel,
        grid=(n // tn,),
        in_specs=[pl.BlockSpec((r, d), lambda j: (0, 0)),
                  pl.BlockSpec((d, tn), lambda j: (0, j)),
                  pl.BlockSpec((1, tn), lambda j: (0, j))],
        out_specs=pl.BlockSpec((r, tn), lambda j: (0, j)),
        out_shape=jax.ShapeDtypeStruct((r, n), F32),
        compiler_params=_params(("arbitrary",)),
        name="ada",
    )(cvecs, w_ada, b_ada.reshape(1, n))


def _in_proj_kernel(x_ref, mod_ref, nw_ref, w_ref, o_ref):
    mod = mod_ref[0]
    h = _norm_mod(x_ref[...], nw_ref[...], mod[1:2], mod[0:1]).astype(BF16)
    o_ref[...] = jnp.dot(h, w_ref[...], preferred_element_type=F32)


def _in_proj(x2, mod, rows_per_mod, nw, w, tm=256):
    t, d = x2.shape
    n = w.shape[1]
    return pl.pallas_call(
        _in_proj_kernel,
        grid=(t // tm,),
        in_specs=[pl.BlockSpec((tm, d), lambda i: (i, 0)),
                  pl.BlockSpec((1, N_MOD, d), lambda i: ((i * tm) // rows_per_mod, 0, 0)),
                  pl.BlockSpec((1, d), lambda i: (0, 0)),
                  pl.BlockSpec((d, n), lambda i: (0, 0), pipeline_mode=pl.Buffered(1))],
        out_specs=pl.BlockSpec((tm, n), lambda i: (i, 0)),
        out_shape=jax.ShapeDtypeStruct((t, n), F32),
        compiler_params=_params(("arbitrary",)),
        name="in_proj",
    )(x2, mod, nw.reshape(1, d), w)


LOG2E = 1.4426950408889634
TOTAL_ROWS = 16


def _decay_matrices(c):
    t = np.arange(c)
    blocks = []
    m = 1
    while m < c:
        mid = (t // (2 * m)) * (2 * m) + m
        second = (t & m) != 0
        qm = np.zeros((c, 2 * c), np.float32)
        km = np.zeros((c, 2 * c), np.float32)
        for i in range(c):
            if second[i]:
                qm[i, mid[i]:i + 1] = 1
                km[i, c + mid[i]:c + i] = 1
            else:
                qm[i, c + i:c + mid[i]] = 1
                km[i, i + 1:mid[i]] = 1
        blocks += [qm, km]
        m *= 2
    qf = np.zeros((c, 2 * c), np.float32)
    qb = np.zeros((c, 2 * c), np.float32)
    kf = np.zeros((c, 2 * c), np.float32)
    kb = np.zeros((c, c), np.float32)
    for i in range(c):
        qf[i, 0:i + 1] = 1
        qb[i, c + i:2 * c] = 1
        kf[i, i + 1:c] = 1
        kb[i, 0:i] = 1
    tf = np.zeros((TOTAL_ROWS, 2 * c), np.float32)
    tf[:, 0:c] = 1
    tb = np.ones((TOTAL_ROWS, c), np.float32)
    sweep = np.concatenate(blocks + [qf, qb, kf, tf], axis=0)
    back = np.concatenate([kb, tb], axis=0)
    return (jnp.asarray(np.concatenate([sweep, sweep], axis=1), BF16),
            jnp.asarray(np.concatenate([back, back], axis=1), BF16))


def _gates(z, lbd):
    f = lbd + (1.0 - lbd) * jax.nn.sigmoid(z)
    return jnp.log(f) * LOG2E, 1.0 - f


def _split_bf16(x):
    hi = x.astype(BF16)
    return hi, (x - hi.astype(F32)).astype(BF16)


def _dot_nt(a, b):
    return lax.dot_general(a, b, (((1,), (1,)), ((), ())), preferred_element_type=F32)


def _dot_tn(a, b):
    return lax.dot_general(a, b, (((0,), (0,)), ((), ())), preferred_element_type=F32)


def _scan_kernel(*refs, seq, c, grp, grp_b, has_s0, want_state):
    q_ref, zf_ref, zb_ref, v_ref, g_ref, lb_ref, nw_ref, ms_ref, mb_ref = refs[:9]
    pos = 9
    s0_ref = None
    if has_s0:
        s0_ref = refs[pos]
        pos += 1
    o_ref = refs[pos]
    pos += 1
    sout_ref = None
    if want_state:
        sout_ref = refs[pos]
        pos += 1
    lfb_scr, kb_scr, sb_scr, e_scr, qs_scr, kf_scr, gate_scr = refs[pos:pos + 7]

    gc = grp * c
    n_iters = seq // gc
    n_levels = c.bit_length() - 1
    base = 2 * n_levels * c
    lb_f = lb_ref[0:1, :]
    lb_b = lb_ref[1:2, :]
    nw = nw_ref[...]
    row = lax.broadcasted_iota(jnp.int32, (c, HEAD_DIM), 0)
    pair = (lax.broadcasted_iota(jnp.int32, (c, c), 0)
            ^ lax.broadcasted_iota(jnp.int32, (c, c), 1))
    level = jnp.full((c, c), -1, jnp.int32)
    for k in range(n_levels):
        level = level + (pair >= (1 << k)).astype(jnp.int32)

    if has_s0:
        stf0 = s0_ref[0].T
        stb0 = s0_ref[1].T
    else:
        stf0 = jnp.zeros((HEAD_DIM, HEAD_DIM), F32)
        stb0 = stf0

    def chunk_rows(x, gi):
        return x[gi * c:(gi + 1) * c]

    def chunk_lanes(x, gi):
        return x[:, gi * HEAD_DIM:(gi + 1) * HEAD_DIM]

    def stack_chunks(parts, n):
        return jnp.concatenate(
            [jnp.concatenate([chunk_rows(p, gi) for p in parts], axis=0) for gi in range(n)],
            axis=1)

    gcb = grp_b * c

    def sweep_b(i, stb):
        it = seq // gcb - 1 - i
        rows = pl.ds(pl.multiple_of(it * gcb, gcb), gcb)
        l2f_b, k_b = _gates(zb_ref[rows, :], lb_b)
        lfb_scr[rows, :] = l2f_b
        kb_scr[rows, :] = k_b
        v_bf = v_ref[rows, :].astype(BF16)
        e = jnp.exp2(jnp.dot(mb_ref[...], stack_chunks(_split_bf16(l2f_b), grp_b),
                             preferred_element_type=F32))
        kv = []
        for gi in range(grp_b):
            k_hat = (chunk_rows(k_b, gi) * chunk_lanes(e, gi)[0:c]).astype(BF16)
            kv.append(_dot_tn(chunk_rows(v_bf, gi), k_hat))
        for gi in reversed(range(grp_b)):
            sb_scr[it * grp_b + gi] = stb
            stb = stb * chunk_lanes(e, gi)[c:c + 1] + kv[gi]
        return stb

    stb_final = lax.fori_loop(0, seq // gcb, sweep_b, stb0)

    def block_rows(blk):
        return pl.ds(pl.multiple_of(blk * gc, gc), gc)

    def stage_a_gates(blk):
        rows = block_rows(blk)
        qs = _silu(q_ref[rows, :])
        gate = _silu(g_ref[rows, :])
        l2f_f, k_f = _gates(zf_ref[rows, :], lb_f)
        fh, fl = _split_bf16(l2f_f)
        bh, bl = _split_bf16(lfb_scr[rows, :])
        return qs, gate, k_f, stack_chunks((fh, bh, fl, bl), grp)

    def stage_a_finish(slot, vals):
        qs, gate, k_f, rhs = vals
        e_scr[slot] = jnp.exp2(jnp.dot(ms_ref[...], rhs, preferred_element_type=F32))
        qs_scr[slot] = qs
        gate_scr[slot] = gate
        kf_scr[slot] = k_f

    def stage_b_levels(blk, slot):
        rows = block_rows(blk)
        vbf_all = v_ref[rows, :].astype(BF16)
        kb_all = kb_scr[rows, :]
        scores, kv, dec = [], [], []
        for gi in range(grp):
            lanes = slice(gi * HEAD_DIM, (gi + 1) * HEAD_DIM)
            crow = slice(gi * c, (gi + 1) * c)
            qs = qs_scr[slot, crow, :]
            k_f = kf_scr[slot, crow, :]
            k_b = chunk_rows(kb_all, gi)
            a = jnp.zeros((c, c), F32)
            for lv in range(n_levels):
                second = (row & (1 << lv)) != 0
                x = (qs * e_scr[slot, 2 * lv * c:(2 * lv + 1) * c, lanes]).astype(BF16)
                y = (jnp.where(second, k_b, k_f)
                     * e_scr[slot, (2 * lv + 1) * c:(2 * lv + 2) * c, lanes]).astype(BF16)
                a = jnp.where(level == lv, _dot_nt(x, y), a)
            scores.append(a.astype(BF16))
            k_hat = (k_f * e_scr[slot, base + 2 * c:base + 3 * c, lanes]).astype(BF16)
            kv.append(_dot_tn(chunk_rows(vbf_all, gi), k_hat))
            dec.append(e_scr[slot, base + 3 * c:base + 3 * c + 1, lanes])
        return scores, kv, dec

    def stage_b_finish(blk, slot, stf, lv_out):
        scores, kv, dec = lv_out
        rows = block_rows(blk)
        v_all = v_ref[rows, :]
        vbf_all = v_all.astype(BF16)
        kb_all = kb_scr[rows, :]
        states = []
        for gi in range(grp):
            states.append(stf)
            stf = stf * dec[gi] + kv[gi]
        outs = []
        for gi in range(grp):
            lanes = slice(gi * HEAD_DIM, (gi + 1) * HEAD_DIM)
            crow = slice(gi * c, (gi + 1) * c)
            qs = qs_scr[slot, crow, :]
            q_fb = jnp.concatenate([qs * e_scr[slot, base:base + c, lanes],
                                    qs * e_scr[slot, base + c:base + 2 * c, lanes]],
                                   axis=1).astype(BF16)
            st_fb = jnp.concatenate([states[gi], sb_scr[blk * grp + gi]], axis=1).astype(BF16)
            o = jnp.dot(scores[gi], chunk_rows(vbf_all, gi), preferred_element_type=F32)
            o = o + jnp.sum(qs * (kf_scr[slot, crow, :] + chunk_rows(kb_all, gi)),
                            axis=-1, keepdims=True) * chunk_rows(v_all, gi)
            o = o + _dot_nt(q_fb, st_fb)
            o = o * lax.rsqrt(jnp.mean(o * o, axis=-1, keepdims=True) + EPS) * nw
            outs.append(o * gate_scr[slot, crow, :])
        o_ref[rows, :] = jnp.concatenate(outs, axis=0).astype(o_ref.dtype)
        return stf

    def step(cur, cur_slot, nxt, stf):
        ahead = stage_a_gates(nxt)
        lv_out = stage_b_levels(cur, cur_slot)
        stage_a_finish(1 - cur_slot, ahead)
        return stage_b_finish(cur, cur_slot, stf, lv_out)

    stage_a_finish(0, stage_a_gates(0))
    if n_iters == 1:
        stf_final = stage_b_finish(0, 0, stf0, stage_b_levels(0, 0))
    else:
        assert n_iters % 2 == 0

        def sweep_f(j, stf):
            stf = step(2 * j, 0, 2 * j + 1, stf)
            return step(2 * j + 1, 1, jnp.minimum(2 * j + 2, n_iters - 1), stf)

        stf_final = lax.fori_loop(0, n_iters // 2, sweep_f, stf0)

    if want_state:
        sout_ref[0] = stf_final.T
        sout_ref[1] = stb_final.T


def _scan(proj, lb, hg_nw, s0, layer, batch, seq, want_state, c=64, grp=4, grp_b=8):
    t = proj.shape[0]
    n_heads = lb.shape[1] // HEAD_DIM
    has_s0 = s0 is not None
    grp_b = min(grp_b, seq // c)
    m_sweep, m_back = _decay_matrices(c)

    def sec(k):
        return pl.BlockSpec((seq, HEAD_DIM), lambda b, h, k=k: (b, k * n_heads + h))

    state_block = (None, None, 2, None, HEAD_DIM, HEAD_DIM)
    state_map = lambda b, h: (b, layer, 0, h, 0, 0)
    in_specs = [sec(0), sec(1), sec(2), sec(3), sec(4),
                pl.BlockSpec((2, HEAD_DIM), lambda b, h: (0, h)),
                pl.BlockSpec((1, HEAD_DIM), lambda b, h: (0, 0)),
                pl.BlockSpec(m_sweep.shape, lambda b, h: (0, 0)),
                pl.BlockSpec(m_back.shape, lambda b, h: (0, 0))]
    args = [proj, proj, proj, proj, proj, lb, hg_nw.reshape(1, HEAD_DIM), m_sweep, m_back]
    if has_s0:
        in_specs.append(pl.BlockSpec(state_block, state_map))
        args.append(s0)
    out_specs = [pl.BlockSpec((seq, HEAD_DIM), lambda b, h: (b, h))]
    out_shape = [jax.ShapeDtypeStruct((t, n_heads * HEAD_DIM), BF16)]
    if want_state:
        out_specs.append(pl.BlockSpec((None, 2, None, HEAD_DIM, HEAD_DIM),
                                      lambda b, h: (b, 0, h, 0, 0)))
        out_shape.append(jax.ShapeDtypeStruct((batch, 2, n_heads, HEAD_DIM, HEAD_DIM), F32))
    gc = grp * c
    res = pl.pallas_call(
        functools.partial(_scan_kernel, seq=seq, c=c, grp=grp, grp_b=grp_b, has_s0=has_s0,
                          want_state=want_state),
        grid=(batch, n_heads),
        in_specs=in_specs,
        out_specs=out_specs,
        out_shape=out_shape,
        scratch_shapes=[pltpu.VMEM((seq, HEAD_DIM), F32),
                        pltpu.VMEM((seq, HEAD_DIM), F32),
                        pltpu.VMEM((seq // c, HEAD_DIM, HEAD_DIM), F32),
                        pltpu.VMEM((2, m_sweep.shape[0], grp * HEAD_DIM), F32),
                        pltpu.VMEM((2, gc, HEAD_DIM), F32),
                        pltpu.VMEM((2, gc, HEAD_DIM), F32),
                        pltpu.VMEM((2, gc, HEAD_DIM), F32)],
        compiler_params=_params(("arbitrary", "arbitrary")),
        name="scan",
    )(*args)
    return (res[0], res[1]) if want_state else (res[0], None)


def _window_mean(a, w, pos, n, stride):
    rows = a.shape[0]
    half = w // 2
    fwd = a
    ln = 1
    while ln < half:
        fwd = fwd + jnp.where(pos < n - ln, pltpu.roll(fwd, rows - ln * stride, 0), 0.0)
        ln *= 2
    bwd = jnp.where(pos >= 1, pltpu.roll(a, stride, 0), 0.0)
    ln = 1
    while ln < half:
        bwd = bwd + jnp.where(pos >= ln, pltpu.roll(bwd, ln * stride, 0), 0.0)
        ln *= 2
    cnt = jnp.minimum(pos + (w - half), n) - jnp.maximum(pos - half, 0)
    return (fwd + bwd) / cnt.astype(F32)


def _pool_kernel(p_ref, w_ref, sc_ref, o_ref, *, seq, on_grid):
    grp = pl.program_id(1)
    for gi, w in enumerate(POOL_WINDOWS):
        @pl.when(grp == gi)
        def _(w=w):
            pg = p_ref[...]
            row = lax.broadcasted_iota(jnp.int32, pg.shape, 0)
            if on_grid:
                m = _window_mean(pg, w, row % GRID_W, GRID_W, 1)
                m = _window_mean(m, w, row // GRID_W, seq // GRID_W, GRID_W)
            else:
                m = _window_mean(pg, w, row, seq, 1)
            d = (m - pg).astype(BF16)
            y = jnp.dot(d, w_ref[...], preferred_element_type=F32) * sc_ref[...]
            o_ref[...] = y.astype(o_ref.dtype)


def _pool(proj, w_pool_l, pool_scale_l, col0, batch, seq, on_grid):
    t = proj.shape[0]
    n_grp, gw, _ = w_pool_l.shape
    blk0 = col0 // gw
    return pl.pallas_call(
        functools.partial(_pool_kernel, seq=seq, on_grid=on_grid),
        grid=(batch, n_grp),
        in_specs=[pl.BlockSpec((seq, gw), lambda b, g: (b, blk0 + g)),
                  pl.BlockSpec((None, gw, gw), lambda b, g: (g, 0, 0)),
                  pl.BlockSpec((1, gw), lambda b, g: (0, g))],
        out_specs=pl.BlockSpec((seq, gw), lambda b, g: (b, g)),
        out_shape=jax.ShapeDtypeStruct((t, n_grp * gw), BF16),
        compiler_params=_params(("arbitrary", "arbitrary")),
        name="pool",
    )(proj, w_pool_l, pool_scale_l.reshape(1, n_grp * gw))


def _mix_kernel(o_ref, pm_ref, x_ref, mod_ref, w_ref, out_ref):
    k = o_ref.shape[1]
    mix = (jnp.dot(o_ref[...], w_ref[0:k, :], preferred_element_type=F32)
           + jnp.dot(pm_ref[...], w_ref[k:, :], preferred_element_type=F32))
    out_ref[...] = x_ref[...] + mod_ref[0][2:3] * mix


def _mix(o, pm, x2, mod, rows_per_mod, w_out, tm=512):
    t, d = x2.shape
    return pl.pallas_call(
        _mix_kernel,
        grid=(t // tm,),
        in_specs=[pl.BlockSpec((tm, o.shape[1]), lambda i: (i, 0)),
                  pl.BlockSpec((tm, pm.shape[1]), lambda i: (i, 0)),
                  pl.BlockSpec((tm, d), lambda i: (i, 0)),
                  pl.BlockSpec((1, N_MOD, d), lambda i: ((i * tm) // rows_per_mod, 0, 0)),
                  pl.BlockSpec(w_out.shape, lambda i: (0, 0))],
        out_specs=pl.BlockSpec((tm, d), lambda i: (i, 0)),
        out_shape=jax.ShapeDtypeStruct((t, d), F32),
        compiler_params=_params(("arbitrary",)),
        name="mix",
    )(o, pm, x2, mod, w_out)


def _ffn_up_kernel(x_ref, xp_ref, xn_ref, mod_ref, nw_ref, wa_ref, wb_ref, cwa_ref, cwb_ref,
                   cba_ref, cbb_ref, o_ref, h_ref, *, tm, seq):
    i = pl.program_id(0)

    @pl.when(pl.program_id(1) == 0)
    def _():
        mod = mod_ref[0]
        nw = nw_ref[...]
        sc, sh = mod[4:5], mod[3:4]
        starts_seq = (i * tm) % seq == 0
        ends_seq = ((i + 1) * tm) % seq == 0
        above = jnp.where(starts_seq, 0.0, _norm_mod(xp_ref[...], nw, sc, sh))
        below = jnp.where(ends_seq, 0.0, _norm_mod(xn_ref[...], nw, sc, sh))
        h_ref[0:CONV_HALO, :] = above.astype(BF16)
        h_ref[CONV_HALO:CONV_HALO + tm, :] = _norm_mod(x_ref[...], nw, sc, sh).astype(BF16)
        h_ref[CONV_HALO + tm:, :] = below.astype(BF16)

    ext = tm + 2 * CONV_HALO
    h = h_ref[...]

    def conv(w_ref, cw_ref, cb_ref):
        u = jnp.dot(h, w_ref[...], preferred_element_type=F32)
        um = u[CONV_HALO:CONV_HALO + tm]
        up = pltpu.roll(u, 1, 0)[CONV_HALO:CONV_HALO + tm]
        un = pltpu.roll(u, ext - 1, 0)[CONV_HALO:CONV_HALO + tm]
        if tm > seq:
            in_seq = lax.broadcasted_iota(jnp.int32, um.shape, 0) % seq
            up = jnp.where(in_seq == 0, 0.0, up)
            un = jnp.where(in_seq == seq - 1, 0.0, un)
        cw = cw_ref[...]
        return up * cw[0:1] + um * cw[1:2] + un * cw[2:3] + cb_ref[...]

    a = conv(wa_ref, cwa_ref, cba_ref)
    b = conv(wb_ref, cwb_ref, cbb_ref)
    o_ref[...] = (_silu(a) * b).astype(o_ref.dtype)


def _ffn_down_kernel(a_ref, x_ref, mod_ref, w_ref, fnw_ref, o_ref, *, final_norm):
    y = x_ref[...] + mod_ref[0][5:6] * jnp.dot(a_ref[...], w_ref[...],
                                               preferred_element_type=F32)
    if final_norm:
        y = y * lax.rsqrt(jnp.mean(y * y, axis=-1, keepdims=True) + EPS) * fnw_ref[...]
    o_ref[...] = y


def _ffn(x2, mod, rows_per_mod, seq, nw, w_up, conv_w, conv_b, w_down, fnw, final_norm,
         tm=1024, tf=512, tm_down=256):
    t, d = x2.shape
    dff = w_down.shape[0]
    nf = dff // tf
    assert seq % tm == 0 or tm % seq == 0
    hb = tm // CONV_HALO
    last_hb = t // CONV_HALO - 1
    cb = conv_b.reshape(1, 2 * dff)
    mod_spec = lambda tile: pl.BlockSpec((1, N_MOD, d),
                                         lambda i, *_: ((i * tile) // rows_per_mod, 0, 0))
    act = pl.pallas_call(
        functools.partial(_ffn_up_kernel, tm=tm, seq=seq),
        grid=(t // tm, nf),
        in_specs=[pl.BlockSpec((tm, d), lambda i, f: (i, 0)),
                  pl.BlockSpec((CONV_HALO, d), lambda i, f: (jnp.maximum(i * hb - 1, 0), 0)),
                  pl.BlockSpec((CONV_HALO, d), lambda i, f: (jnp.minimum((i + 1) * hb, last_hb), 0)),
                  mod_spec(tm),
                  pl.BlockSpec((1, d), lambda i, f: (0, 0)),
                  pl.BlockSpec((d, tf), lambda i, f: (0, f)),
                  pl.BlockSpec((d, tf), lambda i, f: (0, f + nf)),
                  pl.BlockSpec((3, tf), lambda i, f: (0, f)),
                  pl.BlockSpec((3, tf), lambda i, f: (0, f + nf)),
                  pl.BlockSpec((1, tf), lambda i, f: (0, f)),
                  pl.BlockSpec((1, tf), lambda i, f: (0, f + nf))],
        out_specs=pl.BlockSpec((tm, tf), lambda i, f: (i, f)),
        out_shape=jax.ShapeDtypeStruct((t, dff), BF16),
        scratch_shapes=[pltpu.VMEM((tm + 2 * CONV_HALO, d), BF16)],
        compiler_params=_params(("arbitrary", "arbitrary")),
        name="ffn_up",
    )(x2, x2, x2, mod, nw.reshape(1, d), w_up, w_up, conv_w, conv_w, cb, cb)
    return pl.pallas_call(
        functools.partial(_ffn_down_kernel, final_norm=final_norm),
        grid=(t // tm_down,),
        in_specs=[pl.BlockSpec((tm_down, dff), lambda i: (i, 0)),
                  pl.BlockSpec((tm_down, d), lambda i: (i, 0)),
                  mod_spec(tm_down),
                  pl.BlockSpec((dff, d), lambda i: (0, 0), pipeline_mode=pl.Buffered(1)),
                  pl.BlockSpec((1, d), lambda i: (0, 0))],
        out_specs=pl.BlockSpec((tm_down, d), lambda i: (i, 0)),
        out_shape=jax.ShapeDtypeStruct((t, d), F32),
        compiler_params=_params(("arbitrary",)),
        name="ffn_down",
    )(act, x2, mod, w_down, fnw.reshape(1, d))


def kernel(x_prompt, x_sample, state_hgrn, c, c_ctx, w_ada, b_ada, norm1_w, w_in, lb_param,
           hg_norm_w, w_pool, pool_scale, w_out, norm2_w, w_up, conv_w, conv_b, w_down,
           final_norm_w):
    depth = w_in.shape[0]
    bp, lp, d = x_prompt.shape
    bs, ls, _ = x_sample.shape
    hg_width = lb_param.shape[2]

    lb_all = jnp.cumsum(jax.nn.softmax(lb_param.astype(F32), axis=1), axis=1)
    n_c = 1 + bs
    pad = (-n_c) % 8
    cvecs = jnp.concatenate([c_ctx[None, :], c, jnp.zeros((pad, d), F32)], axis=0)

    xp = x_prompt.reshape(bp * lp, d)
    xs = x_sample.reshape(bs * ls, d)
    ctx_states = []
    for l in range(depth):
        last = l == depth - 1
        mods = _ada(cvecs, w_ada[l], b_ada[l])
        mod_p = mods[0:1].reshape(1, N_MOD, d)
        mod_s = mods[1:n_c].reshape(bs, N_MOD, d)
        w_in_l = w_in[l].astype(BF16)
        w_pool_l = w_pool[l].astype(BF16)
        w_out_l = w_out[l].astype(BF16)
        w_up_l = w_up[l].astype(BF16)
        w_down_l = w_down[l].astype(BF16)
        lb = lb_all[:, l]

        def layer(x2, mod, rows_per_mod, batch, seq, s0, on_grid, want_state):
            proj = _in_proj(x2, mod, rows_per_mod, norm1_w[l], w_in_l)
            o, st = _scan(proj, lb, hg_norm_w[l], s0, l, batch, seq, want_state)
            pm = _pool(proj, w_pool_l, pool_scale[l], 5 * hg_width, batch, seq, on_grid)
            x2 = _mix(o, pm, x2, mod, rows_per_mod, w_out_l)
            x2 = _ffn(x2, mod, rows_per_mod, seq, norm2_w[l], w_up_l, conv_w[l], conv_b[l],
                      w_down_l, final_norm_w, last)
            return x2, st

        xp, st = layer(xp, mod_p, bp * lp, bp, lp, None, False, True)
        ctx_states.append(st)
        xs, _ = layer(xs, mod_s, ls, bs, ls, state_hgrn, True, False)

    y_prompt = xp.reshape(bp, lp, d)
    y_sample = xs.reshape(bs, ls, d)
    new_state = jnp.stack(ctx_states, axis=1)
    return (y_prompt, y_sample, new_state)
```

```python
import functools

import jax
import jax.numpy as jnp
import numpy as np
from jax import lax
from jax.experimental import pallas as pl
from jax.experimental.pallas import tpu as pltpu

F32 = jnp.float32
BF16 = jnp.bfloat16
EPS = 1e-6

HEAD_DIM = 128
N_MOD = 6
POOL_WINDOWS = (2, 4, 8, 16)
GRID_W = 64
CONV_HALO = 8
VMEM_LIMIT = 56 * 1024 * 1024


def _params(sem):
    return pltpu.CompilerParams(dimension_semantics=sem, vmem_limit_bytes=VMEM_LIMIT)


def _silu(x):
    hx = 0.5 * x
    return hx + hx * jnp.tanh(hx)


def _norm_mod(x, nw, sc, sh):
    scale = nw * (1.0 + sc)
    return x * lax.rsqrt(jnp.mean(x * x, axis=-1, keepdims=True) + EPS) * scale + sh


def _ada_kernel(c_ref, w_ref, b_ref, o_ref):
    s = _silu(c_ref[...]).astype(BF16)
    o_ref[...] = jnp.dot(s, w_ref[...].astype(BF16), preferred_element_type=F32) + b_ref[...]


def _ada(cvecs, w_ada, b_ada, tn=1024):
    r, d = cvecs.shape
    n = w_ada.shape[1]
    return pl.pallas_call(
        _ada_kernel,
        grid=(n // tn,),
        in_specs=[pl.BlockSpec((r, d), lambda j: (0, 0)),
                  pl.BlockSpec((d, tn), lambda j: (0, j)),
                  pl.BlockSpec((1, tn), lambda j: (0, j))],
        out_specs=pl.BlockSpec((r, tn), lambda j: (0, j)),
        out_shape=jax.ShapeDtypeStruct((r, n), F32),
        compiler_params=_params(("arbitrary",)),
        name="ada",
    )(cvecs, w_ada, b_ada.reshape(1, n))


def _in_proj_kernel(x_ref, mod_ref, nw_ref, w_ref, o_ref):
    mod = mod_ref[0]
    h = _norm_mod(x_ref[...], nw_ref[...], mod[1:2], mod[0:1]).astype(BF16)
    o_ref[...] = jnp.dot(h, w_ref[...], preferred_element_type=F32)


def _in_proj(x2, mod, rows_per_mod, nw, w, tm=256):
    t, d = x2.shape
    n = w.shape[1]
    return pl.pallas_call(
        _in_proj_kernel,
        grid=(t // tm,),
        in_specs=[pl.BlockSpec((tm, d), lambda i: (i, 0)),
                  pl.BlockSpec((1, N_MOD, d), lambda i: ((i * tm) // rows_per_mod, 0, 0)),
                  pl.BlockSpec((1, d), lambda i: (0, 0)),
                  pl.BlockSpec((d, n), lambda i: (0, 0), pipeline_mode=pl.Buffered(1))],
        out_specs=pl.BlockSpec((tm, n), lambda i: (i, 0)),
        out_shape=jax.ShapeDtypeStruct((t, n), F32),
        compiler_params=_params(("arbitrary",)),
        name="in_proj",
    )(x2, mod, nw.reshape(1, d), w)


LOG2E = 1.4426950408889634
TOTAL_ROWS = 16


def _decay_matrices(c):
    t = np.arange(c)
    blocks = []
    m = 2
    while m < c:
        mid = (t // (2 * m)) * (2 * m) + m
        second = (t & m) != 0
        qm = np.zeros((c, 2 * c), np.float32)
        km = np.zeros((c, 2 * c), np.float32)
        for i in range(c):
            if second[i]:
                qm[i, mid[i]:i + 1] = 1
                km[i, c + mid[i]:c + i] = 1
            else:
                qm[i, c + i:c + mid[i]] = 1
                km[i, i + 1:mid[i]] = 1
        blocks += [qm, km]
        m *= 2
    qf = np.zeros((c, 2 * c), np.float32)
    qb = np.zeros((c, 2 * c), np.float32)
    kf = np.zeros((c, 2 * c), np.float32)
    kb = np.zeros((c, c), np.float32)
    for i in range(c):
        qf[i, 0:i + 1] = 1
        qb[i, c + i:2 * c] = 1
        kf[i, i + 1:c] = 1
        kb[i, 0:i] = 1
    tf = np.zeros((TOTAL_ROWS, 2 * c), np.float32)
    tf[:, 0:c] = 1
    tb = np.ones((TOTAL_ROWS, c), np.float32)
    sweep = np.concatenate(blocks + [qf, qb, kf, tf], axis=0)
    back = np.concatenate([kb, tb], axis=0)
    return (jnp.asarray(np.concatenate([sweep, sweep], axis=1), BF16),
            jnp.asarray(np.concatenate([back, back], axis=1), BF16))


def _gates(z, lbd):
    f = lbd + (1.0 - lbd) * jax.nn.sigmoid(z)
    return jnp.log(f) * LOG2E, 1.0 - f


def _split_bf16(x):
    hi = x.astype(BF16)
    return hi, (x - hi.astype(F32)).astype(BF16)


def _dot_nt(a, b):
    return lax.dot_general(a, b, (((1,), (1,)), ((), ())), preferred_element_type=F32)


def _dot_tn(a, b):
    return lax.dot_general(a, b, (((0,), (0,)), ((), ())), preferred_element_type=F32)


def _scan_kernel(*refs, seq, c, grp, grp_b, has_s0, want_state):
    q_ref, zf_ref, zb_ref, v_ref, g_ref, lb_ref, nw_ref, ms_ref, mb_ref = refs[:9]
    pos = 9
    s0_ref = None
    if has_s0:
        s0_ref = refs[pos]
        pos += 1
    o_ref = refs[pos]
    pos += 1
    sout_ref = None
    if want_state:
        sout_ref = refs[pos]
        pos += 1
    lfb_scr, kb_scr, sb_scr, e_scr, qs_scr, kf_scr, gate_scr = refs[pos:pos + 7]

    gc = grp * c
    n_iters = seq // gc
    n_levels = c.bit_length() - 1
    base = 2 * (n_levels - 1) * c
    lb_f = lb_ref[0:1, :]
    lb_b = lb_ref[1:2, :]
    nw = nw_ref[...]
    row = lax.broadcasted_iota(jnp.int32, (c, HEAD_DIM), 0)
    pair = (lax.broadcasted_iota(jnp.int32, (c, c), 0)
            ^ lax.broadcasted_iota(jnp.int32, (c, c), 1))
    level = jnp.full((c, c), -1, jnp.int32)
    for k in range(n_levels):
        level = level + (pair >= (1 << k)).astype(jnp.int32)

    if has_s0:
        stf0 = s0_ref[0].T
        stb0 = s0_ref[1].T
    else:
        stf0 = jnp.zeros((HEAD_DIM, HEAD_DIM), F32)
        stb0 = stf0

    def chunk_rows(x, gi):
        return x[gi * c:(gi + 1) * c]

    def chunk_lanes(x, gi):
        return x[:, gi * HEAD_DIM:(gi + 1) * HEAD_DIM]

    def stack_chunks(parts, n):
        return jnp.concatenate(
            [jnp.concatenate([chunk_rows(p, gi) for p in parts], axis=0) for gi in range(n)],
            axis=1)

    gcb = grp_b * c

    def sweep_b(i, stb):
        it = seq // gcb - 1 - i
        rows = pl.ds(pl.multiple_of(it * gcb, gcb), gcb)
        l2f_b, k_b = _gates(zb_ref[rows, :], lb_b)
        lfb_scr[rows, :] = l2f_b
        kb_scr[rows, :] = k_b
        v_bf = v_ref[rows, :].astype(BF16)
        e = jnp.exp2(jnp.dot(mb_ref[...], stack_chunks(_split_bf16(l2f_b), grp_b),
                             preferred_element_type=F32))
        kv = []
        for gi in range(grp_b):
            k_hat = (chunk_rows(k_b, gi) * chunk_lanes(e, gi)[0:c]).astype(BF16)
            kv.append(_dot_tn(chunk_rows(v_bf, gi), k_hat))
        for gi in reversed(range(grp_b)):
            sb_scr[it * grp_b + gi] = stb
            stb = stb * chunk_lanes(e, gi)[c:c + 1] + kv[gi]
        return stb

    stb_final = lax.fori_loop(0, seq // gcb, sweep_b, stb0)

    def block_rows(blk):
        return pl.ds(pl.multiple_of(blk * gc, gc), gc)

    def stage_a_gates(blk):
        rows = block_rows(blk)
        qs = _silu(q_ref[rows, :])
        gate = _silu(g_ref[rows, :])
        l2f_f, k_f = _gates(zf_ref[rows, :], lb_f)
        fh, fl = _split_bf16(l2f_f)
        bh, bl = _split_bf16(lfb_scr[rows, :])
        return qs, gate, k_f, stack_chunks((fh, bh, fl, bl), grp)

    def stage_a_finish(slot, vals):
        qs, gate, k_f, rhs = vals
        e_scr[slot] = jnp.exp2(jnp.dot(ms_ref[...], rhs, preferred_element_type=F32))
        qs_scr[slot] = qs
        gate_scr[slot] = gate
        kf_scr[slot] = k_f

    def stage_b_levels(blk, slot):
        rows = block_rows(blk)
        vbf_all = v_ref[rows, :].astype(BF16)
        kb_all = kb_scr[rows, :]
        scores, kv, dec = [], [], []
        for gi in range(grp):
            lanes = slice(gi * HEAD_DIM, (gi + 1) * HEAD_DIM)
            crow = slice(gi * c, (gi + 1) * c)
            qs = qs_scr[slot, crow, :]
            k_f = kf_scr[slot, crow, :]
            k_b = chunk_rows(kb_all, gi)
            odd = (row & 1) != 0
            x = (qs * (1.0 - jnp.where(odd, k_f, k_b))).astype(BF16)
            y = jnp.where(odd, k_b, k_f).astype(BF16)
            a = jnp.where(level == 0, _dot_nt(x, y), 0.0)
            for lv in range(1, n_levels):
                second = (row & (1 << lv)) != 0
                r0 = 2 * (lv - 1) * c
                x = (qs * e_scr[slot, r0:r0 + c, lanes]).astype(BF16)
                y = (jnp.where(second, k_b, k_f)
                     * e_scr[slot, r0 + c:r0 + 2 * c, lanes]).astype(BF16)
                a = jnp.where(level == lv, _dot_nt(x, y), a)
            scores.append(a.astype(BF16))
            k_hat = (k_f * e_scr[slot, base + 2 * c:base + 3 * c, lanes]).astype(BF16)
            kv.append(_dot_tn(chunk_rows(vbf_all, gi), k_hat))
            dec.append(e_scr[slot, base + 3 * c:base + 3 * c + 1, lanes])
        return scores, kv, dec

    def stage_b_finish(blk, slot, stf, lv_out):
        scores, kv, dec = lv_out
        rows = block_rows(blk)
        v_all = v_ref[rows, :]
        vbf_all = v_all.astype(BF16)
        kb_all = kb_scr[rows, :]
        states = []
        for gi in range(grp):
            states.append(stf)
            stf = stf * dec[gi] + kv[gi]
        outs = []
        for gi in range(grp):
            lanes = slice(gi * HEAD_DIM, (gi + 1) * HEAD_DIM)
            crow = slice(gi * c, (gi + 1) * c)
            qs = qs_scr[slot, crow, :]
            q_fb = jnp.concatenate([qs * e_scr[slot, base:base + c, lanes],
                                    qs * e_scr[slot, base + c:base + 2 * c, lanes]],
                                   axis=1).astype(BF16)
            st_fb = jnp.concatenate([states[gi], sb_scr[blk * grp + gi]], axis=1).astype(BF16)
            o = jnp.dot(scores[gi], chunk_rows(vbf_all, gi), preferred_element_type=F32)
            o = o + jnp.sum(qs * (kf_scr[slot, crow, :] + chunk_rows(kb_all, gi)),
                            axis=-1, keepdims=True) * chunk_rows(v_all, gi)
            o = o + _dot_nt(q_fb, st_fb)
            o = o * lax.rsqrt(jnp.mean(o * o, axis=-1, keepdims=True) + EPS) * nw
            outs.append(o * gate_scr[slot, crow, :])
        o_ref[rows, :] = jnp.concatenate(outs, axis=0).astype(o_ref.dtype)
        return stf

    def step(cur, cur_slot, nxt, stf):
        ahead = stage_a_gates(nxt)
        lv_out = stage_b_levels(cur, cur_slot)
        stage_a_finish(1 - cur_slot, ahead)
        return stage_b_finish(cur, cur_slot, stf, lv_out)

    stage_a_finish(0, stage_a_gates(0))
    if n_iters == 1:
        stf_final = stage_b_finish(0, 0, stf0, stage_b_levels(0, 0))
    else:
        assert n_iters % 2 == 0

        def sweep_f(j, stf):
            stf = step(2 * j, 0, 2 * j + 1, stf)
            return step(2 * j + 1, 1, jnp.minimum(2 * j + 2, n_iters - 1), stf)

        stf_final = lax.fori_loop(0, n_iters // 2, sweep_f, stf0)

    if want_state:
        sout_ref[0] = stf_final.T
        sout_ref[1] = stb_final.T


def _scan(proj, lb, hg_nw, s0, layer, batch, seq, want_state, c=64, grp=8, grp_b=8):
    t = proj.shape[0]
    n_heads = lb.shape[1] // HEAD_DIM
    has_s0 = s0 is not None
    grp = min(grp, seq // c)
    grp_b = min(grp_b, seq // c)
    m_sweep, m_back = _decay_matrices(c)

    def sec(k):
        return pl.BlockSpec((seq, HEAD_DIM), lambda b, h, k=k: (b, k * n_heads + h))

    state_block = (None, None, 2, None, HEAD_DIM, HEAD_DIM)
    state_map = lambda b, h: (b, layer, 0, h, 0, 0)
    in_specs = [sec(0), sec(1), sec(2), sec(3), sec(4),
                pl.BlockSpec((2, HEAD_DIM), lambda b, h: (0, h)),
                pl.BlockSpec((1, HEAD_DIM), lambda b, h: (0, 0)),
                pl.BlockSpec(m_sweep.shape, lambda b, h: (0, 0)),
                pl.BlockSpec(m_back.shape, lambda b, h: (0, 0))]
    args = [proj, proj, proj, proj, proj, lb, hg_nw.reshape(1, HEAD_DIM), m_sweep, m_back]
    if has_s0:
        in_specs.append(pl.BlockSpec(state_block, state_map))
        args.append(s0)
    out_specs = [pl.BlockSpec((seq, HEAD_DIM), lambda b, h: (b, h))]
    out_shape = [jax.ShapeDtypeStruct((t, n_heads * HEAD_DIM), BF16)]
    if want_state:
        out_specs.append(pl.BlockSpec((None, 2, None, HEAD_DIM, HEAD_DIM),
                                      lambda b, h: (b, 0, h, 0, 0)))
        out_shape.append(jax.ShapeDtypeStruct((batch, 2, n_heads, HEAD_DIM, HEAD_DIM), F32))
    gc = grp * c
    res = pl.pallas_call(
        functools.partial(_scan_kernel, seq=seq, c=c, grp=grp, grp_b=grp_b, has_s0=has_s0,
                          want_state=want_state),
        grid=(batch, n_heads),
        in_specs=in_specs,
        out_specs=out_specs,
        out_shape=out_shape,
        scratch_shapes=[pltpu.VMEM((seq, HEAD_DIM), F32),
                        pltpu.VMEM((seq, HEAD_DIM), F32),
                        pltpu.VMEM((seq // c, HEAD_DIM, HEAD_DIM), F32),
                        pltpu.VMEM((2, m_sweep.shape[0], grp * HEAD_DIM), F32),
                        pltpu.VMEM((2, gc, HEAD_DIM), F32),
                        pltpu.VMEM((2, gc, HEAD_DIM), F32),
                        pltpu.VMEM((2, gc, HEAD_DIM), F32)],
        compiler_params=_params(("arbitrary", "arbitrary")),
        name="scan",
    )(*args)
    return (res[0], res[1]) if want_state else (res[0], None)


def _window_mean(a, w, pos, n, stride):
    rows = a.shape[0]
    half = w // 2
    fwd = a
    ln = 1
    while ln < half:
        fwd = fwd + jnp.where(pos < n - ln, pltpu.roll(fwd, rows - ln * stride, 0), 0.0)
        ln *= 2
    bwd = jnp.where(pos >= 1, pltpu.roll(a, stride, 0), 0.0)
    ln = 1
    while ln < half:
        bwd = bwd + jnp.where(pos >= ln, pltpu.roll(bwd, ln * stride, 0), 0.0)
        ln *= 2
    cnt = jnp.minimum(pos + (w - half), n) - jnp.maximum(pos - half, 0)
    return (fwd + bwd) / cnt.astype(F32)


def _pool_kernel(p_ref, w_ref, sc_ref, o_ref, *, seq, on_grid):
    grp = pl.program_id(1)
    for gi, w in enumerate(POOL_WINDOWS):
        @pl.when(grp == gi)
        def _(w=w):
            pg = p_ref[...]
            row = lax.broadcasted_iota(jnp.int32, pg.shape, 0)
            if on_grid:
                m = _window_mean(pg, w, row % GRID_W, GRID_W, 1)
                m = _window_mean(m, w, row // GRID_W, seq // GRID_W, GRID_W)
            else:
                m = _window_mean(pg, w, row, seq, 1)
            d = (m - pg).astype(BF16)
            y = jnp.dot(d, w_ref[...], preferred_element_type=F32) * sc_ref[...]
            o_ref[...] = y.astype(o_ref.dtype)


def _pool(proj, w_pool_l, pool_scale_l, col0, batch, seq, on_grid):
    t = proj.shape[0]
    n_grp, gw, _ = w_pool_l.shape
    blk0 = col0 // gw
    return pl.pallas_call(
        functools.partial(_pool_kernel, seq=seq, on_grid=on_grid),
        grid=(batch, n_grp),
        in_specs=[pl.BlockSpec((seq, gw), lambda b, g: (b, blk0 + g)),
                  pl.BlockSpec((None, gw, gw), lambda b, g: (g, 0, 0)),
                  pl.BlockSpec((1, gw), lambda b, g: (0, g))],
        out_specs=pl.BlockSpec((seq, gw), lambda b, g: (b, g)),
        out_shape=jax.ShapeDtypeStruct((t, n_grp * gw), BF16),
        compiler_params=_params(("arbitrary", "arbitrary")),
        name="pool",
    )(proj, w_pool_l, pool_scale_l.reshape(1, n_grp * gw))


def _mix_kernel(o_ref, pm_ref, x_ref, mod_ref, w_ref, out_ref):
    k = o_ref.shape[1]
    mix = (jnp.dot(o_ref[...], w_ref[0:k, :], preferred_element_type=F32)
           + jnp.dot(pm_ref[...], w_ref[k:, :], preferred_element_type=F32))
    out_ref[...] = x_ref[...] + mod_ref[0][2:3] * mix


def _mix(o, pm, x2, mod, rows_per_mod, w_out, tm=512):
    t, d = x2.shape
    return pl.pallas_call(
        _mix_kernel,
        grid=(t // tm,),
        in_specs=[pl.BlockSpec((tm, o.shape[1]), lambda i: (i, 0)),
                  pl.BlockSpec((tm, pm.shape[1]), lambda i: (i, 0)),
                  pl.BlockSpec((tm, d), lambda i: (i, 0)),
                  pl.BlockSpec((1, N_MOD, d), lambda i: ((i * tm) // rows_per_mod, 0, 0)),
                  pl.BlockSpec(w_out.shape, lambda i: (0, 0))],
        out_specs=pl.BlockSpec((tm, d), lambda i: (i, 0)),
        out_shape=jax.ShapeDtypeStruct((t, d), F32),
        compiler_params=_params(("arbitrary",)),
        name="mix",
    )(o, pm, x2, mod, w_out)


def _ffn_up_kernel(x_ref, xp_ref, xn_ref, mod_ref, nw_ref, wa_ref, wb_ref, cwa_ref, cwb_ref,
                   cba_ref, cbb_ref, o_ref, h_ref, *, tm, seq):
    i = pl.program_id(0)

    @pl.when(pl.program_id(1) == 0)
    def _():
        mod = mod_ref[0]
        nw = nw_ref[...]
        sc, sh = mod[4:5], mod[3:4]
        starts_seq = (i * tm) % seq == 0
        ends_seq = ((i + 1) * tm) % seq == 0
        above = jnp.where(starts_seq, 0.0, _norm_mod(xp_ref[...], nw, sc, sh))
        below = jnp.where(ends_seq, 0.0, _norm_mod(xn_ref[...], nw, sc, sh))
        h_ref[0:CONV_HALO, :] = above.astype(BF16)
        h_ref[CONV_HALO:CONV_HALO + tm, :] = _norm_mod(x_ref[...], nw, sc, sh).astype(BF16)
        h_ref[CONV_HALO + tm:, :] = below.astype(BF16)

    ext = tm + 2 * CONV_HALO
    h = h_ref[...]

    def conv(w_ref, cw_ref, cb_ref):
        u = jnp.dot(h, w_ref[...], preferred_element_type=F32)
        um = u[CONV_HALO:CONV_HALO + tm]
        up = pltpu.roll(u, 1, 0)[CONV_HALO:CONV_HALO + tm]
        un = pltpu.roll(u, ext - 1, 0)[CONV_HALO:CONV_HALO + tm]
        if tm > seq:
            in_seq = lax.broadcasted_iota(jnp.int32, um.shape, 0) % seq
            up = jnp.where(in_seq == 0, 0.0, up)
            un = jnp.where(in_seq == seq - 1, 0.0, un)
        cw = cw_ref[...]
        return up * cw[0:1] + um * cw[1:2] + un * cw[2:3] + cb_ref[...]

    a = conv(wa_ref, cwa_ref, cba_ref)
    b = conv(wb_ref, cwb_ref, cbb_ref)
    o_ref[...] = (_silu(a) * b).astype(o_ref.dtype)


def _ffn_down_kernel(a_ref, x_ref, mod_ref, w_ref, fnw_ref, o_ref, *, final_norm):
    y = x_ref[...] + mod_ref[0][5:6] * jnp.dot(a_ref[...], w_ref[...],
                                               preferred_element_type=F32)
    if final_norm:
        y = y * lax.rsqrt(jnp.mean(y * y, axis=-1, keepdims=True) + EPS) * fnw_ref[...]
    o_ref[...] = y


def _ffn(x2, mod, rows_per_mod, seq, nw, w_up, conv_w, conv_b, w_down, fnw, final_norm,
         tm=1024, tf=512, tm_down=256):
    t, d = x2.shape
    dff = w_down.shape[0]
    nf = dff // tf
    assert seq % tm == 0 or tm % seq == 0
    hb = tm // CONV_HALO
    last_hb = t // CONV_HALO - 1
    cb = conv_b.reshape(1, 2 * dff)
    mod_spec = lambda tile: pl.BlockSpec((1, N_MOD, d),
                                         lambda i, *_: ((i * tile) // rows_per_mod, 0, 0))
    act = pl.pallas_call(
        functools.partial(_ffn_up_kernel, tm=tm, seq=seq),
        grid=(t // tm, nf),
        in_specs=[pl.BlockSpec((tm, d), lambda i, f: (i, 0)),
                  pl.BlockSpec((CONV_HALO, d), lambda i, f: (jnp.maximum(i * hb - 1, 0), 0)),
                  pl.BlockSpec((CONV_HALO, d), lambda i, f: (jnp.minimum((i + 1) * hb, last_hb), 0)),
                  mod_spec(tm),
                  pl.BlockSpec((1, d), lambda i, f: (0, 0)),
                  pl.BlockSpec((d, tf), lambda i, f: (0, f)),
                  pl.BlockSpec((d, tf), lambda i, f: (0, f + nf)),
                  pl.BlockSpec((3, tf), lambda i, f: (0, f)),
                  pl.BlockSpec((3, tf), lambda i, f: (0, f + nf)),
                  pl.BlockSpec((1, tf), lambda i, f: (0, f)),
                  pl.BlockSpec((1, tf), lambda i, f: (0, f + nf))],
        out_specs=pl.BlockSpec((tm, tf), lambda i, f: (i, f)),
        out_shape=jax.ShapeDtypeStruct((t, dff), BF16),
        scratch_shapes=[pltpu.VMEM((tm + 2 * CONV_HALO, d), BF16)],
        compiler_params=_params(("arbitrary", "arbitrary")),
        name="ffn_up",
    )(x2, x2, x2, mod, nw.reshape(1, d), w_up, w_up, conv_w, conv_w, cb, cb)
    return pl.pallas_call(
        functools.partial(_ffn_down_kernel, final_norm=final_norm),
        grid=(t // tm_down,),
        in_specs=[pl.BlockSpec((tm_down, dff), lambda i: (i, 0)),
                  pl.BlockSpec((tm_down, d), lambda i: (i, 0)),
                  mod_spec(tm_down),
                  pl.BlockSpec((dff, d), lambda i: (0, 0), pipeline_mode=pl.Buffered(1)),
                  pl.BlockSpec((1, d), lambda i: (0, 0))],
        out_specs=pl.BlockSpec((tm_down, d), lambda i: (i, 0)),
        out_shape=jax.ShapeDtypeStruct((t, d), F32),
        compiler_params=_params(("arbitrary",)),
        name="ffn_down",
    )(act, x2, mod, w_down, fnw.reshape(1, d))


def kernel(x_prompt, x_sample, state_hgrn, c, c_ctx, w_ada, b_ada, norm1_w, w_in, lb_param,
           hg_norm_w, w_pool, pool_scale, w_out, norm2_w, w_up, conv_w, conv_b, w_down,
           final_norm_w):
    depth = w_in.shape[0]
    bp, lp, d = x_prompt.shape
    bs, ls, _ = x_sample.shape
    hg_width = lb_param.shape[2]

    lb_all = jnp.cumsum(jax.nn.softmax(lb_param.astype(F32), axis=1), axis=1)
    n_c = 1 + bs
    pad = (-n_c) % 8
    cvecs = jnp.concatenate([c_ctx[None, :], c, jnp.zeros((pad, d), F32)], axis=0)

    xp = x_prompt.reshape(bp * lp, d)
    xs = x_sample.reshape(bs * ls, d)
    ctx_states = []
    for l in range(depth):
        last = l == depth - 1
        mods = _ada(cvecs, w_ada[l], b_ada[l])
        mod_p = mods[0:1].reshape(1, N_MOD, d)
        mod_s = mods[1:n_c].reshape(bs, N_MOD, d)
        w_in_l = w_in[l].astype(BF16)
        w_pool_l = w_pool[l].astype(BF16)
        w_out_l = w_out[l].astype(BF16)
        w_up_l = w_up[l].astype(BF16)
        w_down_l = w_down[l].astype(BF16)
        lb = lb_all[:, l]

        def layer(x2, mod, rows_per_mod, batch, seq, s0, on_grid, want_state):
            proj = _in_proj(x2, mod, rows_per_mod, norm1_w[l], w_in_l)
            o, st = _scan(proj, lb, hg_norm_w[l], s0, l, batch, seq, want_state)
            pm = _pool(proj, w_pool_l, pool_scale[l], 5 * hg_width, batch, seq, on_grid)
            x2 = _mix(o, pm, x2, mod, rows_per_mod, w_out_l)
            x2 = _ffn(x2, mod, rows_per_mod, seq, norm2_w[l], w_up_l, conv_w[l], conv_b[l],
                      w_down_l, final_norm_w, last)
            return x2, st

        xp, st = layer(xp, mod_p, bp * lp, bp, lp, None, False, True)
        ctx_states.append(st)
        xs, _ = layer(xs, mod_s, ls, bs, ls, state_hgrn, True, False)

    y_prompt = xp.reshape(bp, lp, d)
    y_sample = xs.reshape(bs, ls, d)
    new_state = jnp.stack(ctx_states, axis=1)
    return (y_prompt, y_sample, new_state)
```

```python
import functools

import jax
import jax.numpy as jnp
import numpy as np
from jax import lax
from jax.experimental import pallas as pl
from jax.experimental.pallas import tpu as pltpu

F32 = jnp.float32
BF16 = jnp.bfloat16
EPS = 1e-6

HEAD_DIM = 128
N_MOD = 6
POOL_WINDOWS = (2, 4, 8, 16)
GRID_W = 64
CONV_HALO = 8
VMEM_LIMIT = 56 * 1024 * 1024


def _params(sem):
    return pltpu.CompilerParams(dimension_semantics=sem, vmem_limit_bytes=VMEM_LIMIT)


def _silu(x):
    hx = 0.5 * x
    return hx + hx * jnp.tanh(hx)


def _norm_mod(x, nw, sc, sh):
    scale = nw * (1.0 + sc)
    return x * lax.rsqrt(jnp.mean(x * x, axis=-1, keepdims=True) + EPS) * scale + sh


def _ada_kernel(c_ref, w_ref, b_ref, o_ref):
    s = _silu(c_ref[...]).astype(BF16)
    o_ref[...] = jnp.dot(s, w_ref[...].astype(BF16), preferred_element_type=F32) + b_ref[...]


def _ada(cvecs, w_ada, b_ada, tn=1024):
    r, d = cvecs.shape
    n = w_ada.shape[1]
    return pl.pallas_call(
        _ada_kernel,
        grid=(n // tn,),
        in_specs=[pl.BlockSpec((r, d), lambda j: (0, 0)),
                  pl.BlockSpec((d, tn), lambda j: (0, j)),
                  pl.BlockSpec((1, tn), lambda j: (0, j))],
        out_specs=pl.BlockSpec((r, tn), lambda j: (0, j)),
        out_shape=jax.ShapeDtypeStruct((r, n), F32),
        compiler_params=_params(("arbitrary",)),
        name="ada",
    )(cvecs, w_ada, b_ada.reshape(1, n))


def _in_proj_kernel(x_ref, mod_ref, nw_ref, w_ref, o_ref):
    mod = mod_ref[0]
    h = _norm_mod(x_ref[...], nw_ref[...], mod[1:2], mod[0:1]).astype(BF16)
    o_ref[...] = jnp.dot(h, w_ref[...], preferred_element_type=F32)


def _in_proj(x2, mod, rows_per_mod, nw, w, tm=256):
    t, d = x2.shape
    n = w.shape[1]
    return pl.pallas_call(
        _in_proj_kernel,
        grid=(t // tm,),
        in_specs=[pl.BlockSpec((tm, d), lambda i: (i, 0)),
                  pl.BlockSpec((1, N_MOD, d), lambda i: ((i * tm) // rows_per_mod, 0, 0)),
                  pl.BlockSpec((1, d), lambda i: (0, 0)),
                  pl.BlockSpec((d, n), lambda i: (0, 0), pipeline_mode=pl.Buffered(1))],
        out_specs=pl.BlockSpec((tm, n), lambda i: (i, 0)),
        out_shape=jax.ShapeDtypeStruct((t, n), F32),
        compiler_params=_params(("arbitrary",)),
        name="in_proj",
    )(x2, mod, nw.reshape(1, d), w)


LOG2E = 1.4426950408889634
TOTAL_ROWS = 16


def _decay_matrices(c):
    t = np.arange(c)
    blocks = []
    m = 2
    while m < c:
        mid = (t // (2 * m)) * (2 * m) + m
        second = (t & m) != 0
        qm = np.zeros((c, 2 * c), np.float32)
        km = np.zeros((c, 2 * c), np.float32)
        for i in range(c):
            if second[i]:
                qm[i, mid[i]:i + 1] = 1
                km[i, c + mid[i]:c + i] = 1
            else:
                qm[i, c + i:c + mid[i]] = 1
                km[i, i + 1:mid[i]] = 1
        blocks += [qm, km]
        m *= 2
    qf = np.zeros((c, 2 * c), np.float32)
    qb = np.zeros((c, 2 * c), np.float32)
    kf = np.zeros((c, 2 * c), np.float32)
    kb = np.zeros((c, c), np.float32)
    for i in range(c):
        qf[i, 0:i + 1] = 1
        qb[i, c + i:2 * c] = 1
        kf[i, i + 1:c] = 1
        kb[i, 0:i] = 1
    tf = np.zeros((TOTAL_ROWS, 2 * c), np.float32)
    tf[:, 0:c] = 1
    tb = np.ones((TOTAL_ROWS, c), np.float32)
    sweep = np.concatenate(blocks + [qf, qb, kf, tf], axis=0)
    back = np.concatenate([kb, tb], axis=0)
    return (jnp.asarray(np.concatenate([sweep, sweep], axis=1), BF16),
            jnp.asarray(np.concatenate([back, back], axis=1), BF16))


def _gates(z, lbd):
    f = lbd + (1.0 - lbd) * jax.nn.sigmoid(z)
    return jnp.log(f) * LOG2E, 1.0 - f


def _split_bf16(x):
    hi = x.astype(BF16)
    return hi, (x - hi.astype(F32)).astype(BF16)


def _dot_nt(a, b):
    return lax.dot_general(a, b, (((1,), (1,)), ((), ())), preferred_element_type=F32)


def _dot_tn(a, b):
    return lax.dot_general(a, b, (((0,), (0,)), ((), ())), preferred_element_type=F32)


def _scan_kernel(*refs, seq, c, grp, grp_b, has_s0, want_state):
    q_ref, zf_ref, zb_ref, v_ref, g_ref, lb_ref, nw_ref, ms_ref, mb_ref = refs[:9]
    pos = 9
    s0_ref = None
    if has_s0:
        s0_ref = refs[pos]
        pos += 1
    o_ref = refs[pos]
    pos += 1
    sout_ref = None
    if want_state:
        sout_ref = refs[pos]
        pos += 1
    lfb_scr, kb_scr, sb_scr, e_scr, qs_scr, kf_scr, gate_scr = refs[pos:pos + 7]

    gc = grp * c
    n_iters = seq // gc
    n_levels = c.bit_length() - 1
    base = 2 * (n_levels - 1) * c
    lb_f = lb_ref[0:1, :]
    lb_b = lb_ref[1:2, :]
    nw = nw_ref[...]
    row = lax.broadcasted_iota(jnp.int32, (c, HEAD_DIM), 0)
    pair = (lax.broadcasted_iota(jnp.int32, (c, c), 0)
            ^ lax.broadcasted_iota(jnp.int32, (c, c), 1))
    level = jnp.full((c, c), -1, jnp.int32)
    for k in range(n_levels):
        level = level + (pair >= (1 << k)).astype(jnp.int32)

    if has_s0:
        stf0 = s0_ref[0].T
        stb0 = s0_ref[1].T
    else:
        stf0 = jnp.zeros((HEAD_DIM, HEAD_DIM), F32)
        stb0 = stf0

    def chunk_rows(x, gi):
        return x[gi * c:(gi + 1) * c]

    def chunk_lanes(x, gi):
        return x[:, gi * HEAD_DIM:(gi + 1) * HEAD_DIM]

    def stack_chunks(parts, n):
        return jnp.concatenate(
            [jnp.concatenate([chunk_rows(p, gi) for p in parts], axis=0) for gi in range(n)],
            axis=1)

    gcb = grp_b * c

    def sweep_b(i, stb):
        it = seq // gcb - 1 - i
        rows = pl.ds(pl.multiple_of(it * gcb, gcb), gcb)
        l2f_b, k_b = _gates(zb_ref[rows, :], lb_b)
        lfb_scr[rows, :] = l2f_b
        kb_scr[rows, :] = k_b
        v_bf = v_ref[rows, :].astype(BF16)
        e = jnp.exp2(jnp.dot(mb_ref[...], stack_chunks(_split_bf16(l2f_b), grp_b),
                             preferred_element_type=F32))
        kv = []
        for gi in range(grp_b):
            k_hat = (chunk_rows(k_b, gi) * chunk_lanes(e, gi)[0:c]).astype(BF16)
            kv.append(_dot_tn(chunk_rows(v_bf, gi), k_hat))
        for gi in reversed(range(grp_b)):
            sb_scr[it * grp_b + gi] = stb
            stb = stb * chunk_lanes(e, gi)[c:c + 1] + kv[gi]
        return stb

    stb_final = lax.fori_loop(0, seq // gcb, sweep_b, stb0, unroll=True)

    def block_rows(blk):
        return pl.ds(pl.multiple_of(blk * gc, gc), gc)

    def stage_a_gates(blk):
        rows = block_rows(blk)
        qs = _silu(q_ref[rows, :])
        gate = _silu(g_ref[rows, :])
        l2f_f, k_f = _gates(zf_ref[rows, :], lb_f)
        fh, fl = _split_bf16(l2f_f)
        bh, bl = _split_bf16(lfb_scr[rows, :])
        return qs, gate, k_f, stack_chunks((fh, bh, fl, bl), grp)

    def stage_a_finish(slot, vals):
        qs, gate, k_f, rhs = vals
        e_scr[slot] = jnp.exp2(jnp.dot(ms_ref[...], rhs, preferred_element_type=F32))
        qs_scr[slot] = qs
        gate_scr[slot] = gate
        kf_scr[slot] = k_f

    def stage_b_levels(blk, slot):
        rows = block_rows(blk)
        vbf_all = v_ref[rows, :].astype(BF16)
        kb_all = kb_scr[rows, :]
        scores, kv, dec = [], [], []
        for gi in range(grp):
            lanes = slice(gi * HEAD_DIM, (gi + 1) * HEAD_DIM)
            crow = slice(gi * c, (gi + 1) * c)
            qs = qs_scr[slot, crow, :]
            k_f = kf_scr[slot, crow, :]
            k_b = chunk_rows(kb_all, gi)
            odd = (row & 1) != 0
            x = (qs * (1.0 - jnp.where(odd, k_f, k_b))).astype(BF16)
            y = jnp.where(odd, k_b, k_f).astype(BF16)
            a = jnp.where(level == 0, _dot_nt(x, y), 0.0)
            for lv in range(1, n_levels):
                second = (row & (1 << lv)) != 0
                r0 = 2 * (lv - 1) * c
                x = (qs * e_scr[slot, r0:r0 + c, lanes]).astype(BF16)
                y = (jnp.where(second, k_b, k_f)
                     * e_scr[slot, r0 + c:r0 + 2 * c, lanes]).astype(BF16)
                a = jnp.where(level == lv, _dot_nt(x, y), a)
            scores.append(a.astype(BF16))
            k_hat = (k_f * e_scr[slot, base + 2 * c:base + 3 * c, lanes]).astype(BF16)
            kv.append(_dot_tn(chunk_rows(vbf_all, gi), k_hat))
            dec.append(e_scr[slot, base + 3 * c:base + 3 * c + 1, lanes])
        return scores, kv, dec

    def stage_b_finish(blk, slot, stf, lv_out):
        scores, kv, dec = lv_out
        rows = block_rows(blk)
        v_all = v_ref[rows, :]
        vbf_all = v_all.astype(BF16)
        kb_all = kb_scr[rows, :]
        states = []
        for gi in range(grp):
            states.append(stf)
            stf = stf * dec[gi] + kv[gi]
        outs = []
        for gi in range(grp):
            lanes = slice(gi * HEAD_DIM, (gi + 1) * HEAD_DIM)
            crow = slice(gi * c, (gi + 1) * c)
            qs = qs_scr[slot, crow, :]
            q_fb = jnp.concatenate([qs * e_scr[slot, base:base + c, lanes],
                                    qs * e_scr[slot, base + c:base + 2 * c, lanes]],
                                   axis=1).astype(BF16)
            st_fb = jnp.concatenate([states[gi], sb_scr[blk * grp + gi]], axis=1).astype(BF16)
            o = jnp.dot(scores[gi], chunk_rows(vbf_all, gi), preferred_element_type=F32)
            o = o + jnp.sum(qs * (kf_scr[slot, crow, :] + chunk_rows(kb_all, gi)),
                            axis=-1, keepdims=True) * chunk_rows(v_all, gi)
            o = o + _dot_nt(q_fb, st_fb)
            o = o * lax.rsqrt(jnp.mean(o * o, axis=-1, keepdims=True) + EPS) * nw
            outs.append(o * gate_scr[slot, crow, :])
        o_ref[rows, :] = jnp.concatenate(outs, axis=0).astype(o_ref.dtype)
        return stf

    def step(cur, cur_slot, nxt, stf):
        ahead = stage_a_gates(nxt)
        lv_out = stage_b_levels(cur, cur_slot)
        stage_a_finish(1 - cur_slot, ahead)
        return stage_b_finish(cur, cur_slot, stf, lv_out)

    stage_a_finish(0, stage_a_gates(0))
    if n_iters == 1:
        stf_final = stage_b_finish(0, 0, stf0, stage_b_levels(0, 0))
    else:
        assert n_iters % 2 == 0

        def sweep_f(j, stf):
            stf = step(2 * j, 0, 2 * j + 1, stf)
            return step(2 * j + 1, 1, jnp.minimum(2 * j + 2, n_iters - 1), stf)

        stf_final = lax.fori_loop(0, n_iters // 2, sweep_f, stf0)

    if want_state:
        sout_ref[0] = stf_final.T
        sout_ref[1] = stb_final.T


def _scan(proj, lb, hg_nw, s0, layer, batch, seq, want_state, c=64, grp=8, grp_b=8):
    t = proj.shape[0]
    n_heads = lb.shape[1] // HEAD_DIM
    has_s0 = s0 is not None
    grp = min(grp, seq // c)
    grp_b = min(grp_b, seq // c)
    m_sweep, m_back = _decay_matrices(c)

    def sec(k):
        return pl.BlockSpec((seq, HEAD_DIM), lambda b, h, k=k: (b, k * n_heads + h))

    state_block = (None, None, 2, None, HEAD_DIM, HEAD_DIM)
    state_map = lambda b, h: (b, layer, 0, h, 0, 0)
    in_specs = [sec(0), sec(1), sec(2), sec(3), sec(4),
                pl.BlockSpec((2, HEAD_DIM), lambda b, h: (0, h)),
                pl.BlockSpec((1, HEAD_DIM), lambda b, h: (0, 0)),
                pl.BlockSpec(m_sweep.shape, lambda b, h: (0, 0)),
                pl.BlockSpec(m_back.shape, lambda b, h: (0, 0))]
    args = [proj, proj, proj, proj, proj, lb, hg_nw.reshape(1, HEAD_DIM), m_sweep, m_back]
    if has_s0:
        in_specs.append(pl.BlockSpec(state_block, state_map))
        args.append(s0)
    out_specs = [pl.BlockSpec((seq, HEAD_DIM), lambda b, h: (b, h))]
    out_shape = [jax.ShapeDtypeStruct((t, n_heads * HEAD_DIM), BF16)]
    if want_state:
        out_specs.append(pl.BlockSpec((None, 2, None, HEAD_DIM, HEAD_DIM),
                                      lambda b, h: (b, 0, h, 0, 0)))
        out_shape.append(jax.ShapeDtypeStruct((batch, 2, n_heads, HEAD_DIM, HEAD_DIM), F32))
    gc = grp * c
    res = pl.pallas_call(
        functools.partial(_scan_kernel, seq=seq, c=c, grp=grp, grp_b=grp_b, has_s0=has_s0,
                          want_state=want_state),
        grid=(batch, n_heads),
        in_specs=in_specs,
        out_specs=out_specs,
        out_shape=out_shape,
        scratch_shapes=[pltpu.VMEM((seq, HEAD_DIM), F32),
                        pltpu.VMEM((seq, HEAD_DIM), F32),
                        pltpu.VMEM((seq // c, HEAD_DIM, HEAD_DIM), F32),
                        pltpu.VMEM((2, m_sweep.shape[0], grp * HEAD_DIM), F32),
                        pltpu.VMEM((2, gc, HEAD_DIM), F32),
                        pltpu.VMEM((2, gc, HEAD_DIM), F32),
                        pltpu.VMEM((2, gc, HEAD_DIM), F32)],
        compiler_params=_params(("arbitrary", "arbitrary")),
        name="scan",
    )(*args)
    return (res[0], res[1]) if want_state else (res[0], None)


def _window_mean(a, w, pos, n, stride):
    rows = a.shape[0]
    half = w // 2
    fwd = a
    ln = 1
    while ln < half:
        fwd = fwd + jnp.where(pos < n - ln, pltpu.roll(fwd, rows - ln * stride, 0), 0.0)
        ln *= 2
    bwd = jnp.where(pos >= 1, pltpu.roll(a, stride, 0), 0.0)
    ln = 1
    while ln < half:
        bwd = bwd + jnp.where(pos >= ln, pltpu.roll(bwd, ln * stride, 0), 0.0)
        ln *= 2
    cnt = jnp.minimum(pos + (w - half), n) - jnp.maximum(pos - half, 0)
    return (fwd + bwd) / cnt.astype(F32)


def _pool_kernel(p_ref, w_ref, sc_ref, o_ref, *, seq, on_grid):
    grp = pl.program_id(1)
    for gi, w in enumerate(POOL_WINDOWS):
        @pl.when(grp == gi)
        def _(w=w):
            pg = p_ref[...]
            row = lax.broadcasted_iota(jnp.int32, pg.shape, 0)
            if on_grid:
                m = _window_mean(pg, w, row % GRID_W, GRID_W, 1)
                m = _window_mean(m, w, row // GRID_W, seq // GRID_W, GRID_W)
            else:
                m = _window_mean(pg, w, row, seq, 1)
            d = (m - pg).astype(BF16)
            y = jnp.dot(d, w_ref[...], preferred_element_type=F32) * sc_ref[...]
            o_ref[...] = y.astype(o_ref.dtype)


def _pool(proj, w_pool_l, pool_scale_l, col0, batch, seq, on_grid):
    t = proj.shape[0]
    n_grp, gw, _ = w_pool_l.shape
    blk0 = col0 // gw
    return pl.pallas_call(
        functools.partial(_pool_kernel, seq=seq, on_grid=on_grid),
        grid=(batch, n_grp),
        in_specs=[pl.BlockSpec((seq, gw), lambda b, g: (b, blk0 + g)),
                  pl.BlockSpec((None, gw, gw), lambda b, g: (g, 0, 0)),
                  pl.BlockSpec((1, gw), lambda b, g: (0, g))],
        out_specs=pl.BlockSpec((seq, gw), lambda b, g: (b, g)),
        out_shape=jax.ShapeDtypeStruct((t, n_grp * gw), BF16),
        compiler_params=_params(("arbitrary", "arbitrary")),
        name="pool",
    )(proj, w_pool_l, pool_scale_l.reshape(1, n_grp * gw))


def _mix_kernel(o_ref, pm_ref, x_ref, mod_ref, w_ref, out_ref):
    k = o_ref.shape[1]
    mix = (jnp.dot(o_ref[...], w_ref[0:k, :], preferred_element_type=F32)
           + jnp.dot(pm_ref[...], w_ref[k:, :], preferred_element_type=F32))
    out_ref[...] = x_ref[...] + mod_ref[0][2:3] * mix


def _mix(o, pm, x2, mod, rows_per_mod, w_out, tm=512):
    t, d = x2.shape
    return pl.pallas_call(
        _mix_kernel,
        grid=(t // tm,),
        in_specs=[pl.BlockSpec((tm, o.shape[1]), lambda i: (i, 0)),
                  pl.BlockSpec((tm, pm.shape[1]), lambda i: (i, 0)),
                  pl.BlockSpec((tm, d), lambda i: (i, 0)),
                  pl.BlockSpec((1, N_MOD, d), lambda i: ((i * tm) // rows_per_mod, 0, 0)),
                  pl.BlockSpec(w_out.shape, lambda i: (0, 0))],
        out_specs=pl.BlockSpec((tm, d), lambda i: (i, 0)),
        out_shape=jax.ShapeDtypeStruct((t, d), F32),
        compiler_params=_params(("arbitrary",)),
        name="mix",
    )(o, pm, x2, mod, w_out)


def _ffn_up_kernel(x_ref, xp_ref, xn_ref, mod_ref, nw_ref, wa_ref, wb_ref, cwa_ref, cwb_ref,
                   cba_ref, cbb_ref, o_ref, h_ref, *, tm, seq):
    i = pl.program_id(0)

    @pl.when(pl.program_id(1) == 0)
    def _():
        mod = mod_ref[0]
        nw = nw_ref[...]
        sc, sh = mod[4:5], mod[3:4]
        starts_seq = (i * tm) % seq == 0
        ends_seq = ((i + 1) * tm) % seq == 0
        above = jnp.where(starts_seq, 0.0, _norm_mod(xp_ref[...], nw, sc, sh))
        below = jnp.where(ends_seq, 0.0, _norm_mod(xn_ref[...], nw, sc, sh))
        h_ref[0:CONV_HALO, :] = above.astype(BF16)
        h_ref[CONV_HALO:CONV_HALO + tm, :] = _norm_mod(x_ref[...], nw, sc, sh).astype(BF16)
        h_ref[CONV_HALO + tm:, :] = below.astype(BF16)

    ext = tm + 2 * CONV_HALO
    h = h_ref[...]

    def conv(w_ref, cw_ref, cb_ref):
        u = jnp.dot(h, w_ref[...], preferred_element_type=F32)
        um = u[CONV_HALO:CONV_HALO + tm]
        up = pltpu.roll(u, 1, 0)[CONV_HALO:CONV_HALO + tm]
        un = pltpu.roll(u, ext - 1, 0)[CONV_HALO:CONV_HALO + tm]
        if tm > seq:
            in_seq = lax.broadcasted_iota(jnp.int32, um.shape, 0) % seq
            up = jnp.where(in_seq == 0, 0.0, up)
            un = jnp.where(in_seq == seq - 1, 0.0, un)
        cw = cw_ref[...]
        return up * cw[0:1] + um * cw[1:2] + un * cw[2:3] + cb_ref[...]

    a = conv(wa_ref, cwa_ref, cba_ref)
    b = conv(wb_ref, cwb_ref, cbb_ref)
    o_ref[...] = (_silu(a) * b).astype(o_ref.dtype)


def _ffn_down_kernel(a_ref, x_ref, mod_ref, w_ref, fnw_ref, o_ref, *, final_norm):
    y = x_ref[...] + mod_ref[0][5:6] * jnp.dot(a_ref[...], w_ref[...],
                                               preferred_element_type=F32)
    if final_norm:
        y = y * lax.rsqrt(jnp.mean(y * y, axis=-1, keepdims=True) + EPS) * fnw_ref[...]
    o_ref[...] = y


def _ffn(x2, mod, rows_per_mod, seq, nw, w_up, conv_w, conv_b, w_down, fnw, final_norm,
         tm=1024, tf=512, tm_down=256):
    t, d = x2.shape
    dff = w_down.shape[0]
    nf = dff // tf
    assert seq % tm == 0 or tm % seq == 0
    hb = tm // CONV_HALO
    last_hb = t // CONV_HALO - 1
    cb = conv_b.reshape(1, 2 * dff)
    mod_spec = lambda tile: pl.BlockSpec((1, N_MOD, d),
                                         lambda i, *_: ((i * tile) // rows_per_mod, 0, 0))
    act = pl.pallas_call(
        functools.partial(_ffn_up_kernel, tm=tm, seq=seq),
        grid=(t // tm, nf),
        in_specs=[pl.BlockSpec((tm, d), lambda i, f: (i, 0)),
                  pl.BlockSpec((CONV_HALO, d), lambda i, f: (jnp.maximum(i * hb - 1, 0), 0)),
                  pl.BlockSpec((CONV_HALO, d), lambda i, f: (jnp.minimum((i + 1) * hb, last_hb), 0)),
                  mod_spec(tm),
                  pl.BlockSpec((1, d), lambda i, f: (0, 0)),
                  pl.BlockSpec((d, tf), lambda i, f: (0, f)),
                  pl.BlockSpec((d, tf), lambda i, f: (0, f + nf)),
                  pl.BlockSpec((3, tf), lambda i, f: (0, f)),
                  pl.BlockSpec((3, tf), lambda i, f: (0, f + nf)),
                  pl.BlockSpec((1, tf), lambda i, f: (0, f)),
                  pl.BlockSpec((1, tf), lambda i, f: (0, f + nf))],
        out_specs=pl.BlockSpec((tm, tf), lambda i, f: (i, f)),
        out_shape=jax.ShapeDtypeStruct((t, dff), BF16),
        scratch_shapes=[pltpu.VMEM((tm + 2 * CONV_HALO, d), BF16)],
        compiler_params=_params(("arbitrary", "arbitrary")),
        name="ffn_up",
    )(x2, x2, x2, mod, nw.reshape(1, d), w_up, w_up, conv_w, conv_w, cb, cb)
    return pl.pallas_call(
        functools.partial(_ffn_down_kernel, final_norm=final_norm),
        grid=(t // tm_down,),
        in_specs=[pl.BlockSpec((tm_down, dff), lambda i: (i, 0)),
                  pl.BlockSpec((tm_down, d), lambda i: (i, 0)),
                  mod_spec(tm_down),
                  pl.BlockSpec((dff, d), lambda i: (0, 0), pipeline_mode=pl.Buffered(1)),
                  pl.BlockSpec((1, d), lambda i: (0, 0))],
        out_specs=pl.BlockSpec((tm_down, d), lambda i: (i, 0)),
        out_shape=jax.ShapeDtypeStruct((t, d), F32),
        compiler_params=_params(("arbitrary",)),
        name="ffn_down",
    )(act, x2, mod, w_down, fnw.reshape(1, d))


def kernel(x_prompt, x_sample, state_hgrn, c, c_ctx, w_ada, b_ada, norm1_w, w_in, lb_param,
           hg_norm_w, w_pool, pool_scale, w_out, norm2_w, w_up, conv_w, conv_b, w_down,
           final_norm_w):
    depth = w_in.shape[0]
    bp, lp, d = x_prompt.shape
    bs, ls, _ = x_sample.shape
    hg_width = lb_param.shape[2]

    lb_all = jnp.cumsum(jax.nn.softmax(lb_param.astype(F32), axis=1), axis=1)
    n_c = 1 + bs
    pad = (-n_c) % 8
    cvecs = jnp.concatenate([c_ctx[None, :], c, jnp.zeros((pad, d), F32)], axis=0)

    xp = x_prompt.reshape(bp * lp, d)
    xs = x_sample.reshape(bs * ls, d)
    ctx_states = []
    for l in range(depth):
        last = l == depth - 1
        mods = _ada(cvecs, w_ada[l], b_ada[l])
        mod_p = mods[0:1].reshape(1, N_MOD, d)
        mod_s = mods[1:n_c].reshape(bs, N_MOD, d)
        w_in_l = w_in[l].astype(BF16)
        w_pool_l = w_pool[l].astype(BF16)
        w_out_l = w_out[l].astype(BF16)
        w_up_l = w_up[l].astype(BF16)
        w_down_l = w_down[l].astype(BF16)
        lb = lb_all[:, l]

        def layer(x2, mod, rows_per_mod, batch, seq, s0, on_grid, want_state):
            proj = _in_proj(x2, mod, rows_per_mod, norm1_w[l], w_in_l)
            o, st = _scan(proj, lb, hg_norm_w[l], s0, l, batch, seq, want_state)
            pm = _pool(proj, w_pool_l, pool_scale[l], 5 * hg_width, batch, seq, on_grid)
            x2 = _mix(o, pm, x2, mod, rows_per_mod, w_out_l)
            x2 = _ffn(x2, mod, rows_per_mod, seq, norm2_w[l], w_up_l, conv_w[l], conv_b[l],
                      w_down_l, final_norm_w, last)
            return x2, st

        xp, st = layer(xp, mod_p, bp * lp, bp, lp, None, False, True)
        ctx_states.append(st)
        xs, _ = layer(xs, mod_s, ls, bs, ls, state_hgrn, True, False)

    y_prompt = xp.reshape(bp, lp, d)
    y_sample = xs.reshape(bs, ls, d)
    new_state = jnp.stack(ctx_states, axis=1)
    return (y_prompt, y_sample, new_state)
```

```python
import functools

import jax
import jax.numpy as jnp
import numpy as np
from jax import lax
from jax.experimental import pallas as pl
from jax.experimental.pallas import tpu as pltpu

F32 = jnp.float32
BF16 = jnp.bfloat16
EPS = 1e-6

HEAD_DIM = 128
N_MOD = 6
POOL_WINDOWS = (2, 4, 8, 16)
GRID_W = 64
POOL_SEQ_PER_STEP = 8
CONV_HALO = 8
VMEM_LIMIT = 56 * 1024 * 1024


def _params(sem):
    return pltpu.CompilerParams(dimension_semantics=sem, vmem_limit_bytes=VMEM_LIMIT)


def _silu(x):
    hx = 0.5 * x
    return hx + hx * jnp.tanh(hx)


def _norm_mod(x, nw, sc, sh):
    scale = nw * (1.0 + sc)
    return x * lax.rsqrt(jnp.mean(x * x, axis=-1, keepdims=True) + EPS) * scale + sh


def _ada_kernel(c_ref, w_ref, b_ref, o_ref):
    s = _silu(c_ref[...]).astype(BF16)
    o_ref[...] = jnp.dot(s, w_ref[...].astype(BF16), preferred_element_type=F32) + b_ref[...]


def _ada(cvecs, w_ada, b_ada, tn=1024):
    r, d = cvecs.shape
    n = w_ada.shape[1]
    return pl.pallas_call(
        _ada_kernel,
        grid=(n // tn,),
        in_specs=[pl.BlockSpec((r, d), lambda j: (0, 0)),
                  pl.BlockSpec((d, tn), lambda j: (0, j)),
                  pl.BlockSpec((1, tn), lambda j: (0, j))],
        out_specs=pl.BlockSpec((r, tn), lambda j: (0, j)),
        out_shape=jax.ShapeDtypeStruct((r, n), F32),
        compiler_params=_params(("arbitrary",)),
        name="ada",
    )(cvecs, w_ada, b_ada.reshape(1, n))


def _in_proj_kernel(x_ref, mod_ref, nw_ref, w_ref, o_ref):
    mod = mod_ref[0]
    h = _norm_mod(x_ref[...], nw_ref[...], mod[1:2], mod[0:1]).astype(BF16)
    o_ref[...] = jnp.dot(h, w_ref[...], preferred_element_type=F32)


def _in_proj(x2, mod, rows_per_mod, nw, w, tm=256):
    t, d = x2.shape
    n = w.shape[1]
    return pl.pallas_call(
        _in_proj_kernel,
        grid=(t // tm,),
        in_specs=[pl.BlockSpec((tm, d), lambda i: (i, 0)),
                  pl.BlockSpec((1, N_MOD, d), lambda i: ((i * tm) // rows_per_mod, 0, 0)),
                  pl.BlockSpec((1, d), lambda i: (0, 0)),
                  pl.BlockSpec((d, n), lambda i: (0, 0), pipeline_mode=pl.Buffered(1))],
        out_specs=pl.BlockSpec((tm, n), lambda i: (i, 0)),
        out_shape=jax.ShapeDtypeStruct((t, n), F32),
        compiler_params=_params(("arbitrary",)),
        name="in_proj",
    )(x2, mod, nw.reshape(1, d), w)


LOG2E = 1.4426950408889634
TOTAL_ROWS = 16
SHORT_SEQ_PER_STEP = 4


def _decay_matrices(c):
    t = np.arange(c)
    blocks = []
    m = 2
    while m < c:
        mid = (t // (2 * m)) * (2 * m) + m
        second = (t & m) != 0
        qm = np.zeros((c, 2 * c), np.float32)
        km = np.zeros((c, 2 * c), np.float32)
        for i in range(c):
            if second[i]:
                qm[i, mid[i]:i + 1] = 1
                km[i, c + mid[i]:c + i] = 1
            else:
                qm[i, c + i:c + mid[i]] = 1
                km[i, i + 1:mid[i]] = 1
        blocks += [qm, km]
        m *= 2
    qf = np.zeros((c, 2 * c), np.float32)
    qb = np.zeros((c, 2 * c), np.float32)
    kf = np.zeros((c, 2 * c), np.float32)
    kb = np.zeros((c, c), np.float32)
    for i in range(c):
        qf[i, 0:i + 1] = 1
        qb[i, c + i:2 * c] = 1
        kf[i, i + 1:c] = 1
        kb[i, 0:i] = 1
    tf = np.zeros((TOTAL_ROWS, 2 * c), np.float32)
    tf[:, 0:c] = 1
    tb = np.ones((TOTAL_ROWS, c), np.float32)
    sweep = np.concatenate(blocks + [qf, qb, kf, tf], axis=0)
    back = np.concatenate([kb, tb], axis=0)
    return (jnp.asarray(np.concatenate([sweep, sweep], axis=1), BF16),
            jnp.asarray(np.concatenate([back, back], axis=1), BF16))


def _gates(z, lbd):
    f = lbd + (1.0 - lbd) * jax.nn.sigmoid(z)
    return jnp.log(f) * LOG2E, 1.0 - f


def _split_bf16(x):
    hi = x.astype(BF16)
    return hi, (x - hi.astype(F32)).astype(BF16)


def _dot_nt(a, b):
    return lax.dot_general(a, b, (((1,), (1,)), ((), ())), preferred_element_type=F32)


def _dot_tn(a, b):
    return lax.dot_general(a, b, (((0,), (0,)), ((), ())), preferred_element_type=F32)


def _scan_kernel(*refs, seq, spb, c, grp, grp_b, has_s0, want_state):
    q_ref, zf_ref, zb_ref, v_ref, g_ref, lb_ref, nw_ref, ms_ref, mb_ref = refs[:9]
    pos = 9
    s0_ref = None
    if has_s0:
        s0_ref = refs[pos]
        pos += 1
    o_ref = refs[pos]
    pos += 1
    sout_ref = None
    if want_state:
        sout_ref = refs[pos]
        pos += 1
    lfb_scr, kb_scr, sb_scr, e_scr, qs_scr, kf_scr, gate_scr = refs[pos:pos + 7]

    gc = grp * c
    n_iters = seq // gc
    n_levels = c.bit_length() - 1
    base = 2 * (n_levels - 1) * c
    lb_f = lb_ref[0:1, :]
    lb_b = lb_ref[1:2, :]
    nw = nw_ref[...]
    row = lax.broadcasted_iota(jnp.int32, (c, HEAD_DIM), 0)
    pair = (lax.broadcasted_iota(jnp.int32, (c, c), 0)
            ^ lax.broadcasted_iota(jnp.int32, (c, c), 1))
    level = jnp.full((c, c), -1, jnp.int32)
    for k in range(n_levels):
        level = level + (pair >= (1 << k)).astype(jnp.int32)

    def initial_state(sq, direction):
        if has_s0:
            return s0_ref[sq, direction].T
        return jnp.zeros((HEAD_DIM, HEAD_DIM), F32)

    def chunk_rows(x, gi):
        return x[gi * c:(gi + 1) * c]

    def chunk_lanes(x, gi):
        return x[:, gi * HEAD_DIM:(gi + 1) * HEAD_DIM]

    def stack_chunks(parts, n):
        return jnp.concatenate(
            [jnp.concatenate([chunk_rows(p, gi) for p in parts], axis=0) for gi in range(n)],
            axis=1)

    gcb = grp_b * c

    def sweep_b(i, stb, sq):
        it = (sq + 1) * (seq // gcb) - 1 - i
        rows = pl.ds(pl.multiple_of(it * gcb, gcb), gcb)
        l2f_b, k_b = _gates(zb_ref[rows, :], lb_b)
        lfb_scr[rows, :] = l2f_b
        kb_scr[rows, :] = k_b
        v_bf = v_ref[rows, :].astype(BF16)
        e = jnp.exp2(jnp.dot(mb_ref[...], stack_chunks(_split_bf16(l2f_b), grp_b),
                             preferred_element_type=F32))
        kv = []
        for gi in range(grp_b):
            k_hat = (chunk_rows(k_b, gi) * chunk_lanes(e, gi)[0:c]).astype(BF16)
            kv.append(_dot_tn(chunk_rows(v_bf, gi), k_hat))
        for gi in reversed(range(grp_b)):
            sb_scr[it * grp_b + gi] = stb
            stb = stb * chunk_lanes(e, gi)[c:c + 1] + kv[gi]
        return stb

    stb_final = [lax.fori_loop(0, seq // gcb, functools.partial(sweep_b, sq=sq),
                               initial_state(sq, 1), unroll=True) for sq in range(spb)]

    def block_rows(blk):
        return pl.ds(pl.multiple_of(blk * gc, gc), gc)

    def stage_a_gates(blk):
        rows = block_rows(blk)
        qs = _silu(q_ref[rows, :])
        gate = _silu(g_ref[rows, :])
        l2f_f, k_f = _gates(zf_ref[rows, :], lb_f)
        fh, fl = _split_bf16(l2f_f)
        bh, bl = _split_bf16(lfb_scr[rows, :])
        return qs, gate, k_f, stack_chunks((fh, bh, fl, bl), grp)

    def stage_a_finish(slot, vals):
        qs, gate, k_f, rhs = vals
        e_scr[slot] = jnp.exp2(jnp.dot(ms_ref[...], rhs, preferred_element_type=F32))
        qs_scr[slot] = qs
        gate_scr[slot] = gate
        kf_scr[slot] = k_f

    def stage_b_levels(blk, slot):
        rows = block_rows(blk)
        vbf_all = v_ref[rows, :].astype(BF16)
        kb_all = kb_scr[rows, :]
        scores, kv, dec = [], [], []
        for gi in range(grp):
            lanes = slice(gi * HEAD_DIM, (gi + 1) * HEAD_DIM)
            crow = slice(gi * c, (gi + 1) * c)
            qs = qs_scr[slot, crow, :]
            k_f = kf_scr[slot, crow, :]
            k_b = chunk_rows(kb_all, gi)
            odd = (row & 1) != 0
            x = (qs * (1.0 - jnp.where(odd, k_f, k_b))).astype(BF16)
            y = jnp.where(odd, k_b, k_f).astype(BF16)
            a = jnp.where(level == 0, _dot_nt(x, y), 0.0)
            for lv in range(1, n_levels):
                second = (row & (1 << lv)) != 0
                r0 = 2 * (lv - 1) * c
                x = (qs * e_scr[slot, r0:r0 + c, lanes]).astype(BF16)
                y = (jnp.where(second, k_b, k_f)
                     * e_scr[slot, r0 + c:r0 + 2 * c, lanes]).astype(BF16)
                a = jnp.where(level == lv, _dot_nt(x, y), a)
            scores.append(a.astype(BF16))
            k_hat = (k_f * e_scr[slot, base + 2 * c:base + 3 * c, lanes]).astype(BF16)
            kv.append(_dot_tn(chunk_rows(vbf_all, gi), k_hat))
            dec.append(e_scr[slot, base + 3 * c:base + 3 * c + 1, lanes])
        return scores, kv, dec

    def stage_b_finish(blk, slot, stf, lv_out):
        scores, kv, dec = lv_out
        rows = block_rows(blk)
        v_all = v_ref[rows, :]
        vbf_all = v_all.astype(BF16)
        kb_all = kb_scr[rows, :]
        states = []
        for gi in range(grp):
            states.append(stf)
            stf = stf * dec[gi] + kv[gi]
        outs = []
        for gi in range(grp):
            lanes = slice(gi * HEAD_DIM, (gi + 1) * HEAD_DIM)
            crow = slice(gi * c, (gi + 1) * c)
            qs = qs_scr[slot, crow, :]
            q_fb = jnp.concatenate([qs * e_scr[slot, base:base + c, lanes],
                                    qs * e_scr[slot, base + c:base + 2 * c, lanes]],
                                   axis=1).astype(BF16)
            st_fb = jnp.concatenate([states[gi], sb_scr[blk * grp + gi]], axis=1).astype(BF16)
            o = jnp.dot(scores[gi], chunk_rows(vbf_all, gi), preferred_element_type=F32)
            o = o + jnp.sum(qs * (kf_scr[slot, crow, :] + chunk_rows(kb_all, gi)),
                            axis=-1, keepdims=True) * chunk_rows(v_all, gi)
            o = o + _dot_nt(q_fb, st_fb)
            o = o * lax.rsqrt(jnp.mean(o * o, axis=-1, keepdims=True) + EPS) * nw
            outs.append(o * gate_scr[slot, crow, :])
        o_ref[rows, :] = jnp.concatenate(outs, axis=0).astype(o_ref.dtype)
        return stf

    def step(cur, cur_slot, nxt, stf):
        ahead = stage_a_gates(nxt)
        lv_out = stage_b_levels(cur, cur_slot)
        stage_a_finish(1 - cur_slot, ahead)
        return stage_b_finish(cur, cur_slot, stf, lv_out)

    if n_iters == 1:
        ahead = [stage_a_gates(sq) for sq in range(spb)]
        for sq in range(spb):
            stage_a_finish(sq, ahead[sq])
        lv_out = [stage_b_levels(sq, sq) for sq in range(spb)]
        stf_final = [stage_b_finish(sq, sq, initial_state(sq, 0), lv_out[sq])
                     for sq in range(spb)]
    else:
        assert n_iters % 2 == 0 and spb == 1
        stage_a_finish(0, stage_a_gates(0))

        def sweep_f(j, stf):
            stf = step(2 * j, 0, 2 * j + 1, stf)
            return step(2 * j + 1, 1, jnp.minimum(2 * j + 2, n_iters - 1), stf)

        stf_final = [lax.fori_loop(0, n_iters // 2, sweep_f, initial_state(0, 0))]

    if want_state:
        for sq in range(spb):
            sout_ref[sq, 0] = stf_final[sq].T
            sout_ref[sq, 1] = stb_final[sq].T


def _scan(proj, lb, hg_nw, s0, layer, batch, seq, want_state, c=64, grp=8, grp_b=8):
    t = proj.shape[0]
    n_heads = lb.shape[1] // HEAD_DIM
    has_s0 = s0 is not None
    grp = min(grp, seq // c)
    grp_b = min(grp_b, seq // c)
    spb = min(SHORT_SEQ_PER_STEP, batch) if seq == grp * c else 1
    assert batch % spb == 0
    rows = spb * seq
    m_sweep, m_back = _decay_matrices(c)

    def sec(k):
        return pl.BlockSpec((rows, HEAD_DIM), lambda b, h, k=k: (b, k * n_heads + h))

    state_block = (spb, None, 2, None, HEAD_DIM, HEAD_DIM)
    state_map = lambda b, h: (b, layer, 0, h, 0, 0)
    in_specs = [sec(0), sec(1), sec(2), sec(3), sec(4),
                pl.BlockSpec((2, HEAD_DIM), lambda b, h: (0, h)),
                pl.BlockSpec((1, HEAD_DIM), lambda b, h: (0, 0)),
                pl.BlockSpec(m_sweep.shape, lambda b, h: (0, 0)),
                pl.BlockSpec(m_back.shape, lambda b, h: (0, 0))]
    args = [proj, proj, proj, proj, proj, lb, hg_nw.reshape(1, HEAD_DIM), m_sweep, m_back]
    if has_s0:
        in_specs.append(pl.BlockSpec(state_block, state_map))
        args.append(s0)
    out_specs = [pl.BlockSpec((rows, HEAD_DIM), lambda b, h: (b, h))]
    out_shape = [jax.ShapeDtypeStruct((t, n_heads * HEAD_DIM), BF16)]
    if want_state:
        out_specs.append(pl.BlockSpec((spb, 2, None, HEAD_DIM, HEAD_DIM),
                                      lambda b, h: (b, 0, h, 0, 0)))
        out_shape.append(jax.ShapeDtypeStruct((batch, 2, n_heads, HEAD_DIM, HEAD_DIM), F32))
    gc = grp * c
    slots = max(2, spb)
    res = pl.pallas_call(
        functools.partial(_scan_kernel, seq=seq, spb=spb, c=c, grp=grp, grp_b=grp_b,
                          has_s0=has_s0, want_state=want_state),
        grid=(batch // spb, n_heads),
        in_specs=in_specs,
        out_specs=out_specs,
        out_shape=out_shape,
        scratch_shapes=[pltpu.VMEM((rows, HEAD_DIM), F32),
                        pltpu.VMEM((rows, HEAD_DIM), F32),
                        pltpu.VMEM((rows // c, HEAD_DIM, HEAD_DIM), F32),
                        pltpu.VMEM((slots, m_sweep.shape[0], grp * HEAD_DIM), F32),
                        pltpu.VMEM((slots, gc, HEAD_DIM), F32),
                        pltpu.VMEM((slots, gc, HEAD_DIM), F32),
                        pltpu.VMEM((slots, gc, HEAD_DIM), F32)],
        compiler_params=_params(("arbitrary", "arbitrary")),
        name="scan",
    )(*args)
    return (res[0], res[1]) if want_state else (res[0], None)


def _tile_rows(small, nb):
    n, w = small.shape
    return jnp.broadcast_to(small[None], (nb, n, w)).reshape(nb * n, w)


def _window_counts(pos, w, n):
    half = w // 2
    return (jnp.minimum(pos + (w - half), n) - jnp.maximum(pos - half, 0)).astype(F32)


def _window_sum_rows(x, w, n, nb):
    rows, width = x.shape
    half = w // 2
    pos = lax.broadcasted_iota(jnp.int32, (n, width), 0)

    def keep(cond):
        return _tile_rows(cond.astype(F32), nb)

    fwd = x
    ln = 1
    while ln < half:
        fwd = fwd + pltpu.roll(fwd, rows - ln, 0) * keep(pos < n - ln)
        ln *= 2
    bwd = pltpu.roll(x, 1, 0) * keep(pos >= 1)
    ln = 1
    while ln < half:
        bwd = bwd + pltpu.roll(bwd, ln, 0) * keep(pos >= ln)
        ln *= 2
    return fwd + bwd


def _window_sum_bands(x3, w):
    half = w // 2

    def later(a, k):
        return jnp.concatenate([a[k:], jnp.zeros((k,) + a.shape[1:], a.dtype)], axis=0)

    def earlier(a, k):
        return jnp.concatenate([jnp.zeros((k,) + a.shape[1:], a.dtype), a[:-k]], axis=0)

    fwd = x3
    ln = 1
    while ln < half:
        fwd = fwd + later(fwd, ln)
        ln *= 2
    bwd = earlier(x3, 1)
    ln = 1
    while ln < half:
        bwd = bwd + earlier(bwd, ln)
        ln *= 2
    return fwd + bwd


def _pool_kernel(p_ref, w_ref, sc_ref, o_ref, *, seq, on_grid):
    grp = pl.program_id(1)
    rows, width = p_ref.shape
    for gi, w in enumerate(POOL_WINDOWS):
        @pl.when(grp == gi)
        def _(w=w):
            pg = p_ref[...]
            if on_grid:
                nb = rows // GRID_W
                pos_c = lax.broadcasted_iota(jnp.int32, (GRID_W, width), 0)
                pos_r = lax.broadcasted_iota(jnp.int32, (nb, 1, width), 0)
                m = _window_sum_rows(pg, w, GRID_W, nb)
                m = m * _tile_rows(1.0 / _window_counts(pos_c, w, GRID_W), nb)
                m3 = _window_sum_bands(m.reshape(nb, GRID_W, width), w)
                m = (m3 * (1.0 / _window_counts(pos_r, w, nb))).reshape(rows, width)
            else:
                nb = rows // seq
                pos = lax.broadcasted_iota(jnp.int32, (seq, width), 0)
                m = _window_sum_rows(pg, w, seq, nb)
                m = m * _tile_rows(1.0 / _window_counts(pos, w, seq), nb)
            d = (m - pg).astype(BF16)
            y = jnp.dot(d, w_ref[...], preferred_element_type=F32) * sc_ref[...]
            o_ref[...] = y.astype(o_ref.dtype)


def _pool(proj, w_pool_l, pool_scale_l, col0, batch, seq, on_grid):
    t = proj.shape[0]
    n_grp, gw, _ = w_pool_l.shape
    blk0 = col0 // gw
    spb = 1 if on_grid else min(POOL_SEQ_PER_STEP, batch)
    assert batch % spb == 0
    rows = spb * seq
    return pl.pallas_call(
        functools.partial(_pool_kernel, seq=seq, on_grid=on_grid),
        grid=(batch // spb, n_grp),
        in_specs=[pl.BlockSpec((rows, gw), lambda b, g: (b, blk0 + g)),
                  pl.BlockSpec((None, gw, gw), lambda b, g: (g, 0, 0)),
                  pl.BlockSpec((1, gw), lambda b, g: (0, g))],
        out_specs=pl.BlockSpec((rows, gw), lambda b, g: (b, g)),
        out_shape=jax.ShapeDtypeStruct((t, n_grp * gw), BF16),
        compiler_params=_params(("arbitrary", "arbitrary")),
        name="pool",
    )(proj, w_pool_l, pool_scale_l.reshape(1, n_grp * gw))


def _mix_kernel(o_ref, pm_ref, x_ref, mod_ref, w_ref, out_ref):
    k = o_ref.shape[1]
    mix = (jnp.dot(o_ref[...], w_ref[0:k, :], preferred_element_type=F32)
           + jnp.dot(pm_ref[...], w_ref[k:, :], preferred_element_type=F32))
    out_ref[...] = x_ref[...] + mod_ref[0][2:3] * mix


def _mix(o, pm, x2, mod, rows_per_mod, w_out, tm=512):
    t, d = x2.shape
    return pl.pallas_call(
        _mix_kernel,
        grid=(t // tm,),
        in_specs=[pl.BlockSpec((tm, o.shape[1]), lambda i: (i, 0)),
                  pl.BlockSpec((tm, pm.shape[1]), lambda i: (i, 0)),
                  pl.BlockSpec((tm, d), lambda i: (i, 0)),
                  pl.BlockSpec((1, N_MOD, d), lambda i: ((i * tm) // rows_per_mod, 0, 0)),
                  pl.BlockSpec(w_out.shape, lambda i: (0, 0))],
        out_specs=pl.BlockSpec((tm, d), lambda i: (i, 0)),
        out_shape=jax.ShapeDtypeStruct((t, d), F32),
        compiler_params=_params(("arbitrary",)),
        name="mix",
    )(o, pm, x2, mod, w_out)


def _ffn_up_kernel(x_ref, xp_ref, xn_ref, mod_ref, nw_ref, wa_ref, wb_ref, cwa_ref, cwb_ref,
                   cba_ref, cbb_ref, o_ref, h_ref, *, tm, seq):
    i = pl.program_id(0)

    @pl.when(pl.program_id(1) == 0)
    def _():
        mod = mod_ref[0]
        nw = nw_ref[...]
        sc, sh = mod[4:5], mod[3:4]
        starts_seq = (i * tm) % seq == 0
        ends_seq = ((i + 1) * tm) % seq == 0
        above = jnp.where(starts_seq, 0.0, _norm_mod(xp_ref[...], nw, sc, sh))
        below = jnp.where(ends_seq, 0.0, _norm_mod(xn_ref[...], nw, sc, sh))
        h_ref[0:CONV_HALO, :] = above.astype(BF16)
        h_ref[CONV_HALO:CONV_HALO + tm, :] = _norm_mod(x_ref[...], nw, sc, sh).astype(BF16)
        h_ref[CONV_HALO + tm:, :] = below.astype(BF16)

    ext = tm + 2 * CONV_HALO
    h = h_ref[...]

    def conv(w_ref, cw_ref, cb_ref):
        u = jnp.dot(h, w_ref[...], preferred_element_type=F32)
        um = u[CONV_HALO:CONV_HALO + tm]
        up = pltpu.roll(u, 1, 0)[CONV_HALO:CONV_HALO + tm]
        un = pltpu.roll(u, ext - 1, 0)[CONV_HALO:CONV_HALO + tm]
        if tm > seq:
            in_seq = lax.broadcasted_iota(jnp.int32, um.shape, 0) % seq
            up = jnp.where(in_seq == 0, 0.0, up)
            un = jnp.where(in_seq == seq - 1, 0.0, un)
        cw = cw_ref[...]
        return up * cw[0:1] + um * cw[1:2] + un * cw[2:3] + cb_ref[...]

    a = conv(wa_ref, cwa_ref, cba_ref)
    b = conv(wb_ref, cwb_ref, cbb_ref)
    o_ref[...] = (_silu(a) * b).astype(o_ref.dtype)


def _ffn_down_kernel(a_ref, x_ref, mod_ref, w_ref, fnw_ref, o_ref, *, final_norm):
    y = x_ref[...] + mod_ref[0][5:6] * jnp.dot(a_ref[...], w_ref[...],
                                               preferred_element_type=F32)
    if final_norm:
        y = y * lax.rsqrt(jnp.mean(y * y, axis=-1, keepdims=True) + EPS) * fnw_ref[...]
    o_ref[...] = y


def _ffn(x2, mod, rows_per_mod, seq, nw, w_up, conv_w, conv_b, w_down, fnw, final_norm,
         tm=1024, tf=512, tm_down=256):
    t, d = x2.shape
    dff = w_down.shape[0]
    nf = dff // tf
    assert seq % tm == 0 or tm % seq == 0
    hb = tm // CONV_HALO
    last_hb = t // CONV_HALO - 1
    cb = conv_b.reshape(1, 2 * dff)
    mod_spec = lambda tile: pl.BlockSpec((1, N_MOD, d),
                                         lambda i, *_: ((i * tile) // rows_per_mod, 0, 0))
    act = pl.pallas_call(
        functools.partial(_ffn_up_kernel, tm=tm, seq=seq),
        grid=(t // tm, nf),
        in_specs=[pl.BlockSpec((tm, d), lambda i, f: (i, 0)),
                  pl.BlockSpec((CONV_HALO, d), lambda i, f: (jnp.maximum(i * hb - 1, 0), 0)),
                  pl.BlockSpec((CONV_HALO, d), lambda i, f: (jnp.minimum((i + 1) * hb, last_hb), 0)),
                  mod_spec(tm),
                  pl.BlockSpec((1, d), lambda i, f: (0, 0)),
                  pl.BlockSpec((d, tf), lambda i, f: (0, f)),
                  pl.BlockSpec((d, tf), lambda i, f: (0, f + nf)),
                  pl.BlockSpec((3, tf), lambda i, f: (0, f)),
                  pl.BlockSpec((3, tf), lambda i, f: (0, f + nf)),
                  pl.BlockSpec((1, tf), lambda i, f: (0, f)),
                  pl.BlockSpec((1, tf), lambda i, f: (0, f + nf))],
        out_specs=pl.BlockSpec((tm, tf), lambda i, f: (i, f)),
        out_shape=jax.ShapeDtypeStruct((t, dff), BF16),
        scratch_shapes=[pltpu.VMEM((tm + 2 * CONV_HALO, d), BF16)],
        compiler_params=_params(("arbitrary", "arbitrary")),
        name="ffn_up",
    )(x2, x2, x2, mod, nw.reshape(1, d), w_up, w_up, conv_w, conv_w, cb, cb)
    return pl.pallas_call(
        functools.partial(_ffn_down_kernel, final_norm=final_norm),
        grid=(t // tm_down,),
        in_specs=[pl.BlockSpec((tm_down, dff), lambda i: (i, 0)),
                  pl.BlockSpec((tm_down, d), lambda i: (i, 0)),
                  mod_spec(tm_down),
                  pl.BlockSpec((dff, d), lambda i: (0, 0), pipeline_mode=pl.Buffered(1)),
                  pl.BlockSpec((1, d), lambda i: (0, 0))],
        out_specs=pl.BlockSpec((tm_down, d), lambda i: (i, 0)),
        out_shape=jax.ShapeDtypeStruct((t, d), F32),
        compiler_params=_params(("arbitrary",)),
        name="ffn_down",
    )(act, x2, mod, w_down, fnw.reshape(1, d))


def kernel(x_prompt, x_sample, state_hgrn, c, c_ctx, w_ada, b_ada, norm1_w, w_in, lb_param,
           hg_norm_w, w_pool, pool_scale, w_out, norm2_w, w_up, conv_w, conv_b, w_down,
           final_norm_w):
    depth = w_in.shape[0]
    bp, lp, d = x_prompt.shape
    bs, ls, _ = x_sample.shape
    hg_width = lb_param.shape[2]

    lb_all = jnp.cumsum(jax.nn.softmax(lb_param.astype(F32), axis=1), axis=1)
    n_c = 1 + bs
    pad = (-n_c) % 8
    cvecs = jnp.concatenate([c_ctx[None, :], c, jnp.zeros((pad, d), F32)], axis=0)

    xp = x_prompt.reshape(bp * lp, d)
    xs = x_sample.reshape(bs * ls, d)
    ctx_states = []
    for l in range(depth):
        last = l == depth - 1
        mods = _ada(cvecs, w_ada[l], b_ada[l])
        mod_p = mods[0:1].reshape(1, N_MOD, d)
        mod_s = mods[1:n_c].reshape(bs, N_MOD, d)
        w_in_l = w_in[l].astype(BF16)
        w_pool_l = w_pool[l].astype(BF16)
        w_out_l = w_out[l].astype(BF16)
        w_up_l = w_up[l].astype(BF16)
        w_down_l = w_down[l].astype(BF16)
        lb = lb_all[:, l]

        def layer(x2, mod, rows_per_mod, batch, seq, s0, on_grid, want_state):
            proj = _in_proj(x2, mod, rows_per_mod, norm1_w[l], w_in_l)
            o, st = _scan(proj, lb, hg_norm_w[l], s0, l, batch, seq, want_state)
            pm = _pool(proj, w_pool_l, pool_scale[l], 5 * hg_width, batch, seq, on_grid)
            x2 = _mix(o, pm, x2, mod, rows_per_mod, w_out_l)
            x2 = _ffn(x2, mod, rows_per_mod, seq, norm2_w[l], w_up_l, conv_w[l], conv_b[l],
                      w_down_l, final_norm_w, last)
            return x2, st

        xp, st = layer(xp, mod_p, bp * lp, bp, lp, None, False, True)
        ctx_states.append(st)
        xs, _ = layer(xs, mod_s, ls, bs, ls, state_hgrn, True, False)

    y_prompt = xp.reshape(bp, lp, d)
    y_sample = xs.reshape(bs, ls, d)
    new_state = jnp.stack(ctx_states, axis=1)
    return (y_prompt, y_sample, new_state)
```

```python
import functools

import jax
import jax.numpy as jnp
import numpy as np
from jax import lax
from jax.experimental import pallas as pl
from jax.experimental.pallas import tpu as pltpu

F32 = jnp.float32
BF16 = jnp.bfloat16
EPS = 1e-6

HEAD_DIM = 128
N_MOD = 6
POOL_WINDOWS = (2, 4, 8, 16)
GRID_W = 64
POOL_SEQ_PER_STEP = 8
CONV_HALO = 8
V7X_VMEM_BYTES = 64 * 1024 * 1024
VMEM_LIMIT = V7X_VMEM_BYTES - 8 * 1024 * 1024


def _params(sem):
    return pltpu.CompilerParams(dimension_semantics=sem, vmem_limit_bytes=VMEM_LIMIT)


def _silu(x):
    hx = 0.5 * x
    return hx + hx * jnp.tanh(hx)


def _norm_mod(x, nw, sc, sh):
    scale = nw * (1.0 + sc)
    return x * lax.rsqrt(jnp.mean(x * x, axis=-1, keepdims=True) + EPS) * scale + sh


def _ada_kernel(c_ref, w_ref, b_ref, o_ref):
    s = _silu(c_ref[...]).astype(BF16)
    o_ref[...] = jnp.dot(s, w_ref[...].astype(BF16), preferred_element_type=F32) + b_ref[...]


def _ada(cvecs, w_ada, b_ada, tn=1024):
    r, d = cvecs.shape
    n = w_ada.shape[1]
    return pl.pallas_call(
        _ada_kernel,
        grid=(n // tn,),
        in_specs=[pl.BlockSpec((r, d), lambda j: (0, 0)),
                  pl.BlockSpec((d, tn), lambda j: (0, j)),
                  pl.BlockSpec((1, tn), lambda j: (0, j))],
        out_specs=pl.BlockSpec((r, tn), lambda j: (0, j)),
        out_shape=jax.ShapeDtypeStruct((r, n), F32),
        compiler_params=_params(("arbitrary",)),
        name="ada",
    )(cvecs, w_ada, b_ada.reshape(1, n))


def _in_proj_kernel(x_ref, mod_ref, nw_ref, w_ref, o_ref):
    mod = mod_ref[0]
    h = _norm_mod(x_ref[...], nw_ref[...], mod[1:2], mod[0:1]).astype(BF16)
    o_ref[...] = jnp.dot(h, w_ref[...], preferred_element_type=F32)


def _in_proj(x2, mod, rows_per_mod, nw, w, tm=256):
    t, d = x2.shape
    n = w.shape[1]
    return pl.pallas_call(
        _in_proj_kernel,
        grid=(t // tm,),
        in_specs=[pl.BlockSpec((tm, d), lambda i: (i, 0)),
                  pl.BlockSpec((1, N_MOD, d), lambda i: ((i * tm) // rows_per_mod, 0, 0)),
                  pl.BlockSpec((1, d), lambda i: (0, 0)),
                  pl.BlockSpec((d, n), lambda i: (0, 0), pipeline_mode=pl.Buffered(1))],
        out_specs=pl.BlockSpec((tm, n), lambda i: (i, 0)),
        out_shape=jax.ShapeDtypeStruct((t, n), F32),
        compiler_params=_params(("arbitrary",)),
        name="in_proj",
    )(x2, mod, nw.reshape(1, d), w)


LOG2E = 1.4426950408889634
TOTAL_ROWS = 16
SHORT_SEQ_PER_STEP = 4


def _decay_matrices(c):
    t = np.arange(c)
    blocks = []
    m = 2
    while m < c:
        mid = (t // (2 * m)) * (2 * m) + m
        second = (t & m) != 0
        qm = np.zeros((c, 2 * c), np.float32)
        km = np.zeros((c, 2 * c), np.float32)
        for i in range(c):
            if second[i]:
                qm[i, mid[i]:i + 1] = 1
                km[i, c + mid[i]:c + i] = 1
            else:
                qm[i, c + i:c + mid[i]] = 1
                km[i, i + 1:mid[i]] = 1
        blocks += [qm, km]
        m *= 2
    qf = np.zeros((c, 2 * c), np.float32)
    qb = np.zeros((c, 2 * c), np.float32)
    kf = np.zeros((c, 2 * c), np.float32)
    kb = np.zeros((c, c), np.float32)
    for i in range(c):
        qf[i, 0:i + 1] = 1
        qb[i, c + i:2 * c] = 1
        kf[i, i + 1:c] = 1
        kb[i, 0:i] = 1
    tf = np.zeros((TOTAL_ROWS, 2 * c), np.float32)
    tf[:, 0:c] = 1
    tb = np.ones((TOTAL_ROWS, c), np.float32)
    sweep = np.concatenate(blocks + [qf, qb, kf, tf], axis=0)
    back = np.concatenate([kb, tb], axis=0)
    return (jnp.asarray(np.concatenate([sweep, sweep], axis=1), BF16),
            jnp.asarray(np.concatenate([back, back], axis=1), BF16))


def _gates(z, lbd):
    f = lbd + (1.0 - lbd) * jax.nn.sigmoid(z)
    return jnp.log(f) * LOG2E, 1.0 - f


def _split_bf16(x):
    hi = x.astype(BF16)
    return hi, (x - hi.astype(F32)).astype(BF16)


def _dot_nt(a, b):
    return lax.dot_general(a, b, (((1,), (1,)), ((), ())), preferred_element_type=F32)


def _dot_tn(a, b):
    return lax.dot_general(a, b, (((0,), (0,)), ((), ())), preferred_element_type=F32)


def _scan_kernel(*refs, seq, spb, c, grp, grp_b, has_s0, want_state):
    q_ref, zf_ref, zb_ref, v_ref, g_ref, lb_ref, nw_ref, ms_ref, mb_ref = refs[:9]
    pos = 9
    s0_ref = None
    if has_s0:
        s0_ref = refs[pos]
        pos += 1
    o_ref = refs[pos]
    pos += 1
    sout_ref = None
    if want_state:
        sout_ref = refs[pos]
        pos += 1
    lfbh_scr, lfbl_scr, kb_scr, sb_scr, e_scr, qs_scr, kf_scr, gate_scr = refs[pos:pos + 8]

    gc = grp * c
    n_iters = seq // gc
    n_levels = c.bit_length() - 1
    base = 2 * (n_levels - 1) * c
    lb_f = lb_ref[0:1, :]
    lb_b = lb_ref[1:2, :]
    nw = nw_ref[...]
    row = lax.broadcasted_iota(jnp.int32, (c, HEAD_DIM), 0)
    pair = (lax.broadcasted_iota(jnp.int32, (c, c), 0)
            ^ lax.broadcasted_iota(jnp.int32, (c, c), 1))
    level = jnp.full((c, c), -1, jnp.int32)
    for k in range(n_levels):
        level = level + (pair >= (1 << k)).astype(jnp.int32)

    def initial_state(sq, direction):
        if has_s0:
            return s0_ref[sq, direction].T
        return jnp.zeros((HEAD_DIM, HEAD_DIM), F32)

    def chunk_rows(x, gi):
        return x[gi * c:(gi + 1) * c]

    def chunk_lanes(x, gi):
        return x[:, gi * HEAD_DIM:(gi + 1) * HEAD_DIM]

    def stack_chunks(parts, n):
        return jnp.concatenate(
            [jnp.concatenate([chunk_rows(p, gi) for p in parts], axis=0) for gi in range(n)],
            axis=1)

    gcb = grp_b * c

    def sweep_b(i, stb, sq):
        it = (sq + 1) * (seq // gcb) - 1 - i
        rows = pl.ds(pl.multiple_of(it * gcb, gcb), gcb)
        l2f_b, k_b = _gates(zb_ref[rows, :], lb_b)
        bh, bl = _split_bf16(l2f_b)
        lfbh_scr[rows, :] = bh
        lfbl_scr[rows, :] = bl
        kb_scr[rows, :] = k_b
        v_bf = v_ref[rows, :].astype(BF16)
        e = jnp.exp2(jnp.dot(mb_ref[...], stack_chunks((bh, bl), grp_b),
                             preferred_element_type=F32))
        kv = []
        for gi in range(grp_b):
            k_hat = (chunk_rows(k_b, gi) * chunk_lanes(e, gi)[0:c]).astype(BF16)
            kv.append(_dot_tn(chunk_rows(v_bf, gi), k_hat))
        for gi in reversed(range(grp_b)):
            sb_scr[it * grp_b + gi] = stb.astype(BF16)
            stb = stb * chunk_lanes(e, gi)[c:c + 1] + kv[gi]
        return stb

    stb_final = [lax.fori_loop(0, seq // gcb, functools.partial(sweep_b, sq=sq),
                               initial_state(sq, 1), unroll=True) for sq in range(spb)]

    def block_rows(blk):
        return pl.ds(pl.multiple_of(blk * gc, gc), gc)

    def stage_a_gates(blk):
        rows = block_rows(blk)
        qs = _silu(q_ref[rows, :])
        gate = _silu(g_ref[rows, :])
        l2f_f, k_f = _gates(zf_ref[rows, :], lb_f)
        fh, fl = _split_bf16(l2f_f)
        bh, bl = lfbh_scr[rows, :], lfbl_scr[rows, :]
        return qs, gate, k_f, stack_chunks((fh, bh, fl, bl), grp)

    def stage_a_finish(slot, vals):
        qs, gate, k_f, rhs = vals
        e_scr[slot] = jnp.exp2(jnp.dot(ms_ref[...], rhs, preferred_element_type=F32))
        qs_scr[slot] = qs
        gate_scr[slot] = gate
        kf_scr[slot] = k_f

    def stage_b_levels(blk, slot):
        rows = block_rows(blk)
        vbf_all = v_ref[rows, :].astype(BF16)
        kb_all = kb_scr[rows, :]
        scores, kv, dec = [], [], []
        for gi in range(grp):
            lanes = slice(gi * HEAD_DIM, (gi + 1) * HEAD_DIM)
            crow = slice(gi * c, (gi + 1) * c)
            qs = qs_scr[slot, crow, :]
            k_f = kf_scr[slot, crow, :]
            k_b = chunk_rows(kb_all, gi)
            odd = (row & 1) != 0
            x = (qs * (1.0 - jnp.where(odd, k_f, k_b))).astype(BF16)
            y = jnp.where(odd, k_b, k_f).astype(BF16)
            a = jnp.where(level == 0, _dot_nt(x, y), 0.0)
            for lv in range(1, n_levels):
                second = (row & (1 << lv)) != 0
                r0 = 2 * (lv - 1) * c
                x = (qs * e_scr[slot, r0:r0 + c, lanes]).astype(BF16)
                y = (jnp.where(second, k_b, k_f)
                     * e_scr[slot, r0 + c:r0 + 2 * c, lanes]).astype(BF16)
                a = jnp.where(level == lv, _dot_nt(x, y), a)
            scores.append(a.astype(BF16))
            k_hat = (k_f * e_scr[slot, base + 2 * c:base + 3 * c, lanes]).astype(BF16)
            kv.append(_dot_tn(chunk_rows(vbf_all, gi), k_hat))
            dec.append(e_scr[slot, base + 3 * c:base + 3 * c + 1, lanes])
        return scores, kv, dec

    def stage_b_finish(blk, slot, stf, lv_out):
        scores, kv, dec = lv_out
        rows = block_rows(blk)
        v_all = v_ref[rows, :]
        vbf_all = v_all.astype(BF16)
        kb_all = kb_scr[rows, :]
        states = []
        for gi in range(grp):
            states.append(stf)
            stf = stf * dec[gi] + kv[gi]
        outs = []
        for gi in range(grp):
            lanes = slice(gi * HEAD_DIM, (gi + 1) * HEAD_DIM)
            crow = slice(gi * c, (gi + 1) * c)
            qs = qs_scr[slot, crow, :]
            q_fb = jnp.concatenate([qs * e_scr[slot, base:base + c, lanes],
                                    qs * e_scr[slot, base + c:base + 2 * c, lanes]],
                                   axis=1).astype(BF16)
            st_fb = jnp.concatenate([states[gi].astype(BF16), sb_scr[blk * grp + gi]], axis=1)
            o = jnp.dot(scores[gi], chunk_rows(vbf_all, gi), preferred_element_type=F32)
            o = o + jnp.sum(qs * (kf_scr[slot, crow, :] + chunk_rows(kb_all, gi)),
                            axis=-1, keepdims=True) * chunk_rows(v_all, gi)
            o = o + _dot_nt(q_fb, st_fb)
            o = o * lax.rsqrt(jnp.mean(o * o, axis=-1, keepdims=True) + EPS) * nw
            outs.append(o * gate_scr[slot, crow, :])
        o_ref[rows, :] = jnp.concatenate(outs, axis=0).astype(o_ref.dtype)
        return stf

    def step(cur, cur_slot, nxt, stf):
        ahead = stage_a_gates(nxt)
        lv_out = stage_b_levels(cur, cur_slot)
        stage_a_finish(1 - cur_slot, ahead)
        return stage_b_finish(cur, cur_slot, stf, lv_out)

    if n_iters == 1:
        ahead = [stage_a_gates(sq) for sq in range(spb)]
        for sq in range(spb):
            stage_a_finish(sq, ahead[sq])
        lv_out = [stage_b_levels(sq, sq) for sq in range(spb)]
        stf_final = [stage_b_finish(sq, sq, initial_state(sq, 0), lv_out[sq])
                     for sq in range(spb)]
    else:
        assert n_iters % 2 == 0 and spb == 1
        stage_a_finish(0, stage_a_gates(0))

        def sweep_f(j, stf):
            stf = step(2 * j, 0, 2 * j + 1, stf)
            return step(2 * j + 1, 1, jnp.minimum(2 * j + 2, n_iters - 1), stf)

        stf_final = [lax.fori_loop(0, n_iters // 2, sweep_f, initial_state(0, 0))]

    if want_state:
        for sq in range(spb):
            sout_ref[sq, 0] = stf_final[sq].T
            sout_ref[sq, 1] = stb_final[sq].T


def _scan(proj, lb, hg_nw, s0, layer, batch, seq, want_state, c=64, grp=8, grp_b=8):
    t = proj.shape[0]
    n_heads = lb.shape[1] // HEAD_DIM
    has_s0 = s0 is not None
    grp = min(grp, seq // c)
    grp_b = min(grp_b, seq // c)
    spb = min(SHORT_SEQ_PER_STEP, batch) if seq == grp * c else 1
    assert batch % spb == 0
    rows = spb * seq
    m_sweep, m_back = _decay_matrices(c)

    def sec(k):
        return pl.BlockSpec((rows, HEAD_DIM), lambda b, h, k=k: (b, k * n_heads + h))

    state_block = (spb, None, 2, None, HEAD_DIM, HEAD_DIM)
    state_map = lambda b, h: (b, layer, 0, h, 0, 0)
    in_specs = [sec(0), sec(1), sec(2), sec(3), sec(4),
                pl.BlockSpec((2, HEAD_DIM), lambda b, h: (0, h)),
                pl.BlockSpec((1, HEAD_DIM), lambda b, h: (0, 0)),
                pl.BlockSpec(m_sweep.shape, lambda b, h: (0, 0)),
                pl.BlockSpec(m_back.shape, lambda b, h: (0, 0))]
    args = [proj, proj, proj, proj, proj, lb, hg_nw.reshape(1, HEAD_DIM), m_sweep, m_back]
    if has_s0:
        in_specs.append(pl.BlockSpec(state_block, state_map))
        args.append(s0)
    out_specs = [pl.BlockSpec((rows, HEAD_DIM), lambda b, h: (b, h))]
    out_shape = [jax.ShapeDtypeStruct((t, n_heads * HEAD_DIM), BF16)]
    if want_state:
        out_specs.append(pl.BlockSpec((spb, 2, None, HEAD_DIM, HEAD_DIM),
                                      lambda b, h: (b, 0, h, 0, 0)))
        out_shape.append(jax.ShapeDtypeStruct((batch, 2, n_heads, HEAD_DIM, HEAD_DIM), F32))
    gc = grp * c
    slots = max(2, spb)
    res = pl.pallas_call(
        functools.partial(_scan_kernel, seq=seq, spb=spb, c=c, grp=grp, grp_b=grp_b,
                          has_s0=has_s0, want_state=want_state),
        grid=(batch // spb, n_heads),
        in_specs=in_specs,
        out_specs=out_specs,
        out_shape=out_shape,
        scratch_shapes=[pltpu.VMEM((rows, HEAD_DIM), BF16),
                        pltpu.VMEM((rows, HEAD_DIM), BF16),
                        pltpu.VMEM((rows, HEAD_DIM), F32),
                        pltpu.VMEM((rows // c, HEAD_DIM, HEAD_DIM), BF16),
                        pltpu.VMEM((slots, m_sweep.shape[0], grp * HEAD_DIM), F32),
                        pltpu.VMEM((slots, gc, HEAD_DIM), F32),
                        pltpu.VMEM((slots, gc, HEAD_DIM), F32),
                        pltpu.VMEM((slots, gc, HEAD_DIM), F32)],
        compiler_params=_params(("arbitrary", "arbitrary")),
        name="scan",
    )(*args)
    return (res[0], res[1]) if want_state else (res[0], None)


def _tile_rows(small, nb):
    n, w = small.shape
    return jnp.broadcast_to(small[None], (nb, n, w)).reshape(nb * n, w)


def _window_counts(pos, w, n):
    half = w // 2
    return (jnp.minimum(pos + (w - half), n) - jnp.maximum(pos - half, 0)).astype(F32)


def _window_sum_rows(x, w, n, nb):
    rows, width = x.shape
    half = w // 2
    pos = lax.broadcasted_iota(jnp.int32, (n, width), 0)

    def keep(cond):
        return _tile_rows(cond.astype(F32), nb)

    fwd = x
    ln = 1
    while ln < half:
        fwd = fwd + pltpu.roll(fwd, rows - ln, 0) * keep(pos < n - ln)
        ln *= 2
    bwd = pltpu.roll(x, 1, 0) * keep(pos >= 1)
    ln = 1
    while ln < half:
        bwd = bwd + pltpu.roll(bwd, ln, 0) * keep(pos >= ln)
        ln *= 2
    return fwd + bwd


def _window_sum_bands(x3, w):
    half = w // 2

    def later(a, k):
        return jnp.concatenate([a[k:], jnp.zeros((k,) + a.shape[1:], a.dtype)], axis=0)

    def earlier(a, k):
        return jnp.concatenate([jnp.zeros((k,) + a.shape[1:], a.dtype), a[:-k]], axis=0)

    fwd = x3
    ln = 1
    while ln < half:
        fwd = fwd + later(fwd, ln)
        ln *= 2
    bwd = earlier(x3, 1)
    ln = 1
    while ln < half:
        bwd = bwd + earlier(bwd, ln)
        ln *= 2
    return fwd + bwd


def _pool_kernel(p_ref, w_ref, sc_ref, o_ref, *, seq, on_grid):
    grp = pl.program_id(1)
    rows, width = p_ref.shape
    for gi, w in enumerate(POOL_WINDOWS):
        @pl.when(grp == gi)
        def _(w=w):
            pg = p_ref[...]
            if on_grid:
                nb = rows // GRID_W
                pos_c = lax.broadcasted_iota(jnp.int32, (GRID_W, width), 0)
                pos_r = lax.broadcasted_iota(jnp.int32, (nb, 1, width), 0)
                m = _window_sum_rows(pg, w, GRID_W, nb)
                m = m * _tile_rows(1.0 / _window_counts(pos_c, w, GRID_W), nb)
                m3 = _window_sum_bands(m.reshape(nb, GRID_W, width), w)
                m = (m3 * (1.0 / _window_counts(pos_r, w, nb))).reshape(rows, width)
            else:
                nb = rows // seq
                pos = lax.broadcasted_iota(jnp.int32, (seq, width), 0)
                m = _window_sum_rows(pg, w, seq, nb)
                m = m * _tile_rows(1.0 / _window_counts(pos, w, seq), nb)
            d = (m - pg).astype(BF16)
            y = jnp.dot(d, w_ref[...], preferred_element_type=F32) * sc_ref[...]
            o_ref[...] = y.astype(o_ref.dtype)


def _pool(proj, w_pool_l, pool_scale_l, col0, batch, seq, on_grid):
    t = proj.shape[0]
    n_grp, gw, _ = w_pool_l.shape
    blk0 = col0 // gw
    spb = 1 if on_grid else min(POOL_SEQ_PER_STEP, batch)
    assert batch % spb == 0
    rows = spb * seq
    return pl.pallas_call(
        functools.partial(_pool_kernel, seq=seq, on_grid=on_grid),
        grid=(batch // spb, n_grp),
        in_specs=[pl.BlockSpec((rows, gw), lambda b, g: (b, blk0 + g)),
                  pl.BlockSpec((None, gw, gw), lambda b, g: (g, 0, 0)),
                  pl.BlockSpec((1, gw), lambda b, g: (0, g))],
        out_specs=pl.BlockSpec((rows, gw), lambda b, g: (b, g)),
        out_shape=jax.ShapeDtypeStruct((t, n_grp * gw), BF16),
        compiler_params=_params(("arbitrary", "arbitrary")),
        name="pool",
    )(proj, w_pool_l, pool_scale_l.reshape(1, n_grp * gw))


def _mix_kernel(o_ref, pm_ref, x_ref, mod_ref, w_ref, out_ref):
    k = o_ref.shape[1]
    mix = (jnp.dot(o_ref[...], w_ref[0:k, :], preferred_element_type=F32)
           + jnp.dot(pm_ref[...], w_ref[k:, :], preferred_element_type=F32))
    out_ref[...] = x_ref[...] + mod_ref[0][2:3] * mix


def _mix(o, pm, x2, mod, rows_per_mod, w_out, tm=512):
    t, d = x2.shape
    return pl.pallas_call(
        _mix_kernel,
        grid=(t // tm,),
        in_specs=[pl.BlockSpec((tm, o.shape[1]), lambda i: (i, 0)),
                  pl.BlockSpec((tm, pm.shape[1]), lambda i: (i, 0)),
                  pl.BlockSpec((tm, d), lambda i: (i, 0)),
                  pl.BlockSpec((1, N_MOD, d), lambda i: ((i * tm) // rows_per_mod, 0, 0)),
                  pl.BlockSpec(w_out.shape, lambda i: (0, 0))],
        out_specs=pl.BlockSpec((tm, d), lambda i: (i, 0)),
        out_shape=jax.ShapeDtypeStruct((t, d), F32),
        compiler_params=_params(("arbitrary",)),
        name="mix",
    )(o, pm, x2, mod, w_out)


def _ffn_up_kernel(x_ref, xp_ref, xn_ref, mod_ref, nw_ref, wa_ref, wb_ref, cwa_ref, cwb_ref,
                   cba_ref, cbb_ref, o_ref, h_ref, *, tm, seq):
    i = pl.program_id(0)

    @pl.when(pl.program_id(1) == 0)
    def _():
        mod = mod_ref[0]
        nw = nw_ref[...]
        sc, sh = mod[4:5], mod[3:4]
        starts_seq = (i * tm) % seq == 0
        ends_seq = ((i + 1) * tm) % seq == 0
        above = jnp.where(starts_seq, 0.0, _norm_mod(xp_ref[...], nw, sc, sh))
        below = jnp.where(ends_seq, 0.0, _norm_mod(xn_ref[...], nw, sc, sh))
        h_ref[0:CONV_HALO, :] = above.astype(BF16)
        h_ref[CONV_HALO:CONV_HALO + tm, :] = _norm_mod(x_ref[...], nw, sc, sh).astype(BF16)
        h_ref[CONV_HALO + tm:, :] = below.astype(BF16)

    ext = tm + 2 * CONV_HALO
    h = h_ref[...]

    def conv(w_ref, cw_ref, cb_ref):
        u = jnp.dot(h, w_ref[...], preferred_element_type=F32)
        um = u[CONV_HALO:CONV_HALO + tm]
        up = pltpu.roll(u, 1, 0)[CONV_HALO:CONV_HALO + tm]
        un = pltpu.roll(u, ext - 1, 0)[CONV_HALO:CONV_HALO + tm]
        if tm > seq:
            in_seq = lax.broadcasted_iota(jnp.int32, um.shape, 0) % seq
            up = jnp.where(in_seq == 0, 0.0, up)
            un = jnp.where(in_seq == seq - 1, 0.0, un)
        cw = cw_ref[...]
        return up * cw[0:1] + um * cw[1:2] + un * cw[2:3] + cb_ref[...]

    a = conv(wa_ref, cwa_ref, cba_ref)
    b = conv(wb_ref, cwb_ref, cbb_ref)
    o_ref[...] = (_silu(a) * b).astype(o_ref.dtype)


def _ffn_down_kernel(a_ref, x_ref, mod_ref, w_ref, fnw_ref, o_ref, *, final_norm):
    y = x_ref[...] + mod_ref[0][5:6] * jnp.dot(a_ref[...], w_ref[...],
                                               preferred_element_type=F32)
    if final_norm:
        y = y * lax.rsqrt(jnp.mean(y * y, axis=-1, keepdims=True) + EPS) * fnw_ref[...]
    o_ref[...] = y


def _ffn(x2, mod, rows_per_mod, seq, nw, w_up, conv_w, conv_b, w_down, fnw, final_norm,
         tm=1024, tf=512, tm_down=256):
    t, d = x2.shape
    dff = w_down.shape[0]
    nf = dff // tf
    assert seq % tm == 0 or tm % seq == 0
    hb = tm // CONV_HALO
    last_hb = t // CONV_HALO - 1
    cb = conv_b.reshape(1, 2 * dff)
    mod_spec = lambda tile: pl.BlockSpec((1, N_MOD, d),
                                         lambda i, *_: ((i * tile) // rows_per_mod, 0, 0))
    act = pl.pallas_call(
        functools.partial(_ffn_up_kernel, tm=tm, seq=seq),
        grid=(t // tm, nf),
        in_specs=[pl.BlockSpec((tm, d), lambda i, f: (i, 0)),
                  pl.BlockSpec((CONV_HALO, d), lambda i, f: (jnp.maximum(i * hb - 1, 0), 0)),
                  pl.BlockSpec((CONV_HALO, d), lambda i, f: (jnp.minimum((i + 1) * hb, last_hb), 0)),
                  mod_spec(tm),
                  pl.BlockSpec((1, d), lambda i, f: (0, 0)),
                  pl.BlockSpec((d, tf), lambda i, f: (0, f)),
                  pl.BlockSpec((d, tf), lambda i, f: (0, f + nf)),
                  pl.BlockSpec((3, tf), lambda i, f: (0, f)),
                  pl.BlockSpec((3, tf), lambda i, f: (0, f + nf)),
                  pl.BlockSpec((1, tf), lambda i, f: (0, f)),
                  pl.BlockSpec((1, tf), lambda i, f: (0, f + nf))],
        out_specs=pl.BlockSpec((tm, tf), lambda i, f: (i, f)),
        out_shape=jax.ShapeDtypeStruct((t, dff), BF16),
        scratch_shapes=[pltpu.VMEM((tm + 2 * CONV_HALO, d), BF16)],
        compiler_params=_params(("arbitrary", "arbitrary")),
        name="ffn_up",
    )(x2, x2, x2, mod, nw.reshape(1, d), w_up, w_up, conv_w, conv_w, cb, cb)
    return pl.pallas_call(
        functools.partial(_ffn_down_kernel, final_norm=final_norm),
        grid=(t // tm_down,),
        in_specs=[pl.BlockSpec((tm_down, dff), lambda i: (i, 0)),
                  pl.BlockSpec((tm_down, d), lambda i: (i, 0)),
                  mod_spec(tm_down),
                  pl.BlockSpec((dff, d), lambda i: (0, 0), pipeline_mode=pl.Buffered(1)),
                  pl.BlockSpec((1, d), lambda i: (0, 0))],
        out_specs=pl.BlockSpec((tm_down, d), lambda i: (i, 0)),
        out_shape=jax.ShapeDtypeStruct((t, d), F32),
        compiler_params=_params(("arbitrary",)),
        name="ffn_down",
    )(act, x2, mod, w_down, fnw.reshape(1, d))


def kernel(x_prompt, x_sample, state_hgrn, c, c_ctx, w_ada, b_ada, norm1_w, w_in, lb_param,
           hg_norm_w, w_pool, pool_scale, w_out, norm2_w, w_up, conv_w, conv_b, w_down,
           final_norm_w):
    depth = w_in.shape[0]
    bp, lp, d = x_prompt.shape
    bs, ls, _ = x_sample.shape
    hg_width = lb_param.shape[2]

    lb_all = jnp.cumsum(jax.nn.softmax(lb_param.astype(F32), axis=1), axis=1)
    n_c = 1 + bs
    pad = (-n_c) % 8
    cvecs = jnp.concatenate([c_ctx[None, :], c, jnp.zeros((pad, d), F32)], axis=0)

    xp = x_prompt.reshape(bp * lp, d)
    xs = x_sample.reshape(bs * ls, d)
    ctx_states = []
    for l in range(depth):
        last = l == depth - 1
        mods = _ada(cvecs, w_ada[l], b_ada[l])
        mod_p = mods[0:1].reshape(1, N_MOD, d)
        mod_s = mods[1:n_c].reshape(bs, N_MOD, d)
        w_in_l = w_in[l].astype(BF16)
        w_pool_l = w_pool[l].astype(BF16)
        w_out_l = w_out[l].astype(BF16)
        w_up_l = w_up[l].astype(BF16)
        w_down_l = w_down[l].astype(BF16)
        lb = lb_all[:, l]

        def layer(x2, mod, rows_per_mod, batch, seq, s0, on_grid, want_state):
            proj = _in_proj(x2, mod, rows_per_mod, norm1_w[l], w_in_l)
            o, st = _scan(proj, lb, hg_norm_w[l], s0, l, batch, seq, want_state)
            pm = _pool(proj, w_pool_l, pool_scale[l], 5 * hg_width, batch, seq, on_grid)
            x2 = _mix(o, pm, x2, mod, rows_per_mod, w_out_l)
            x2 = _ffn(x2, mod, rows_per_mod, seq, norm2_w[l], w_up_l, conv_w[l], conv_b[l],
                      w_down_l, final_norm_w, last)
            return x2, st

        xp, st = layer(xp, mod_p, bp * lp, bp, lp, None, False, True)
        ctx_states.append(st)
        xs, _ = layer(xs, mod_s, ls, bs, ls, state_hgrn, True, False)

    y_prompt = xp.reshape(bp, lp, d)
    y_sample = xs.reshape(bs, ls, d)
    new_state = jnp.stack(ctx_states, axis=1)
    return (y_prompt, y_sample, new_state)
```

```python
import functools

import jax
import jax.numpy as jnp
import numpy as np
from jax import lax
from jax.experimental import pallas as pl
from jax.experimental.pallas import tpu as pltpu

F32 = jnp.float32
BF16 = jnp.bfloat16
EPS = 1e-6

HEAD_DIM = 128
N_MOD = 6
POOL_WINDOWS = (2, 4, 8, 16)
GRID_W = 64
POOL_SEQ_PER_STEP = 8
CONV_HALO = 8
V7X_VMEM_BYTES = 64 * 1024 * 1024
VMEM_LIMIT = V7X_VMEM_BYTES - 8 * 1024 * 1024


def _params(sem):
    return pltpu.CompilerParams(dimension_semantics=sem, vmem_limit_bytes=VMEM_LIMIT)


def _silu(x):
    hx = 0.5 * x
    return hx + hx * jnp.tanh(hx)


def _norm_mod(x, nw, sc, sh):
    scale = nw * (1.0 + sc)
    return x * lax.rsqrt(jnp.mean(x * x, axis=-1, keepdims=True) + EPS) * scale + sh


def _ada_kernel(c_ref, w_ref, b_ref, o_ref):
    s = _silu(c_ref[...]).astype(BF16)
    o_ref[...] = jnp.dot(s, w_ref[...].astype(BF16), preferred_element_type=F32) + b_ref[...]


def _ada(cvecs, w_ada, b_ada, tn=1024):
    r, d = cvecs.shape
    n = w_ada.shape[1]
    return pl.pallas_call(
        _ada_kernel,
        grid=(n // tn,),
        in_specs=[pl.BlockSpec((r, d), lambda j: (0, 0)),
                  pl.BlockSpec((d, tn), lambda j: (0, j)),
                  pl.BlockSpec((1, tn), lambda j: (0, j))],
        out_specs=pl.BlockSpec((r, tn), lambda j: (0, j)),
        out_shape=jax.ShapeDtypeStruct((r, n), F32),
        compiler_params=_params(("arbitrary",)),
        name="ada",
    )(cvecs, w_ada, b_ada.reshape(1, n))


def _in_proj_kernel(x_ref, mod_ref, nw_ref, w_ref, o_ref):
    mod = mod_ref[0]
    h = _norm_mod(x_ref[...], nw_ref[...], mod[1:2], mod[0:1]).astype(BF16)
    o_ref[...] = jnp.dot(h, w_ref[...], preferred_element_type=F32)


def _in_proj(x2, mod, rows_per_mod, nw, w, tm=256):
    t, d = x2.shape
    n = w.shape[1]
    return pl.pallas_call(
        _in_proj_kernel,
        grid=(t // tm,),
        in_specs=[pl.BlockSpec((tm, d), lambda i: (i, 0)),
                  pl.BlockSpec((1, N_MOD, d), lambda i: ((i * tm) // rows_per_mod, 0, 0)),
                  pl.BlockSpec((1, d), lambda i: (0, 0)),
                  pl.BlockSpec((d, n), lambda i: (0, 0), pipeline_mode=pl.Buffered(1))],
        out_specs=pl.BlockSpec((tm, n), lambda i: (i, 0)),
        out_shape=jax.ShapeDtypeStruct((t, n), F32),
        compiler_params=_params(("arbitrary",)),
        name="in_proj",
    )(x2, mod, nw.reshape(1, d), w)


LOG2E = 1.4426950408889634
TOTAL_ROWS = 16
SHORT_SEQ_PER_STEP = 4
SCAN_INTERLEAVE = 2


def _decay_matrices(c):
    t = np.arange(c)
    blocks = []
    m = 2
    while m < c:
        mid = (t // (2 * m)) * (2 * m) + m
        second = (t & m) != 0
        qm = np.zeros((c, 2 * c), np.float32)
        km = np.zeros((c, 2 * c), np.float32)
        for i in range(c):
            if second[i]:
                qm[i, mid[i]:i + 1] = 1
                km[i, c + mid[i]:c + i] = 1
            else:
                qm[i, c + i:c + mid[i]] = 1
                km[i, i + 1:mid[i]] = 1
        blocks += [qm, km]
        m *= 2
    qf = np.zeros((c, 2 * c), np.float32)
    qb = np.zeros((c, 2 * c), np.float32)
    kf = np.zeros((c, 2 * c), np.float32)
    kb = np.zeros((c, c), np.float32)
    for i in range(c):
        qf[i, 0:i + 1] = 1
        qb[i, c + i:2 * c] = 1
        kf[i, i + 1:c] = 1
        kb[i, 0:i] = 1
    tf = np.zeros((TOTAL_ROWS, 2 * c), np.float32)
    tf[:, 0:c] = 1
    tb = np.ones((TOTAL_ROWS, c), np.float32)
    sweep = np.concatenate(blocks + [qf, qb, kf, tf], axis=0)
    back = np.concatenate([kb, tb], axis=0)
    return (jnp.asarray(np.concatenate([sweep, sweep], axis=1), BF16),
            jnp.asarray(np.concatenate([back, back], axis=1), BF16))


def _gates(z, lbd):
    f = lbd + (1.0 - lbd) * jax.nn.sigmoid(z)
    return jnp.log(f) * LOG2E, 1.0 - f


def _split_bf16(x):
    hi = x.astype(BF16)
    return hi, (x - hi.astype(F32)).astype(BF16)


def _dot_nt(a, b):
    return lax.dot_general(a, b, (((1,), (1,)), ((), ())), preferred_element_type=F32)


def _dot_tn(a, b):
    return lax.dot_general(a, b, (((0,), (0,)), ((), ())), preferred_element_type=F32)


def _scan_kernel(*refs, seq, spb, c, grp, grp_b, has_s0, want_state):
    q_ref, zf_ref, zb_ref, v_ref, g_ref, lb_ref, nw_ref, ms_ref, mb_ref = refs[:9]
    pos = 9
    s0_ref = None
    if has_s0:
        s0_ref = refs[pos]
        pos += 1
    o_ref = refs[pos]
    pos += 1
    sout_ref = None
    if want_state:
        sout_ref = refs[pos]
        pos += 1
    lfbh_scr, lfbl_scr, kb_scr, sb_scr, e_scr, qs_scr, kf_scr, gate_scr = refs[pos:pos + 8]

    gc = grp * c
    n_iters = seq // gc
    n_levels = c.bit_length() - 1
    base = 2 * (n_levels - 1) * c
    lb_f = lb_ref[0:1, :]
    lb_b = lb_ref[1:2, :]
    nw = nw_ref[...]
    row = lax.broadcasted_iota(jnp.int32, (c, HEAD_DIM), 0)
    pair = (lax.broadcasted_iota(jnp.int32, (c, c), 0)
            ^ lax.broadcasted_iota(jnp.int32, (c, c), 1))
    level = jnp.full((c, c), -1, jnp.int32)
    for k in range(n_levels):
        level = level + (pair >= (1 << k)).astype(jnp.int32)

    def initial_state(sq, direction):
        if has_s0:
            return s0_ref[sq, direction].T
        return jnp.zeros((HEAD_DIM, HEAD_DIM), F32)

    def chunk_rows(x, gi):
        return x[gi * c:(gi + 1) * c]

    def chunk_lanes(x, gi):
        return x[:, gi * HEAD_DIM:(gi + 1) * HEAD_DIM]

    def stack_chunks(parts, n):
        return jnp.concatenate(
            [jnp.concatenate([chunk_rows(p, gi) for p in parts], axis=0) for gi in range(n)],
            axis=1)

    gcb = grp_b * c

    def sweep_b(i, stb, sq):
        it = (sq + 1) * (seq // gcb) - 1 - i
        rows = pl.ds(pl.multiple_of(it * gcb, gcb), gcb)
        l2f_b, k_b = _gates(zb_ref[rows, :], lb_b)
        bh, bl = _split_bf16(l2f_b)
        lfbh_scr[rows, :] = bh
        lfbl_scr[rows, :] = bl
        kb_scr[rows, :] = k_b
        v_bf = v_ref[rows, :].astype(BF16)
        e = jnp.exp2(jnp.dot(mb_ref[...], stack_chunks((bh, bl), grp_b),
                             preferred_element_type=F32))
        kv = []
        for gi in range(grp_b):
            k_hat = (chunk_rows(k_b, gi) * chunk_lanes(e, gi)[0:c]).astype(BF16)
            kv.append(_dot_tn(chunk_rows(v_bf, gi), k_hat))
        for gi in reversed(range(grp_b)):
            sb_scr[it * grp_b + gi] = stb.astype(BF16)
            stb = stb * chunk_lanes(e, gi)[c:c + 1] + kv[gi]
        return stb

    stb_final = [lax.fori_loop(0, seq // gcb, functools.partial(sweep_b, sq=sq),
                               initial_state(sq, 1), unroll=True) for sq in range(spb)]

    def block_rows(blk):
        return pl.ds(pl.multiple_of(blk * gc, gc), gc)

    def stage_a_gates(blk):
        rows = block_rows(blk)
        qs = _silu(q_ref[rows, :])
        gate = _silu(g_ref[rows, :])
        l2f_f, k_f = _gates(zf_ref[rows, :], lb_f)
        fh, fl = _split_bf16(l2f_f)
        bh, bl = lfbh_scr[rows, :], lfbl_scr[rows, :]
        return qs, gate, k_f, stack_chunks((fh, bh, fl, bl), grp)

    def stage_a_exponents(slot, vals, chunks):
        lanes = slice(chunks[0] * HEAD_DIM, (chunks[-1] + 1) * HEAD_DIM)
        e_scr[slot, :, lanes] = jnp.exp2(jnp.dot(ms_ref[...], vals[3][:, lanes],
                                                 preferred_element_type=F32))

    def stage_a_store(slot, vals):
        qs_scr[slot], gate_scr[slot], kf_scr[slot] = vals[0], vals[1], vals[2]

    def stage_b_levels(blk, slot, chunks):
        rows = block_rows(blk)
        vbf_all = v_ref[rows, :].astype(BF16)
        kb_all = kb_scr[rows, :]
        scores, kv, dec = [], [], []
        for gi in chunks:
            lanes = slice(gi * HEAD_DIM, (gi + 1) * HEAD_DIM)
            crow = slice(gi * c, (gi + 1) * c)
            qs = qs_scr[slot, crow, :]
            k_f = kf_scr[slot, crow, :]
            k_b = chunk_rows(kb_all, gi)
            odd = (row & 1) != 0
            x = (qs * (1.0 - jnp.where(odd, k_f, k_b))).astype(BF16)
            y = jnp.where(odd, k_b, k_f).astype(BF16)
            a = jnp.where(level == 0, _dot_nt(x, y), 0.0)
            for lv in range(1, n_levels):
                second = (row & (1 << lv)) != 0
                r0 = 2 * (lv - 1) * c
                x = (qs * e_scr[slot, r0:r0 + c, lanes]).astype(BF16)
                y = (jnp.where(second, k_b, k_f)
                     * e_scr[slot, r0 + c:r0 + 2 * c, lanes]).astype(BF16)
                a = jnp.where(level == lv, _dot_nt(x, y), a)
            scores.append(a.astype(BF16))
            k_hat = (k_f * e_scr[slot, base + 2 * c:base + 3 * c, lanes]).astype(BF16)
            kv.append(_dot_tn(chunk_rows(vbf_all, gi), k_hat))
            dec.append(e_scr[slot, base + 3 * c:base + 3 * c + 1, lanes])
        return scores, kv, dec

    def stage_b_finish(blk, slot, stf, lv_out):
        scores, kv, dec = lv_out
        rows = block_rows(blk)
        v_all = v_ref[rows, :]
        vbf_all = v_all.astype(BF16)
        kb_all = kb_scr[rows, :]
        states = []
        for gi in range(grp):
            states.append(stf)
            stf = stf * dec[gi] + kv[gi]
        outs = []
        for gi in range(grp):
            lanes = slice(gi * HEAD_DIM, (gi + 1) * HEAD_DIM)
            crow = slice(gi * c, (gi + 1) * c)
            qs = qs_scr[slot, crow, :]
            q_fb = jnp.concatenate([qs * e_scr[slot, base:base + c, lanes],
                                    qs * e_scr[slot, base + c:base + 2 * c, lanes]],
                                   axis=1).astype(BF16)
            st_fb = jnp.concatenate([states[gi].astype(BF16), sb_scr[blk * grp + gi]], axis=1)
            o = jnp.dot(scores[gi], chunk_rows(vbf_all, gi), preferred_element_type=F32)
            o = o + jnp.sum(qs * (kf_scr[slot, crow, :] + chunk_rows(kb_all, gi)),
                            axis=-1, keepdims=True) * chunk_rows(v_all, gi)
            o = o + _dot_nt(q_fb, st_fb)
            o = o * lax.rsqrt(jnp.mean(o * o, axis=-1, keepdims=True) + EPS) * nw
            outs.append(o * gate_scr[slot, crow, :])
        o_ref[rows, :] = jnp.concatenate(outs, axis=0).astype(o_ref.dtype)
        return stf

    def step(cur, cur_slot, nxt, stf):
        ahead = stage_a_gates(nxt)
        stage_a_store(1 - cur_slot, ahead)
        lv_out = [[], [], []]
        for part in range(0, grp, SCAN_INTERLEAVE):
            chunks = range(part, min(part + SCAN_INTERLEAVE, grp))
            stage_a_exponents(1 - cur_slot, ahead, chunks)
            for acc, new in zip(lv_out, stage_b_levels(cur, cur_slot, chunks)):
                acc.extend(new)
        return stage_b_finish(cur, cur_slot, stf, lv_out)

    if n_iters == 1:
        ahead = [stage_a_gates(sq) for sq in range(spb)]
        for sq in range(spb):
            stage_a_store(sq, ahead[sq])
        stage_a_exponents(0, ahead[0], range(grp))
        lv_out = []
        for sq in range(spb):
            if sq + 1 < spb:
                stage_a_exponents(sq + 1, ahead[sq + 1], range(grp))
            lv_out.append(stage_b_levels(sq, sq, range(grp)))
        stf_final = [stage_b_finish(sq, sq, initial_state(sq, 0), lv_out[sq])
                     for sq in range(spb)]
    else:
        assert n_iters % 2 == 0 and spb == 1
        first = stage_a_gates(0)
        stage_a_store(0, first)
        stage_a_exponents(0, first, range(grp))

        def sweep_f(j, stf):
            stf = step(2 * j, 0, 2 * j + 1, stf)
            return step(2 * j + 1, 1, jnp.minimum(2 * j + 2, n_iters - 1), stf)

        stf_final = [lax.fori_loop(0, n_iters // 2, sweep_f, initial_state(0, 0))]

    if want_state:
        for sq in range(spb):
            sout_ref[sq, 0] = stf_final[sq].T
            sout_ref[sq, 1] = stb_final[sq].T


def _scan(proj, lb, hg_nw, s0, layer, batch, seq, want_state, c=64, grp=8, grp_b=8):
    t = proj.shape[0]
    n_heads = lb.shape[1] // HEAD_DIM
    has_s0 = s0 is not None
    grp = min(grp, seq // c)
    grp_b = min(grp_b, seq // c)
    spb = min(SHORT_SEQ_PER_STEP, batch) if seq == grp * c else 1
    assert batch % spb == 0
    rows = spb * seq
    m_sweep, m_back = _decay_matrices(c)

    def sec(k):
        return pl.BlockSpec((rows, HEAD_DIM), lambda b, h, k=k: (b, k * n_heads + h))

    state_block = (spb, None, 2, None, HEAD_DIM, HEAD_DIM)
    state_map = lambda b, h: (b, layer, 0, h, 0, 0)
    in_specs = [sec(0), sec(1), sec(2), sec(3), sec(4),
                pl.BlockSpec((2, HEAD_DIM), lambda b, h: (0, h)),
                pl.BlockSpec((1, HEAD_DIM), lambda b, h: (0, 0)),
                pl.BlockSpec(m_sweep.shape, lambda b, h: (0, 0)),
                pl.BlockSpec(m_back.shape, lambda b, h: (0, 0))]
    args = [proj, proj, proj, proj, proj, lb, hg_nw.reshape(1, HEAD_DIM), m_sweep, m_back]
    if has_s0:
        in_specs.append(pl.BlockSpec(state_block, state_map))
        args.append(s0)
    out_specs = [pl.BlockSpec((rows, HEAD_DIM), lambda b, h: (b, h))]
    out_shape = [jax.ShapeDtypeStruct((t, n_heads * HEAD_DIM), BF16)]
    if want_state:
        out_specs.append(pl.BlockSpec((spb, 2, None, HEAD_DIM, HEAD_DIM),
                                      lambda b, h: (b, 0, h, 0, 0)))
        out_shape.append(jax.ShapeDtypeStruct((batch, 2, n_heads, HEAD_DIM, HEAD_DIM), F32))
    gc = grp * c
    slots = max(2, spb)
    res = pl.pallas_call(
        functools.partial(_scan_kernel, seq=seq, spb=spb, c=c, grp=grp, grp_b=grp_b,
                          has_s0=has_s0, want_state=want_state),
        grid=(batch // spb, n_heads),
        in_specs=in_specs,
        out_specs=out_specs,
        out_shape=out_shape,
        scratch_shapes=[pltpu.VMEM((rows, HEAD_DIM), BF16),
                        pltpu.VMEM((rows, HEAD_DIM), BF16),
                        pltpu.VMEM((rows, HEAD_DIM), F32),
                        pltpu.VMEM((rows // c, HEAD_DIM, HEAD_DIM), BF16),
                        pltpu.VMEM((slots, m_sweep.shape[0], grp * HEAD_DIM), F32),
                        pltpu.VMEM((slots, gc, HEAD_DIM), F32),
                        pltpu.VMEM((slots, gc, HEAD_DIM), F32),
                        pltpu.VMEM((slots, gc, HEAD_DIM), F32)],
        compiler_params=_params(("arbitrary", "arbitrary")),
        name="scan",
    )(*args)
    return (res[0], res[1]) if want_state else (res[0], None)


def _tile_rows(small, nb):
    n, w = small.shape
    return jnp.broadcast_to(small[None], (nb, n, w)).reshape(nb * n, w)


def _window_counts(pos, w, n):
    half = w // 2
    return (jnp.minimum(pos + (w - half), n) - jnp.maximum(pos - half, 0)).astype(F32)


def _window_sum_rows(x, w, n, nb):
    rows, width = x.shape
    half = w // 2
    pos = lax.broadcasted_iota(jnp.int32, (n, width), 0)

    def keep(cond):
        return _tile_rows(cond.astype(F32), nb)

    fwd = x
    ln = 1
    while ln < half:
        fwd = fwd + pltpu.roll(fwd, rows - ln, 0) * keep(pos < n - ln)
        ln *= 2
    bwd = pltpu.roll(x, 1, 0) * keep(pos >= 1)
    ln = 1
    while ln < half:
        bwd = bwd + pltpu.roll(bwd, ln, 0) * keep(pos >= ln)
        ln *= 2
    return fwd + bwd


def _window_sum_bands(x3, w):
    half = w // 2

    def later(a, k):
        return jnp.concatenate([a[k:], jnp.zeros((k,) + a.shape[1:], a.dtype)], axis=0)

    def earlier(a, k):
        return jnp.concatenate([jnp.zeros((k,) + a.shape[1:], a.dtype), a[:-k]], axis=0)

    fwd = x3
    ln = 1
    while ln < half:
        fwd = fwd + later(fwd, ln)
        ln *= 2
    bwd = earlier(x3, 1)
    ln = 1
    while ln < half:
        bwd = bwd + earlier(bwd, ln)
        ln *= 2
    return fwd + bwd


def _pool_kernel(p_ref, w_ref, sc_ref, o_ref, *, seq, on_grid):
    grp = pl.program_id(1)
    rows, width = p_ref.shape
    for gi, w in enumerate(POOL_WINDOWS):
        @pl.when(grp == gi)
        def _(w=w):
            pg = p_ref[...]
            if on_grid:
                nb = rows // GRID_W
                pos_c = lax.broadcasted_iota(jnp.int32, (GRID_W, width), 0)
                pos_r = lax.broadcasted_iota(jnp.int32, (nb, 1, width), 0)
                m = _window_sum_rows(pg, w, GRID_W, nb)
                m = m * _tile_rows(1.0 / _window_counts(pos_c, w, GRID_W), nb)
                m3 = _window_sum_bands(m.reshape(nb, GRID_W, width), w)
                m = (m3 * (1.0 / _window_counts(pos_r, w, nb))).reshape(rows, width)
            else:
                nb = rows // seq
                pos = lax.broadcasted_iota(jnp.int32, (seq, width), 0)
                m = _window_sum_rows(pg, w, seq, nb)
                m = m * _tile_rows(1.0 / _window_counts(pos, w, seq), nb)
            d = (m - pg).astype(BF16)
            y = jnp.dot(d, w_ref[...], preferred_element_type=F32) * sc_ref[...]
            o_ref[...] = y.astype(o_ref.dtype)


def _pool(proj, w_pool_l, pool_scale_l, col0, batch, seq, on_grid):
    t = proj.shape[0]
    n_grp, gw, _ = w_pool_l.shape
    blk0 = col0 // gw
    spb = 1 if on_grid else min(POOL_SEQ_PER_STEP, batch)
    assert batch % spb == 0
    rows = spb * seq
    return pl.pallas_call(
        functools.partial(_pool_kernel, seq=seq, on_grid=on_grid),
        grid=(batch // spb, n_grp),
        in_specs=[pl.BlockSpec((rows, gw), lambda b, g: (b, blk0 + g)),
                  pl.BlockSpec((None, gw, gw), lambda b, g: (g, 0, 0)),
                  pl.BlockSpec((1, gw), lambda b, g: (0, g))],
        out_specs=pl.BlockSpec((rows, gw), lambda b, g: (b, g)),
        out_shape=jax.ShapeDtypeStruct((t, n_grp * gw), BF16),
        compiler_params=_params(("arbitrary", "arbitrary")),
        name="pool",
    )(proj, w_pool_l, pool_scale_l.reshape(1, n_grp * gw))


def _mix_kernel(o_ref, pm_ref, x_ref, mod_ref, w_ref, out_ref):
    k = o_ref.shape[1]
    mix = (jnp.dot(o_ref[...], w_ref[0:k, :], preferred_element_type=F32)
           + jnp.dot(pm_ref[...], w_ref[k:, :], preferred_element_type=F32))
    out_ref[...] = x_ref[...] + mod_ref[0][2:3] * mix


def _mix(o, pm, x2, mod, rows_per_mod, w_out, tm=512):
    t, d = x2.shape
    return pl.pallas_call(
        _mix_kernel,
        grid=(t // tm,),
        in_specs=[pl.BlockSpec((tm, o.shape[1]), lambda i: (i, 0)),
                  pl.BlockSpec((tm, pm.shape[1]), lambda i: (i, 0)),
                  pl.BlockSpec((tm, d), lambda i: (i, 0)),
                  pl.BlockSpec((1, N_MOD, d), lambda i: ((i * tm) // rows_per_mod, 0, 0)),
                  pl.BlockSpec(w_out.shape, lambda i: (0, 0))],
        out_specs=pl.BlockSpec((tm, d), lambda i: (i, 0)),
        out_shape=jax.ShapeDtypeStruct((t, d), F32),
        compiler_params=_params(("arbitrary",)),
        name="mix",
    )(o, pm, x2, mod, w_out)


def _ffn_up_kernel(x_ref, xp_ref, xn_ref, mod_ref, nw_ref, wa_ref, wb_ref, cwa_ref, cwb_ref,
                   cba_ref, cbb_ref, o_ref, h_ref, *, tm, seq):
    i = pl.program_id(0)

    @pl.when(pl.program_id(1) == 0)
    def _():
        mod = mod_ref[0]
        nw = nw_ref[...]
        sc, sh = mod[4:5], mod[3:4]
        starts_seq = (i * tm) % seq == 0
        ends_seq = ((i + 1) * tm) % seq == 0
        above = jnp.where(starts_seq, 0.0, _norm_mod(xp_ref[...], nw, sc, sh))
        below = jnp.where(ends_seq, 0.0, _norm_mod(xn_ref[...], nw, sc, sh))
        h_ref[0:CONV_HALO, :] = above.astype(BF16)
        h_ref[CONV_HALO:CONV_HALO + tm, :] = _norm_mod(x_ref[...], nw, sc, sh).astype(BF16)
        h_ref[CONV_HALO + tm:, :] = below.astype(BF16)

    ext = tm + 2 * CONV_HALO
    h = h_ref[...]

    def conv(w_ref, cw_ref, cb_ref):
        u = jnp.dot(h, w_ref[...], preferred_element_type=F32)
        um = u[CONV_HALO:CONV_HALO + tm]
        up = pltpu.roll(u, 1, 0)[CONV_HALO:CONV_HALO + tm]
        un = pltpu.roll(u, ext - 1, 0)[CONV_HALO:CONV_HALO + tm]
        if tm > seq:
            in_seq = lax.broadcasted_iota(jnp.int32, um.shape, 0) % seq
            up = jnp.where(in_seq == 0, 0.0, up)
            un = jnp.where(in_seq == seq - 1, 0.0, un)
        cw = cw_ref[...]
        return up * cw[0:1] + um * cw[1:2] + un * cw[2:3] + cb_ref[...]

    a = conv(wa_ref, cwa_ref, cba_ref)
    b = conv(wb_ref, cwb_ref, cbb_ref)
    o_ref[...] = (_silu(a) * b).astype(o_ref.dtype)


def _ffn_down_kernel(a_ref, x_ref, mod_ref, w_ref, fnw_ref, o_ref, *, final_norm):
    y = x_ref[...] + mod_ref[0][5:6] * jnp.dot(a_ref[...], w_ref[...],
                                               preferred_element_type=F32)
    if final_norm:
        y = y * lax.rsqrt(jnp.mean(y * y, axis=-1, keepdims=True) + EPS) * fnw_ref[...]
    o_ref[...] = y


def _ffn(x2, mod, rows_per_mod, seq, nw, w_up, conv_w, conv_b, w_down, fnw, final_norm,
         tm=1024, tf=512, tm_down=256):
    t, d = x2.shape
    dff = w_down.shape[0]
    nf = dff // tf
    assert seq % tm == 0 or tm % seq == 0
    hb = tm // CONV_HALO
    last_hb = t // CONV_HALO - 1
    cb = conv_b.reshape(1, 2 * dff)
    mod_spec = lambda tile: pl.BlockSpec((1, N_MOD, d),
                                         lambda i, *_: ((i * tile) // rows_per_mod, 0, 0))
    act = pl.pallas_call(
        functools.partial(_ffn_up_kernel, tm=tm, seq=seq),
        grid=(t // tm, nf),
        in_specs=[pl.BlockSpec((tm, d), lambda i, f: (i, 0)),
                  pl.BlockSpec((CONV_HALO, d), lambda i, f: (jnp.maximum(i * hb - 1, 0), 0)),
                  pl.BlockSpec((CONV_HALO, d), lambda i, f: (jnp.minimum((i + 1) * hb, last_hb), 0)),
                  mod_spec(tm),
                  pl.BlockSpec((1, d), lambda i, f: (0, 0)),
                  pl.BlockSpec((d, tf), lambda i, f: (0, f)),
                  pl.BlockSpec((d, tf), lambda i, f: (0, f + nf)),
                  pl.BlockSpec((3, tf), lambda i, f: (0, f)),
                  pl.BlockSpec((3, tf), lambda i, f: (0, f + nf)),
                  pl.BlockSpec((1, tf), lambda i, f: (0, f)),
                  pl.BlockSpec((1, tf), lambda i, f: (0, f + nf))],
        out_specs=pl.BlockSpec((tm, tf), lambda i, f: (i, f)),
        out_shape=jax.ShapeDtypeStruct((t, dff), BF16),
        scratch_shapes=[pltpu.VMEM((tm + 2 * CONV_HALO, d), BF16)],
        compiler_params=_params(("arbitrary", "arbitrary")),
        name="ffn_up",
    )(x2, x2, x2, mod, nw.reshape(1, d), w_up, w_up, conv_w, conv_w, cb, cb)
    return pl.pallas_call(
        functools.partial(_ffn_down_kernel, final_norm=final_norm),
        grid=(t // tm_down,),
        in_specs=[pl.BlockSpec((tm_down, dff), lambda i: (i, 0)),
                  pl.BlockSpec((tm_down, d), lambda i: (i, 0)),
                  mod_spec(tm_down),
                  pl.BlockSpec((dff, d), lambda i: (0, 0), pipeline_mode=pl.Buffered(1)),
                  pl.BlockSpec((1, d), lambda i: (0, 0))],
        out_specs=pl.BlockSpec((tm_down, d), lambda i: (i, 0)),
        out_shape=jax.ShapeDtypeStruct((t, d), F32),
        compiler_params=_params(("arbitrary",)),
        name="ffn_down",
    )(act, x2, mod, w_down, fnw.reshape(1, d))


def kernel(x_prompt, x_sample, state_hgrn, c, c_ctx, w_ada, b_ada, norm1_w, w_in, lb_param,
           hg_norm_w, w_pool, pool_scale, w_out, norm2_w, w_up, conv_w, conv_b, w_down,
           final_norm_w):
    depth = w_in.shape[0]
    bp, lp, d = x_prompt.shape
    bs, ls, _ = x_sample.shape
    hg_width = lb_param.shape[2]

    lb_all = jnp.cumsum(jax.nn.softmax(lb_param.astype(F32), axis=1), axis=1)
    n_c = 1 + bs
    pad = (-n_c) % 8
    cvecs = jnp.concatenate([c_ctx[None, :], c, jnp.zeros((pad, d), F32)], axis=0)

    xp = x_prompt.reshape(bp * lp, d)
    xs = x_sample.reshape(bs * ls, d)
    ctx_states = []
    for l in range(depth):
        last = l == depth - 1
        mods = _ada(cvecs, w_ada[l], b_ada[l])
        mod_p = mods[0:1].reshape(1, N_MOD, d)
        mod_s = mods[1:n_c].reshape(bs, N_MOD, d)
        w_in_l = w_in[l].astype(BF16)
        w_pool_l = w_pool[l].astype(BF16)
        w_out_l = w_out[l].astype(BF16)
        w_up_l = w_up[l].astype(BF16)
        w_down_l = w_down[l].astype(BF16)
        lb = lb_all[:, l]

        def layer(x2, mod, rows_per_mod, batch, seq, s0, on_grid, want_state):
            proj = _in_proj(x2, mod, rows_per_mod, norm1_w[l], w_in_l)
            o, st = _scan(proj, lb, hg_norm_w[l], s0, l, batch, seq, want_state)
            pm = _pool(proj, w_pool_l, pool_scale[l], 5 * hg_width, batch, seq, on_grid)
            x2 = _mix(o, pm, x2, mod, rows_per_mod, w_out_l)
            x2 = _ffn(x2, mod, rows_per_mod, seq, norm2_w[l], w_up_l, conv_w[l], conv_b[l],
                      w_down_l, final_norm_w, last)
            return x2, st

        xp, st = layer(xp, mod_p, bp * lp, bp, lp, None, False, True)
        ctx_states.append(st)
        xs, _ = layer(xs, mod_s, ls, bs, ls, state_hgrn, True, False)

    y_prompt = xp.reshape(bp, lp, d)
    y_sample = xs.reshape(bs, ls, d)
    new_state = jnp.stack(ctx_states, axis=1)
    return (y_prompt, y_sample, new_state)
```

```python
import functools

import jax
import jax.numpy as jnp
import numpy as np
from jax import lax
from jax.experimental import pallas as pl
from jax.experimental.pallas import tpu as pltpu

F32 = jnp.float32
BF16 = jnp.bfloat16
EPS = 1e-6

HEAD_DIM = 128
N_MOD = 6
POOL_WINDOWS = (2, 4, 8, 16)
GRID_W = 64
POOL_SEQ_PER_STEP = 8
CONV_HALO = 8
NORM_ROWS = 16
V7X_VMEM_BYTES = 64 * 1024 * 1024
VMEM_LIMIT = V7X_VMEM_BYTES - 8 * 1024 * 1024


def _params(sem):
    return pltpu.CompilerParams(dimension_semantics=sem, vmem_limit_bytes=VMEM_LIMIT)


def _silu(x):
    hx = 0.5 * x
    return hx + hx * jnp.tanh(hx)


def _norm_mod(x, nw, sc, sh):
    scale = nw * (1.0 + sc)
    return x * lax.rsqrt(jnp.mean(x * x, axis=-1, keepdims=True) + EPS) * scale + sh


def _ada_kernel(c_ref, w_ref, b_ref, o_ref):
    s = _silu(c_ref[...]).astype(BF16)
    o_ref[...] = jnp.dot(s, w_ref[...].astype(BF16), preferred_element_type=F32) + b_ref[...]


def _ada(cvecs, w_ada, b_ada, tn=1024):
    r, d = cvecs.shape
    n = w_ada.shape[1]
    return pl.pallas_call(
        _ada_kernel,
        grid=(n // tn,),
        in_specs=[pl.BlockSpec((r, d), lambda j: (0, 0)),
                  pl.BlockSpec((d, tn), lambda j: (0, j)),
                  pl.BlockSpec((1, tn), lambda j: (0, j))],
        out_specs=pl.BlockSpec((r, tn), lambda j: (0, j)),
        out_shape=jax.ShapeDtypeStruct((r, n), F32),
        compiler_params=_params(("arbitrary",)),
        name="ada",
    )(cvecs, w_ada, b_ada.reshape(1, n))


def _in_proj_kernel(x_ref, mod_ref, nw_ref, w_ref, o_ref):
    mod = mod_ref[0]
    h = _norm_mod(x_ref[...], nw_ref[...], mod[1:2], mod[0:1]).astype(BF16)
    o_ref[...] = jnp.dot(h, w_ref[...], preferred_element_type=F32)


def _in_proj(x2, mod, rows_per_mod, nw, w, tm=256):
    t, d = x2.shape
    n = w.shape[1]
    return pl.pallas_call(
        _in_proj_kernel,
        grid=(t // tm,),
        in_specs=[pl.BlockSpec((tm, d), lambda i: (i, 0)),
                  pl.BlockSpec((1, N_MOD, d), lambda i: ((i * tm) // rows_per_mod, 0, 0)),
                  pl.BlockSpec((1, d), lambda i: (0, 0)),
                  pl.BlockSpec((d, n), lambda i: (0, 0), pipeline_mode=pl.Buffered(1))],
        out_specs=pl.BlockSpec((tm, n), lambda i: (i, 0)),
        out_shape=jax.ShapeDtypeStruct((t, n), F32),
        compiler_params=_params(("arbitrary",)),
        name="in_proj",
    )(x2, mod, nw.reshape(1, d), w)


LOG2E = 1.4426950408889634
TOTAL_ROWS = 16
SHORT_SEQ_PER_STEP = 4
SCAN_INTERLEAVE = 2


def _decay_matrices(c):
    t = np.arange(c)
    blocks = []
    m = 2
    while m < c:
        mid = (t // (2 * m)) * (2 * m) + m
        second = (t & m) != 0
        qm = np.zeros((c, 2 * c), np.float32)
        km = np.zeros((c, 2 * c), np.float32)
        for i in range(c):
            if second[i]:
                qm[i, mid[i]:i + 1] = 1
                km[i, c + mid[i]:c + i] = 1
            else:
                qm[i, c + i:c + mid[i]] = 1
                km[i, i + 1:mid[i]] = 1
        blocks += [qm, km]
        m *= 2
    qf = np.zeros((c, 2 * c), np.float32)
    qb = np.zeros((c, 2 * c), np.float32)
    kf = np.zeros((c, 2 * c), np.float32)
    kb = np.zeros((c, c), np.float32)
    for i in range(c):
        qf[i, 0:i + 1] = 1
        qb[i, c + i:2 * c] = 1
        kf[i, i + 1:c] = 1
        kb[i, 0:i] = 1
    tf = np.zeros((TOTAL_ROWS, 2 * c), np.float32)
    tf[:, 0:c] = 1
    tb = np.ones((TOTAL_ROWS, c), np.float32)
    sweep = np.concatenate(blocks + [qf, qb, kf, tf], axis=0)
    back = np.concatenate([kb, tb], axis=0)
    return (jnp.asarray(np.concatenate([sweep, sweep], axis=1), BF16),
            jnp.asarray(np.concatenate([back, back], axis=1), BF16))


def _gates(z, lbd):
    f = lbd + (1.0 - lbd) * jax.nn.sigmoid(z)
    return jnp.log(f) * LOG2E, 1.0 - f


def _split_bf16(x):
    hi = x.astype(BF16)
    return hi, (x - hi.astype(F32)).astype(BF16)


def _dot_nt(a, b):
    return lax.dot_general(a, b, (((1,), (1,)), ((), ())), preferred_element_type=F32)


def _dot_tn(a, b):
    return lax.dot_general(a, b, (((0,), (0,)), ((), ())), preferred_element_type=F32)


def _scan_kernel(*refs, seq, spb, c, grp, grp_b, has_s0, want_state):
    q_ref, zf_ref, zb_ref, v_ref, g_ref, lb_ref, nw_ref, ms_ref, mb_ref = refs[:9]
    pos = 9
    s0_ref = None
    if has_s0:
        s0_ref = refs[pos]
        pos += 1
    o_ref = refs[pos]
    pos += 1
    sout_ref = None
    if want_state:
        sout_ref = refs[pos]
        pos += 1
    lfbh_scr, lfbl_scr, kb_scr, sb_scr, e_scr, qs_scr, kf_scr, gate_scr = refs[pos:pos + 8]

    gc = grp * c
    n_iters = seq // gc
    n_levels = c.bit_length() - 1
    base = 2 * (n_levels - 1) * c
    lb_f = lb_ref[0:1, :]
    lb_b = lb_ref[1:2, :]
    nw = nw_ref[...]
    row = lax.broadcasted_iota(jnp.int32, (c, HEAD_DIM), 0)
    pair = (lax.broadcasted_iota(jnp.int32, (c, c), 0)
            ^ lax.broadcasted_iota(jnp.int32, (c, c), 1))
    level = jnp.full((c, c), -1, jnp.int32)
    for k in range(n_levels):
        level = level + (pair >= (1 << k)).astype(jnp.int32)

    def initial_state(sq, direction):
        if has_s0:
            return s0_ref[sq, direction].T
        return jnp.zeros((HEAD_DIM, HEAD_DIM), F32)

    def chunk_rows(x, gi):
        return x[gi * c:(gi + 1) * c]

    def chunk_lanes(x, gi):
        return x[:, gi * HEAD_DIM:(gi + 1) * HEAD_DIM]

    def stack_chunks(parts, n):
        return jnp.concatenate(
            [jnp.concatenate([chunk_rows(p, gi) for p in parts], axis=0) for gi in range(n)],
            axis=1)

    gcb = grp_b * c

    def sweep_b(i, stb, sq):
        it = (sq + 1) * (seq // gcb) - 1 - i
        rows = pl.ds(pl.multiple_of(it * gcb, gcb), gcb)
        l2f_b, k_b = _gates(zb_ref[rows, :], lb_b)
        bh, bl = _split_bf16(l2f_b)
        lfbh_scr[rows, :] = bh
        lfbl_scr[rows, :] = bl
        kb_scr[rows, :] = k_b
        v_bf = v_ref[rows, :].astype(BF16)
        e = jnp.exp2(jnp.dot(mb_ref[...], stack_chunks((bh, bl), grp_b),
                             preferred_element_type=F32))
        kv = []
        for gi in range(grp_b):
            k_hat = (chunk_rows(k_b, gi) * chunk_lanes(e, gi)[0:c]).astype(BF16)
            kv.append(_dot_tn(chunk_rows(v_bf, gi), k_hat))
        for gi in reversed(range(grp_b)):
            sb_scr[it * grp_b + gi] = stb.astype(BF16)
            stb = stb * chunk_lanes(e, gi)[c:c + 1] + kv[gi]
        return stb

    stb_final = [lax.fori_loop(0, seq // gcb, functools.partial(sweep_b, sq=sq),
                               initial_state(sq, 1), unroll=True) for sq in range(spb)]

    def block_rows(blk):
        return pl.ds(pl.multiple_of(blk * gc, gc), gc)

    def stage_a_gates(blk):
        rows = block_rows(blk)
        qs = _silu(q_ref[rows, :])
        gate = _silu(g_ref[rows, :])
        l2f_f, k_f = _gates(zf_ref[rows, :], lb_f)
        fh, fl = _split_bf16(l2f_f)
        bh, bl = lfbh_scr[rows, :], lfbl_scr[rows, :]
        return qs, gate, k_f, stack_chunks((fh, bh, fl, bl), grp)

    def stage_a_exponents(slot, vals, chunks):
        lanes = slice(chunks[0] * HEAD_DIM, (chunks[-1] + 1) * HEAD_DIM)
        e_scr[slot, :, lanes] = jnp.exp2(jnp.dot(ms_ref[...], vals[3][:, lanes],
                                                 preferred_element_type=F32))

    def stage_a_store(slot, vals):
        qs_scr[slot], gate_scr[slot], kf_scr[slot] = vals[0], vals[1], vals[2]

    def stage_b_levels(blk, slot, chunks):
        rows = block_rows(blk)
        vbf_all = v_ref[rows, :].astype(BF16)
        kb_all = kb_scr[rows, :]
        scores, kv, dec = [], [], []
        for gi in chunks:
            lanes = slice(gi * HEAD_DIM, (gi + 1) * HEAD_DIM)
            crow = slice(gi * c, (gi + 1) * c)
            qs = qs_scr[slot, crow, :]
            k_f = kf_scr[slot, crow, :]
            k_b = chunk_rows(kb_all, gi)
            odd = (row & 1) != 0
            x = (qs * (1.0 - jnp.where(odd, k_f, k_b))).astype(BF16)
            y = jnp.where(odd, k_b, k_f).astype(BF16)
            a = jnp.where(level == 0, _dot_nt(x, y), 0.0)
            for lv in range(1, n_levels):
                second = (row & (1 << lv)) != 0
                r0 = 2 * (lv - 1) * c
                x = (qs * e_scr[slot, r0:r0 + c, lanes]).astype(BF16)
                y = (jnp.where(second, k_b, k_f)
                     * e_scr[slot, r0 + c:r0 + 2 * c, lanes]).astype(BF16)
                a = jnp.where(level == lv, _dot_nt(x, y), a)
            scores.append(a.astype(BF16))
            k_hat = (k_f * e_scr[slot, base + 2 * c:base + 3 * c, lanes]).astype(BF16)
            kv.append(_dot_tn(chunk_rows(vbf_all, gi), k_hat))
            dec.append(e_scr[slot, base + 3 * c:base + 3 * c + 1, lanes])
        return scores, kv, dec

    def stage_b_finish(blk, slot, stf, lv_out):
        scores, kv, dec = lv_out
        rows = block_rows(blk)
        v_all = v_ref[rows, :]
        vbf_all = v_all.astype(BF16)
        kb_all = kb_scr[rows, :]
        states = []
        for gi in range(grp):
            states.append(stf)
            stf = stf * dec[gi] + kv[gi]
        outs = []
        for gi in range(grp):
            lanes = slice(gi * HEAD_DIM, (gi + 1) * HEAD_DIM)
            crow = slice(gi * c, (gi + 1) * c)
            qs = qs_scr[slot, crow, :]
            q_fb = jnp.concatenate([qs * e_scr[slot, base:base + c, lanes],
                                    qs * e_scr[slot, base + c:base + 2 * c, lanes]],
                                   axis=1).astype(BF16)
            st_fb = jnp.concatenate([states[gi].astype(BF16), sb_scr[blk * grp + gi]], axis=1)
            o = jnp.dot(scores[gi], chunk_rows(vbf_all, gi), preferred_element_type=F32)
            o = o + jnp.sum(qs * (kf_scr[slot, crow, :] + chunk_rows(kb_all, gi)),
                            axis=-1, keepdims=True) * chunk_rows(v_all, gi)
            o = o + _dot_nt(q_fb, st_fb)
            o = o * lax.rsqrt(jnp.mean(o * o, axis=-1, keepdims=True) + EPS) * nw
            outs.append(o * gate_scr[slot, crow, :])
        o_ref[rows, :] = jnp.concatenate(outs, axis=0).astype(o_ref.dtype)
        return stf

    def step(cur, cur_slot, nxt, stf):
        ahead = stage_a_gates(nxt)
        stage_a_store(1 - cur_slot, ahead)
        lv_out = [[], [], []]
        for part in range(0, grp, SCAN_INTERLEAVE):
            chunks = range(part, min(part + SCAN_INTERLEAVE, grp))
            stage_a_exponents(1 - cur_slot, ahead, chunks)
            for acc, new in zip(lv_out, stage_b_levels(cur, cur_slot, chunks)):
                acc.extend(new)
        return stage_b_finish(cur, cur_slot, stf, lv_out)

    if n_iters == 1:
        ahead = [stage_a_gates(sq) for sq in range(spb)]
        for sq in range(spb):
            stage_a_store(sq, ahead[sq])
        stage_a_exponents(0, ahead[0], range(grp))
        lv_out = []
        for sq in range(spb):
            if sq + 1 < spb:
                stage_a_exponents(sq + 1, ahead[sq + 1], range(grp))
            lv_out.append(stage_b_levels(sq, sq, range(grp)))
        stf_final = [stage_b_finish(sq, sq, initial_state(sq, 0), lv_out[sq])
                     for sq in range(spb)]
    else:
        assert n_iters % 2 == 0 and spb == 1
        first = stage_a_gates(0)
        stage_a_store(0, first)
        stage_a_exponents(0, first, range(grp))

        def sweep_f(j, stf):
            stf = step(2 * j, 0, 2 * j + 1, stf)
            return step(2 * j + 1, 1, jnp.minimum(2 * j + 2, n_iters - 1), stf)

        stf_final = [lax.fori_loop(0, n_iters // 2, sweep_f, initial_state(0, 0))]

    if want_state:
        for sq in range(spb):
            sout_ref[sq, 0] = stf_final[sq].T
            sout_ref[sq, 1] = stb_final[sq].T


def _scan(proj, lb, hg_nw, s0, layer, batch, seq, want_state, c=64, grp=8, grp_b=8):
    t = proj.shape[0]
    n_heads = lb.shape[1] // HEAD_DIM
    has_s0 = s0 is not None
    grp = min(grp, seq // c)
    grp_b = min(grp_b, seq // c)
    spb = min(SHORT_SEQ_PER_STEP, batch) if seq == grp * c else 1
    assert batch % spb == 0
    rows = spb * seq
    m_sweep, m_back = _decay_matrices(c)

    def sec(k):
        return pl.BlockSpec((rows, HEAD_DIM), lambda b, h, k=k: (b, k * n_heads + h))

    state_block = (spb, None, 2, None, HEAD_DIM, HEAD_DIM)
    state_map = lambda b, h: (b, layer, 0, h, 0, 0)
    in_specs = [sec(0), sec(1), sec(2), sec(3), sec(4),
                pl.BlockSpec((2, HEAD_DIM), lambda b, h: (0, h)),
                pl.BlockSpec((1, HEAD_DIM), lambda b, h: (0, 0)),
                pl.BlockSpec(m_sweep.shape, lambda b, h: (0, 0)),
                pl.BlockSpec(m_back.shape, lambda b, h: (0, 0))]
    args = [proj, proj, proj, proj, proj, lb, hg_nw.reshape(1, HEAD_DIM), m_sweep, m_back]
    if has_s0:
        in_specs.append(pl.BlockSpec(state_block, state_map))
        args.append(s0)
    out_specs = [pl.BlockSpec((rows, HEAD_DIM), lambda b, h: (b, h))]
    out_shape = [jax.ShapeDtypeStruct((t, n_heads * HEAD_DIM), BF16)]
    if want_state:
        out_specs.append(pl.BlockSpec((spb, 2, None, HEAD_DIM, HEAD_DIM),
                                      lambda b, h: (b, 0, h, 0, 0)))
        out_shape.append(jax.ShapeDtypeStruct((batch, 2, n_heads, HEAD_DIM, HEAD_DIM), F32))
    gc = grp * c
    slots = max(2, spb)
    res = pl.pallas_call(
        functools.partial(_scan_kernel, seq=seq, spb=spb, c=c, grp=grp, grp_b=grp_b,
                          has_s0=has_s0, want_state=want_state),
        grid=(batch // spb, n_heads),
        in_specs=in_specs,
        out_specs=out_specs,
        out_shape=out_shape,
        scratch_shapes=[pltpu.VMEM((rows, HEAD_DIM), BF16),
                        pltpu.VMEM((rows, HEAD_DIM), BF16),
                        pltpu.VMEM((rows, HEAD_DIM), F32),
                        pltpu.VMEM((rows // c, HEAD_DIM, HEAD_DIM), BF16),
                        pltpu.VMEM((slots, m_sweep.shape[0], grp * HEAD_DIM), F32),
                        pltpu.VMEM((slots, gc, HEAD_DIM), F32),
                        pltpu.VMEM((slots, gc, HEAD_DIM), F32),
                        pltpu.VMEM((slots, gc, HEAD_DIM), F32)],
        compiler_params=_params(("arbitrary", "arbitrary")),
        name="scan",
    )(*args)
    return (res[0], res[1]) if want_state else (res[0], None)


def _tile_rows(small, nb):
    n, w = small.shape
    return jnp.broadcast_to(small[None], (nb, n, w)).reshape(nb * n, w)


def _window_counts(pos, w, n):
    half = w // 2
    return (jnp.minimum(pos + (w - half), n) - jnp.maximum(pos - half, 0)).astype(F32)


def _window_sum_rows(x, w, n, nb):
    rows, width = x.shape
    half = w // 2
    pos = lax.broadcasted_iota(jnp.int32, (n, width), 0)

    def keep(cond):
        return _tile_rows(cond.astype(F32), nb)

    fwd = x
    ln = 1
    while ln < half:
        fwd = fwd + pltpu.roll(fwd, rows - ln, 0) * keep(pos < n - ln)
        ln *= 2
    bwd = pltpu.roll(x, 1, 0) * keep(pos >= 1)
    ln = 1
    while ln < half:
        bwd = bwd + pltpu.roll(bwd, ln, 0) * keep(pos >= ln)
        ln *= 2
    return fwd + bwd


def _window_sum_bands(x3, w):
    half = w // 2

    def later(a, k):
        return jnp.concatenate([a[k:], jnp.zeros((k,) + a.shape[1:], a.dtype)], axis=0)

    def earlier(a, k):
        return jnp.concatenate([jnp.zeros((k,) + a.shape[1:], a.dtype), a[:-k]], axis=0)

    fwd = x3
    ln = 1
    while ln < half:
        fwd = fwd + later(fwd, ln)
        ln *= 2
    bwd = earlier(x3, 1)
    ln = 1
    while ln < half:
        bwd = bwd + earlier(bwd, ln)
        ln *= 2
    return fwd + bwd


def _pool_kernel(p_ref, w_ref, sc_ref, o_ref, *, seq, on_grid):
    grp = pl.program_id(1)
    rows, width = p_ref.shape
    for gi, w in enumerate(POOL_WINDOWS):
        @pl.when(grp == gi)
        def _(w=w):
            pg = p_ref[...]
            if on_grid:
                nb = rows // GRID_W
                pos_c = lax.broadcasted_iota(jnp.int32, (GRID_W, width), 0)
                pos_r = lax.broadcasted_iota(jnp.int32, (nb, 1, width), 0)
                m = _window_sum_rows(pg, w, GRID_W, nb)
                m = m * _tile_rows(1.0 / _window_counts(pos_c, w, GRID_W), nb)
                m3 = _window_sum_bands(m.reshape(nb, GRID_W, width), w)
                m = (m3 * (1.0 / _window_counts(pos_r, w, nb))).reshape(rows, width)
            else:
                nb = rows // seq
                pos = lax.broadcasted_iota(jnp.int32, (seq, width), 0)
                m = _window_sum_rows(pg, w, seq, nb)
                m = m * _tile_rows(1.0 / _window_counts(pos, w, seq), nb)
            d = (m - pg).astype(BF16)
            y = jnp.dot(d, w_ref[...], preferred_element_type=F32) * sc_ref[...]
            o_ref[...] = y.astype(o_ref.dtype)


def _pool(proj, w_pool_l, pool_scale_l, col0, batch, seq, on_grid):
    t = proj.shape[0]
    n_grp, gw, _ = w_pool_l.shape
    blk0 = col0 // gw
    spb = 1 if on_grid else min(POOL_SEQ_PER_STEP, batch)
    assert batch % spb == 0
    rows = spb * seq
    return pl.pallas_call(
        functools.partial(_pool_kernel, seq=seq, on_grid=on_grid),
        grid=(batch // spb, n_grp),
        in_specs=[pl.BlockSpec((rows, gw), lambda b, g: (b, blk0 + g)),
                  pl.BlockSpec((None, gw, gw), lambda b, g: (g, 0, 0)),
                  pl.BlockSpec((1, gw), lambda b, g: (0, g))],
        out_specs=pl.BlockSpec((rows, gw), lambda b, g: (b, g)),
        out_shape=jax.ShapeDtypeStruct((t, n_grp * gw), BF16),
        compiler_params=_params(("arbitrary", "arbitrary")),
        name="pool",
    )(proj, w_pool_l, pool_scale_l.reshape(1, n_grp * gw))


def _mix_kernel(o_ref, pm_ref, x_ref, mod_ref, w_ref, out_ref):
    k = o_ref.shape[1]
    mix = (jnp.dot(o_ref[...], w_ref[0:k, :], preferred_element_type=F32)
           + jnp.dot(pm_ref[...], w_ref[k:, :], preferred_element_type=F32))
    out_ref[...] = x_ref[...] + mod_ref[0][2:3] * mix


def _mix(o, pm, x2, mod, rows_per_mod, w_out, tm=512):
    t, d = x2.shape
    return pl.pallas_call(
        _mix_kernel,
        grid=(t // tm,),
        in_specs=[pl.BlockSpec((tm, o.shape[1]), lambda i: (i, 0)),
                  pl.BlockSpec((tm, pm.shape[1]), lambda i: (i, 0)),
                  pl.BlockSpec((tm, d), lambda i: (i, 0)),
                  pl.BlockSpec((1, N_MOD, d), lambda i: ((i * tm) // rows_per_mod, 0, 0)),
                  pl.BlockSpec(w_out.shape, lambda i: (0, 0))],
        out_specs=pl.BlockSpec((tm, d), lambda i: (i, 0)),
        out_shape=jax.ShapeDtypeStruct((t, d), F32),
        compiler_params=_params(("arbitrary",)),
        name="mix",
    )(o, pm, x2, mod, w_out)


def _ffn_up_kernel(x_ref, xp_ref, xn_ref, mod_ref, nw_ref, wa_ref, wb_ref, cwa_ref, cwb_ref,
                   cba_ref, cbb_ref, o_ref, h_ref, *, tm, seq):
    i = pl.program_id(0)

    @pl.when(pl.program_id(1) == 0)
    def _():
        mod = mod_ref[0]
        nw = nw_ref[...]
        sc, sh = mod[4:5], mod[3:4]
        starts_seq = (i * tm) % seq == 0
        ends_seq = ((i + 1) * tm) % seq == 0
        above = jnp.where(starts_seq, 0.0, _norm_mod(xp_ref[...], nw, sc, sh))
        below = jnp.where(ends_seq, 0.0, _norm_mod(xn_ref[...], nw, sc, sh))
        for r0 in range(0, tm, NORM_ROWS):
            h_ref[r0:r0 + NORM_ROWS, :] = _norm_mod(
                x_ref[r0:r0 + NORM_ROWS, :], nw, sc, sh).astype(BF16)
        h_ref[tm:tm + CONV_HALO, :] = below.astype(BF16)
        h_ref[tm + CONV_HALO:, :] = above.astype(BF16)

    ext = tm + 2 * CONV_HALO
    h = h_ref[...]

    def conv(w_ref, cw_ref, cb_ref):
        u = jnp.dot(h, w_ref[...], preferred_element_type=F32)
        um = u[0:tm]
        up = pltpu.roll(u, 1, 0)[0:tm]
        un = pltpu.roll(u, ext - 1, 0)[0:tm]
        if tm > seq:
            in_seq = lax.broadcasted_iota(jnp.int32, um.shape, 0) % seq
            up = jnp.where(in_seq == 0, 0.0, up)
            un = jnp.where(in_seq == seq - 1, 0.0, un)
        cw = cw_ref[...]
        return up * cw[0:1] + um * cw[1:2] + un * cw[2:3] + cb_ref[...]

    a = conv(wa_ref, cwa_ref, cba_ref)
    b = conv(wb_ref, cwb_ref, cbb_ref)
    o_ref[...] = (_silu(a) * b).astype(o_ref.dtype)


def _ffn_down_kernel(a_ref, x_ref, mod_ref, w_ref, fnw_ref, o_ref, *, final_norm):
    y = x_ref[...] + mod_ref[0][5:6] * jnp.dot(a_ref[...], w_ref[...],
                                               preferred_element_type=F32)
    if final_norm:
        y = y * lax.rsqrt(jnp.mean(y * y, axis=-1, keepdims=True) + EPS) * fnw_ref[...]
    o_ref[...] = y


def _ffn(x2, mod, rows_per_mod, seq, nw, w_up, conv_w, conv_b, w_down, fnw, final_norm,
         tm=1024, tf=512, tm_down=256):
    t, d = x2.shape
    dff = w_down.shape[0]
    nf = dff // tf
    assert seq % tm == 0 or tm % seq == 0
    hb = tm // CONV_HALO
    last_hb = t // CONV_HALO - 1
    cb = conv_b.reshape(1, 2 * dff)
    mod_spec = lambda tile: pl.BlockSpec((1, N_MOD, d),
                                         lambda i, *_: ((i * tile) // rows_per_mod, 0, 0))
    act = pl.pallas_call(
        functools.partial(_ffn_up_kernel, tm=tm, seq=seq),
        grid=(t // tm, nf),
        in_specs=[pl.BlockSpec((tm, d), lambda i, f: (i, 0)),
                  pl.BlockSpec((CONV_HALO, d), lambda i, f: (jnp.maximum(i * hb - 1, 0), 0)),
                  pl.BlockSpec((CONV_HALO, d), lambda i, f: (jnp.minimum((i + 1) * hb, last_hb), 0)),
                  mod_spec(tm),
                  pl.BlockSpec((1, d), lambda i, f: (0, 0)),
                  pl.BlockSpec((d, tf), lambda i, f: (0, f)),
                  pl.BlockSpec((d, tf), lambda i, f: (0, f + nf)),
                  pl.BlockSpec((3, tf), lambda i, f: (0, f)),
                  pl.BlockSpec((3, tf), lambda i, f: (0, f + nf)),
                  pl.BlockSpec((1, tf), lambda i, f: (0, f)),
                  pl.BlockSpec((1, tf), lambda i, f: (0, f + nf))],
        out_specs=pl.BlockSpec((tm, tf), lambda i, f: (i, f)),
        out_shape=jax.ShapeDtypeStruct((t, dff), BF16),
        scratch_shapes=[pltpu.VMEM((tm + 2 * CONV_HALO, d), BF16)],
        compiler_params=_params(("arbitrary", "arbitrary")),
        name="ffn_up",
    )(x2, x2, x2, mod, nw.reshape(1, d), w_up, w_up, conv_w, conv_w, cb, cb)
    return pl.pallas_call(
        functools.partial(_ffn_down_kernel, final_norm=final_norm),
        grid=(t // tm_down,),
        in_specs=[pl.BlockSpec((tm_down, dff), lambda i: (i, 0)),
                  pl.BlockSpec((tm_down, d), lambda i: (i, 0)),
                  mod_spec(tm_down),
                  pl.BlockSpec((dff, d), lambda i: (0, 0), pipeline_mode=pl.Buffered(1)),
                  pl.BlockSpec((1, d), lambda i: (0, 0))],
        out_specs=pl.BlockSpec((tm_down, d), lambda i: (i, 0)),
        out_shape=jax.ShapeDtypeStruct((t, d), F32),
        compiler_params=_params(("arbitrary",)),
        name="ffn_down",
    )(act, x2, mod, w_down, fnw.reshape(1, d))


def kernel(x_prompt, x_sample, state_hgrn, c, c_ctx, w_ada, b_ada, norm1_w, w_in, lb_param,
           hg_norm_w, w_pool, pool_scale, w_out, norm2_w, w_up, conv_w, conv_b, w_down,
           final_norm_w):
    depth = w_in.shape[0]
    bp, lp, d = x_prompt.shape
    bs, ls, _ = x_sample.shape
    hg_width = lb_param.shape[2]

    lb_all = jnp.cumsum(jax.nn.softmax(lb_param.astype(F32), axis=1), axis=1)
    n_c = 1 + bs
    pad = (-n_c) % 8
    cvecs = jnp.concatenate([c_ctx[None, :], c, jnp.zeros((pad, d), F32)], axis=0)

    xp = x_prompt.reshape(bp * lp, d)
    xs = x_sample.reshape(bs * ls, d)
    ctx_states = []
    for l in range(depth):
        last = l == depth - 1
        mods = _ada(cvecs, w_ada[l], b_ada[l])
        mod_p = mods[0:1].reshape(1, N_MOD, d)
        mod_s = mods[1:n_c].reshape(bs, N_MOD, d)
        w_in_l = w_in[l].astype(BF16)
        w_pool_l = w_pool[l].astype(BF16)
        w_out_l = w_out[l].astype(BF16)
        w_up_l = w_up[l].astype(BF16)
        w_down_l = w_down[l].astype(BF16)
        lb = lb_all[:, l]

        def layer(x2, mod, rows_per_mod, batch, seq, s0, on_grid, want_state):
            proj = _in_proj(x2, mod, rows_per_mod, norm1_w[l], w_in_l)
            o, st = _scan(proj, lb, hg_norm_w[l], s0, l, batch, seq, want_state)
            pm = _pool(proj, w_pool_l, pool_scale[l], 5 * hg_width, batch, seq, on_grid)
            x2 = _mix(o, pm, x2, mod, rows_per_mod, w_out_l)
            x2 = _ffn(x2, mod, rows_per_mod, seq, norm2_w[l], w_up_l, conv_w[l], conv_b[l],
                      w_down_l, final_norm_w, last)
            return x2, st

        xp, st = layer(xp, mod_p, bp * lp, bp, lp, None, False, True)
        ctx_states.append(st)
        xs, _ = layer(xs, mod_s, ls, bs, ls, state_hgrn, True, False)

    y_prompt = xp.reshape(bp, lp, d)
    y_sample = xs.reshape(bs, ls, d)
    new_state = jnp.stack(ctx_states, axis=1)
    return (y_prompt, y_sample, new_state)
```

```python
import functools

import jax
import jax.numpy as jnp
import numpy as np
from jax import lax
from jax.experimental import pallas as pl
from jax.experimental.pallas import tpu as pltpu

F32 = jnp.float32
BF16 = jnp.bfloat16
EPS = 1e-6

HEAD_DIM = 128
N_MOD = 6
POOL_WINDOWS = (2, 4, 8, 16)
GRID_W = 64
POOL_SEQ_PER_STEP = 8
CONV_HALO = 8
NORM_ROWS = 16
V7X_VMEM_BYTES = 64 * 1024 * 1024
VMEM_LIMIT = V7X_VMEM_BYTES - 8 * 1024 * 1024


def _params(sem):
    return pltpu.CompilerParams(dimension_semantics=sem, vmem_limit_bytes=VMEM_LIMIT)


def _silu(x):
    hx = 0.5 * x
    return hx + hx * jnp.tanh(hx)


def _norm_mod(x, nw, sc, sh):
    scale = nw * (1.0 + sc)
    return x * lax.rsqrt(jnp.mean(x * x, axis=-1, keepdims=True) + EPS) * scale + sh


def _ada_kernel(c_ref, w_ref, b_ref, o_ref):
    s = _silu(c_ref[...]).astype(BF16)
    o_ref[...] = jnp.dot(s, w_ref[...].astype(BF16), preferred_element_type=F32) + b_ref[...]


def _ada(cvecs, w_ada, b_ada, tn=1024):
    r, d = cvecs.shape
    n = w_ada.shape[1]
    return pl.pallas_call(
        _ada_kernel,
        grid=(n // tn,),
        in_specs=[pl.BlockSpec((r, d), lambda j: (0, 0)),
                  pl.BlockSpec((d, tn), lambda j: (0, j)),
                  pl.BlockSpec((1, tn), lambda j: (0, j))],
        out_specs=pl.BlockSpec((r, tn), lambda j: (0, j)),
        out_shape=jax.ShapeDtypeStruct((r, n), F32),
        compiler_params=_params(("arbitrary",)),
        name="ada",
    )(cvecs, w_ada, b_ada.reshape(1, n))


def _in_proj_kernel(x_ref, mod_ref, nw_ref, w_ref, o_ref):
    mod = mod_ref[0]
    h = _norm_mod(x_ref[...], nw_ref[...], mod[1:2], mod[0:1]).astype(BF16)
    o_ref[...] = jnp.dot(h, w_ref[...], preferred_element_type=F32)


def _in_proj(x2, mod, rows_per_mod, nw, w, tm=256):
    t, d = x2.shape
    n = w.shape[1]
    return pl.pallas_call(
        _in_proj_kernel,
        grid=(t // tm,),
        in_specs=[pl.BlockSpec((tm, d), lambda i: (i, 0)),
                  pl.BlockSpec((1, N_MOD, d), lambda i: ((i * tm) // rows_per_mod, 0, 0)),
                  pl.BlockSpec((1, d), lambda i: (0, 0)),
                  pl.BlockSpec((d, n), lambda i: (0, 0), pipeline_mode=pl.Buffered(1))],
        out_specs=pl.BlockSpec((tm, n), lambda i: (i, 0)),
        out_shape=jax.ShapeDtypeStruct((t, n), F32),
        compiler_params=_params(("arbitrary",)),
        name="in_proj",
    )(x2, mod, nw.reshape(1, d), w)


LOG2E = 1.4426950408889634
TOTAL_ROWS = 16
SHORT_SEQ_PER_STEP = 4
SCAN_INTERLEAVE = 2


def _decay_matrices(c):
    t = np.arange(c)
    blocks = []
    m = 2
    while m < c:
        mid = (t // (2 * m)) * (2 * m) + m
        second = (t & m) != 0
        qm = np.zeros((c, 2 * c), np.float32)
        km = np.zeros((c, 2 * c), np.float32)
        for i in range(c):
            if second[i]:
                qm[i, mid[i]:i + 1] = 1
                km[i, c + mid[i]:c + i] = 1
            else:
                qm[i, c + i:c + mid[i]] = 1
                km[i, i + 1:mid[i]] = 1
        blocks += [qm, km]
        m *= 2
    qf = np.zeros((c, 2 * c), np.float32)
    qb = np.zeros((c, 2 * c), np.float32)
    kf = np.zeros((c, 2 * c), np.float32)
    kb = np.zeros((c, c), np.float32)
    for i in range(c):
        qf[i, 0:i + 1] = 1
        qb[i, c + i:2 * c] = 1
        kf[i, i + 1:c] = 1
        kb[i, 0:i] = 1
    tf = np.zeros((TOTAL_ROWS, 2 * c), np.float32)
    tf[:, 0:c] = 1
    tb = np.ones((TOTAL_ROWS, c), np.float32)
    sweep = np.concatenate(blocks + [qf, qb, kf, tf], axis=0)
    back = np.concatenate([kb, tb], axis=0)
    return (jnp.asarray(np.concatenate([sweep, sweep], axis=1), BF16),
            jnp.asarray(np.concatenate([back, back], axis=1), BF16))


def _gates(z, lbd):
    f = lbd + (1.0 - lbd) * jax.nn.sigmoid(z)
    return jnp.log(f) * LOG2E, 1.0 - f


def _split_bf16(x):
    hi = x.astype(BF16)
    return hi, (x - hi.astype(F32)).astype(BF16)


def _dot_nt(a, b):
    return lax.dot_general(a, b, (((1,), (1,)), ((), ())), preferred_element_type=F32)


def _dot_tn(a, b):
    return lax.dot_general(a, b, (((0,), (0,)), ((), ())), preferred_element_type=F32)


def _scan_kernel(*refs, seq, spb, c, grp, grp_b, has_s0, want_state):
    q_ref, zf_ref, zb_ref, v_ref, g_ref, lb_ref, nw_ref, ms_ref, mb_ref = refs[:9]
    pos = 9
    s0_ref = None
    if has_s0:
        s0_ref = refs[pos]
        pos += 1
    o_ref = refs[pos]
    pos += 1
    sout_ref = None
    if want_state:
        sout_ref = refs[pos]
        pos += 1
    lfbh_scr, lfbl_scr, kb_scr, sb_scr, e_scr, qs_scr, kf_scr, gate_scr = refs[pos:pos + 8]

    gc = grp * c
    n_iters = seq // gc
    n_levels = c.bit_length() - 1
    base = 2 * (n_levels - 1) * c
    lb_f = lb_ref[0:1, :]
    lb_b = lb_ref[1:2, :]
    nw = nw_ref[...]
    row = lax.broadcasted_iota(jnp.int32, (c, HEAD_DIM), 0)
    pair = (lax.broadcasted_iota(jnp.int32, (c, c), 0)
            ^ lax.broadcasted_iota(jnp.int32, (c, c), 1))
    level = jnp.full((c, c), -1, jnp.int32)
    for k in range(n_levels):
        level = level + (pair >= (1 << k)).astype(jnp.int32)

    def initial_state(sq, direction):
        if has_s0:
            return s0_ref[sq, direction].T
        return jnp.zeros((HEAD_DIM, HEAD_DIM), F32)

    def chunk_rows(x, gi):
        return x[gi * c:(gi + 1) * c]

    def chunk_lanes(x, gi):
        return x[:, gi * HEAD_DIM:(gi + 1) * HEAD_DIM]

    def stack_chunks(parts, n):
        return jnp.concatenate(
            [jnp.concatenate([chunk_rows(p, gi) for p in parts], axis=0) for gi in range(n)],
            axis=1)

    gcb = grp_b * c

    def sweep_b(i, stb, sq):
        it = (sq + 1) * (seq // gcb) - 1 - i
        rows = pl.ds(pl.multiple_of(it * gcb, gcb), gcb)
        l2f_b, k_b = _gates(zb_ref[rows, :], lb_b)
        bh, bl = _split_bf16(l2f_b)
        lfbh_scr[rows, :] = bh
        lfbl_scr[rows, :] = bl
        kb_scr[rows, :] = k_b
        v_bf = v_ref[rows, :].astype(BF16)
        e = jnp.exp2(jnp.dot(mb_ref[...], stack_chunks((bh, bl), grp_b),
                             preferred_element_type=F32))
        kv = []
        for gi in range(grp_b):
            k_hat = (chunk_rows(k_b, gi) * chunk_lanes(e, gi)[0:c]).astype(BF16)
            kv.append(_dot_tn(chunk_rows(v_bf, gi), k_hat))
        for gi in reversed(range(grp_b)):
            sb_scr[it * grp_b + gi] = stb.astype(BF16)
            stb = stb * chunk_lanes(e, gi)[c:c + 1] + kv[gi]
        return stb

    stb_final = [lax.fori_loop(0, seq // gcb, functools.partial(sweep_b, sq=sq),
                               initial_state(sq, 1), unroll=True) for sq in range(spb)]

    def block_rows(blk):
        return pl.ds(pl.multiple_of(blk * gc, gc), gc)

    def stage_a_gates(blk):
        rows = block_rows(blk)
        qs = _silu(q_ref[rows, :])
        gate = _silu(g_ref[rows, :])
        l2f_f, k_f = _gates(zf_ref[rows, :], lb_f)
        fh, fl = _split_bf16(l2f_f)
        bh, bl = lfbh_scr[rows, :], lfbl_scr[rows, :]
        return qs, gate, k_f, stack_chunks((fh, bh, fl, bl), grp)

    def stage_a_exponents(slot, vals, chunks):
        lanes = slice(chunks[0] * HEAD_DIM, (chunks[-1] + 1) * HEAD_DIM)
        e_scr[slot, :, lanes] = jnp.exp2(jnp.dot(ms_ref[...], vals[3][:, lanes],
                                                 preferred_element_type=F32))

    def stage_a_store(slot, vals):
        qs_scr[slot], gate_scr[slot], kf_scr[slot] = vals[0], vals[1], vals[2]

    def stage_b_levels(blk, slot, chunks):
        rows = block_rows(blk)
        vbf_all = v_ref[rows, :].astype(BF16)
        kb_all = kb_scr[rows, :]
        scores, kv, dec = [], [], []
        for gi in chunks:
            lanes = slice(gi * HEAD_DIM, (gi + 1) * HEAD_DIM)
            crow = slice(gi * c, (gi + 1) * c)
            qs = qs_scr[slot, crow, :]
            k_f = kf_scr[slot, crow, :]
            k_b = chunk_rows(kb_all, gi)
            odd = (row & 1) != 0
            x = (qs * (1.0 - jnp.where(odd, k_f, k_b))).astype(BF16)
            y = jnp.where(odd, k_b, k_f).astype(BF16)
            a = jnp.where(level == 0, _dot_nt(x, y), 0.0)
            for lv in range(1, n_levels):
                second = (row & (1 << lv)) != 0
                r0 = 2 * (lv - 1) * c
                x = (qs * e_scr[slot, r0:r0 + c, lanes]).astype(BF16)
                y = (jnp.where(second, k_b, k_f)
                     * e_scr[slot, r0 + c:r0 + 2 * c, lanes]).astype(BF16)
                a = jnp.where(level == lv, _dot_nt(x, y), a)
            scores.append(a.astype(BF16))
            k_hat = (k_f * e_scr[slot, base + 2 * c:base + 3 * c, lanes]).astype(BF16)
            kv.append(_dot_tn(chunk_rows(vbf_all, gi), k_hat))
            dec.append(e_scr[slot, base + 3 * c:base + 3 * c + 1, lanes])
        return scores, kv, dec

    def stage_b_finish(blk, slot, stf, lv_out):
        scores, kv, dec = lv_out
        rows = block_rows(blk)
        v_all = v_ref[rows, :]
        vbf_all = v_all.astype(BF16)
        kb_all = kb_scr[rows, :]
        states = []
        for gi in range(grp):
            states.append(stf)
            stf = stf * dec[gi] + kv[gi]
        outs = []
        for gi in range(grp):
            lanes = slice(gi * HEAD_DIM, (gi + 1) * HEAD_DIM)
            crow = slice(gi * c, (gi + 1) * c)
            qs = qs_scr[slot, crow, :]
            q_fb = jnp.concatenate([qs * e_scr[slot, base:base + c, lanes],
                                    qs * e_scr[slot, base + c:base + 2 * c, lanes]],
                                   axis=1).astype(BF16)
            st_fb = jnp.concatenate([states[gi].astype(BF16), sb_scr[blk * grp + gi]], axis=1)
            o = jnp.dot(scores[gi], chunk_rows(vbf_all, gi), preferred_element_type=F32)
            o = o + jnp.sum(qs * (kf_scr[slot, crow, :] + chunk_rows(kb_all, gi)),
                            axis=-1, keepdims=True) * chunk_rows(v_all, gi)
            o = o + _dot_nt(q_fb, st_fb)
            o = o * lax.rsqrt(jnp.mean(o * o, axis=-1, keepdims=True) + EPS) * nw
            outs.append(o * gate_scr[slot, crow, :])
        o_ref[rows, :] = jnp.concatenate(outs, axis=0).astype(o_ref.dtype)
        return stf

    def step(cur, cur_slot, nxt, stf):
        ahead = stage_a_gates(nxt)
        stage_a_store(1 - cur_slot, ahead)
        lv_out = [[], [], []]
        for part in range(0, grp, SCAN_INTERLEAVE):
            chunks = range(part, min(part + SCAN_INTERLEAVE, grp))
            stage_a_exponents(1 - cur_slot, ahead, chunks)
            for acc, new in zip(lv_out, stage_b_levels(cur, cur_slot, chunks)):
                acc.extend(new)
        return stage_b_finish(cur, cur_slot, stf, lv_out)

    if n_iters == 1:
        ahead = [stage_a_gates(sq) for sq in range(spb)]
        for sq in range(spb):
            stage_a_store(sq, ahead[sq])
        stage_a_exponents(0, ahead[0], range(grp))
        lv_out = []
        for sq in range(spb):
            if sq + 1 < spb:
                stage_a_exponents(sq + 1, ahead[sq + 1], range(grp))
            lv_out.append(stage_b_levels(sq, sq, range(grp)))
        stf_final = [stage_b_finish(sq, sq, initial_state(sq, 0), lv_out[sq])
                     for sq in range(spb)]
    else:
        assert n_iters % 2 == 0 and spb == 1
        first = stage_a_gates(0)
        stage_a_store(0, first)
        stage_a_exponents(0, first, range(grp))

        def sweep_f(j, stf):
            stf = step(2 * j, 0, 2 * j + 1, stf)
            return step(2 * j + 1, 1, jnp.minimum(2 * j + 2, n_iters - 1), stf)

        stf_final = [lax.fori_loop(0, n_iters // 2, sweep_f, initial_state(0, 0))]

    if want_state:
        for sq in range(spb):
            sout_ref[sq, 0] = stf_final[sq].T
            sout_ref[sq, 1] = stb_final[sq].T


def _scan(proj, lb, hg_nw, s0, layer, batch, seq, want_state, c=64, grp=8, grp_b=8):
    t = proj.shape[0]
    n_heads = lb.shape[1] // HEAD_DIM
    has_s0 = s0 is not None
    grp = min(grp, seq // c)
    grp_b = min(grp_b, seq // c)
    spb = min(SHORT_SEQ_PER_STEP, batch) if seq == grp * c else 1
    assert batch % spb == 0
    rows = spb * seq
    m_sweep, m_back = _decay_matrices(c)

    def sec(k):
        return pl.BlockSpec((rows, HEAD_DIM), lambda b, h, k=k: (b, k * n_heads + h))

    state_block = (spb, None, 2, None, HEAD_DIM, HEAD_DIM)
    state_map = lambda b, h: (b, layer, 0, h, 0, 0)
    in_specs = [sec(0), sec(1), sec(2), sec(3), sec(4),
                pl.BlockSpec((2, HEAD_DIM), lambda b, h: (0, h)),
                pl.BlockSpec((1, HEAD_DIM), lambda b, h: (0, 0)),
                pl.BlockSpec(m_sweep.shape, lambda b, h: (0, 0)),
                pl.BlockSpec(m_back.shape, lambda b, h: (0, 0))]
    args = [proj, proj, proj, proj, proj, lb, hg_nw.reshape(1, HEAD_DIM), m_sweep, m_back]
    if has_s0:
        in_specs.append(pl.BlockSpec(state_block, state_map))
        args.append(s0)
    out_specs = [pl.BlockSpec((rows, HEAD_DIM), lambda b, h: (b, h))]
    out_shape = [jax.ShapeDtypeStruct((t, n_heads * HEAD_DIM), BF16)]
    if want_state:
        out_specs.append(pl.BlockSpec((spb, 2, None, HEAD_DIM, HEAD_DIM),
                                      lambda b, h: (b, 0, h, 0, 0)))
        out_shape.append(jax.ShapeDtypeStruct((batch, 2, n_heads, HEAD_DIM, HEAD_DIM), F32))
    gc = grp * c
    slots = max(2, spb)
    res = pl.pallas_call(
        functools.partial(_scan_kernel, seq=seq, spb=spb, c=c, grp=grp, grp_b=grp_b,
                          has_s0=has_s0, want_state=want_state),
        grid=(batch // spb, n_heads),
        in_specs=in_specs,
        out_specs=out_specs,
        out_shape=out_shape,
        scratch_shapes=[pltpu.VMEM((rows, HEAD_DIM), BF16),
                        pltpu.VMEM((rows, HEAD_DIM), BF16),
                        pltpu.VMEM((rows, HEAD_DIM), F32),
                        pltpu.VMEM((rows // c, HEAD_DIM, HEAD_DIM), BF16),
                        pltpu.VMEM((slots, m_sweep.shape[0], grp * HEAD_DIM), F32),
                        pltpu.VMEM((slots, gc, HEAD_DIM), F32),
                        pltpu.VMEM((slots, gc, HEAD_DIM), F32),
                        pltpu.VMEM((slots, gc, HEAD_DIM), F32)],
        compiler_params=_params(("arbitrary", "arbitrary")),
        name="scan",
    )(*args)
    return (res[0], res[1]) if want_state else (res[0], None)


def _tile_rows(small, nb):
    n, w = small.shape
    return jnp.broadcast_to(small[None], (nb, n, w)).reshape(nb * n, w)


def _window_counts(pos, w, n):
    half = w // 2
    return (jnp.minimum(pos + (w - half), n) - jnp.maximum(pos - half, 0)).astype(F32)


def _window_sum_rows(x, w, n, nb):
    rows, width = x.shape
    half = w // 2
    pos = lax.broadcasted_iota(jnp.int32, (n, width), 0)

    def keep(cond):
        return _tile_rows(cond.astype(F32), nb)

    fwd = x
    ln = 1
    while ln < half:
        fwd = fwd + pltpu.roll(fwd, rows - ln, 0) * keep(pos < n - ln)
        ln *= 2
    bwd = pltpu.roll(x, 1, 0) * keep(pos >= 1)
    ln = 1
    while ln < half:
        bwd = bwd + pltpu.roll(bwd, ln, 0) * keep(pos >= ln)
        ln *= 2
    return fwd + bwd


def _window_sum_bands(x3, w):
    half = w // 2

    def later(a, k):
        return jnp.concatenate([a[k:], jnp.zeros((k,) + a.shape[1:], a.dtype)], axis=0)

    def earlier(a, k):
        return jnp.concatenate([jnp.zeros((k,) + a.shape[1:], a.dtype), a[:-k]], axis=0)

    fwd = x3
    ln = 1
    while ln < half:
        fwd = fwd + later(fwd, ln)
        ln *= 2
    bwd = earlier(x3, 1)
    ln = 1
    while ln < half:
        bwd = bwd + earlier(bwd, ln)
        ln *= 2
    return fwd + bwd


def _pool_kernel(p_ref, w_ref, sc_ref, o_ref, *, seq, on_grid):
    grp = pl.program_id(1)
    rows, width = p_ref.shape
    for gi, w in enumerate(POOL_WINDOWS):
        @pl.when(grp == gi)
        def _(w=w):
            pg = p_ref[...]
            if on_grid:
                nb = rows // GRID_W
                pos_c = lax.broadcasted_iota(jnp.int32, (GRID_W, width), 0)
                pos_r = lax.broadcasted_iota(jnp.int32, (nb, 1, width), 0)
                m = _window_sum_rows(pg, w, GRID_W, nb)
                m = m * _tile_rows(1.0 / _window_counts(pos_c, w, GRID_W), nb)
                m3 = _window_sum_bands(m.reshape(nb, GRID_W, width), w)
                m = (m3 * (1.0 / _window_counts(pos_r, w, nb))).reshape(rows, width)
            else:
                nb = rows // seq
                pos = lax.broadcasted_iota(jnp.int32, (seq, width), 0)
                m = _window_sum_rows(pg, w, seq, nb)
                m = m * _tile_rows(1.0 / _window_counts(pos, w, seq), nb)
            d = (m - pg).astype(BF16)
            y = jnp.dot(d, w_ref[...], preferred_element_type=F32) * sc_ref[...]
            o_ref[...] = y.astype(o_ref.dtype)


def _pool(proj, w_pool_l, pool_scale_l, col0, batch, seq, on_grid):
    t = proj.shape[0]
    n_grp, gw, _ = w_pool_l.shape
    blk0 = col0 // gw
    spb = 1 if on_grid else min(POOL_SEQ_PER_STEP, batch)
    assert batch % spb == 0
    rows = spb * seq
    return pl.pallas_call(
        functools.partial(_pool_kernel, seq=seq, on_grid=on_grid),
        grid=(batch // spb, n_grp),
        in_specs=[pl.BlockSpec((rows, gw), lambda b, g: (b, blk0 + g)),
                  pl.BlockSpec((None, gw, gw), lambda b, g: (g, 0, 0)),
                  pl.BlockSpec((1, gw), lambda b, g: (0, g))],
        out_specs=pl.BlockSpec((rows, gw), lambda b, g: (b, g)),
        out_shape=jax.ShapeDtypeStruct((t, n_grp * gw), BF16),
        compiler_params=_params(("arbitrary", "arbitrary")),
        name="pool",
    )(proj, w_pool_l, pool_scale_l.reshape(1, n_grp * gw))


def _mix_kernel(o_ref, pm_ref, x_ref, mod_ref, w_ref, out_ref):
    k = o_ref.shape[1]
    mix = (jnp.dot(o_ref[...], w_ref[0:k, :], preferred_element_type=F32)
           + jnp.dot(pm_ref[...], w_ref[k:, :], preferred_element_type=F32))
    out_ref[...] = x_ref[...] + mod_ref[0][2:3] * mix


def _mix(o, pm, x2, mod, rows_per_mod, w_out, tm=512):
    t, d = x2.shape
    return pl.pallas_call(
        _mix_kernel,
        grid=(t // tm,),
        in_specs=[pl.BlockSpec((tm, o.shape[1]), lambda i: (i, 0)),
                  pl.BlockSpec((tm, pm.shape[1]), lambda i: (i, 0)),
                  pl.BlockSpec((tm, d), lambda i: (i, 0)),
                  pl.BlockSpec((1, N_MOD, d), lambda i: ((i * tm) // rows_per_mod, 0, 0)),
                  pl.BlockSpec(w_out.shape, lambda i: (0, 0))],
        out_specs=pl.BlockSpec((tm, d), lambda i: (i, 0)),
        out_shape=jax.ShapeDtypeStruct((t, d), F32),
        compiler_params=_params(("arbitrary",)),
        name="mix",
    )(o, pm, x2, mod, w_out)


def _ffn_up_kernel(x_ref, xp_ref, xn_ref, mod_ref, nw_ref, wa_ref, wb_ref, cwa_ref, cwb_ref,
                   cba_ref, cbb_ref, o_ref, h_ref, *, tm, seq):
    i = pl.program_id(0)

    @pl.when(pl.program_id(1) == 0)
    def _():
        mod = mod_ref[0]
        nw = nw_ref[...]
        sc, sh = mod[4:5], mod[3:4]
        starts_seq = (i * tm) % seq == 0
        ends_seq = ((i + 1) * tm) % seq == 0
        above = jnp.where(starts_seq, 0.0, _norm_mod(xp_ref[...], nw, sc, sh))
        below = jnp.where(ends_seq, 0.0, _norm_mod(xn_ref[...], nw, sc, sh))
        for r0 in range(0, tm, NORM_ROWS):
            h_ref[r0:r0 + NORM_ROWS, :] = _norm_mod(
                x_ref[r0:r0 + NORM_ROWS, :], nw, sc, sh).astype(BF16)
        h_ref[tm:tm + CONV_HALO, :] = below.astype(BF16)
        h_ref[tm + CONV_HALO:, :] = above.astype(BF16)

    ext = tm + 2 * CONV_HALO
    h = h_ref[...]

    def conv(w_ref, cw_ref, cb_ref):
        u = jnp.dot(h, w_ref[...], preferred_element_type=F32)
        um = u[0:tm]
        up = pltpu.roll(u, 1, 0)[0:tm]
        un = pltpu.roll(u, ext - 1, 0)[0:tm]
        if tm > seq:
            in_seq = lax.broadcasted_iota(jnp.int32, um.shape, 0) % seq
            up = jnp.where(in_seq == 0, 0.0, up)
            un = jnp.where(in_seq == seq - 1, 0.0, un)
        cw = cw_ref[...]
        return up * cw[0:1] + um * cw[1:2] + un * cw[2:3] + cb_ref[...]

    a = conv(wa_ref, cwa_ref, cba_ref)
    b = conv(wb_ref, cwb_ref, cbb_ref)
    o_ref[...] = (_silu(a) * b).astype(o_ref.dtype)


def _ffn_down_kernel(a_ref, x_ref, mod_ref, w_ref, fnw_ref, o_ref, *, final_norm):
    y = x_ref[...] + mod_ref[0][5:6] * jnp.dot(a_ref[...], w_ref[...],
                                               preferred_element_type=F32)
    if final_norm:
        y = y * lax.rsqrt(jnp.mean(y * y, axis=-1, keepdims=True) + EPS) * fnw_ref[...]
    o_ref[...] = y


def _ffn(x2, mod, rows_per_mod, seq, nw, w_up, conv_w, conv_b, w_down, fnw, final_norm,
         tm=1024, tf=512, tm_down=256):
    t, d = x2.shape
    dff = w_down.shape[0]
    nf = dff // tf
    assert seq % tm == 0 or tm % seq == 0
    hb = tm // CONV_HALO
    last_hb = t // CONV_HALO - 1
    cb = conv_b.reshape(1, 2 * dff)
    mod_spec = lambda tile: pl.BlockSpec((1, N_MOD, d),
                                         lambda i, *_: ((i * tile) // rows_per_mod, 0, 0))
    last_tile = t // tm - 1

    def row_tile(i, f):
        return jnp.minimum(i + jnp.minimum(f, 1), last_tile)

    act = pl.pallas_call(
        functools.partial(_ffn_up_kernel, tm=tm, seq=seq),
        grid=(t // tm, nf),
        in_specs=[pl.BlockSpec((tm, d), lambda i, f: (row_tile(i, f), 0)),
                  pl.BlockSpec((CONV_HALO, d),
                               lambda i, f: (jnp.maximum(row_tile(i, f) * hb - 1, 0), 0)),
                  pl.BlockSpec((CONV_HALO, d),
                               lambda i, f: (jnp.minimum((row_tile(i, f) + 1) * hb, last_hb), 0)),
                  pl.BlockSpec((1, N_MOD, d),
                               lambda i, f: ((row_tile(i, f) * tm) // rows_per_mod, 0, 0)),
                  pl.BlockSpec((1, d), lambda i, f: (0, 0)),
                  pl.BlockSpec((d, tf), lambda i, f: (0, f)),
                  pl.BlockSpec((d, tf), lambda i, f: (0, f + nf)),
                  pl.BlockSpec((3, tf), lambda i, f: (0, f)),
                  pl.BlockSpec((3, tf), lambda i, f: (0, f + nf)),
                  pl.BlockSpec((1, tf), lambda i, f: (0, f)),
                  pl.BlockSpec((1, tf), lambda i, f: (0, f + nf))],
        out_specs=pl.BlockSpec((tm, tf), lambda i, f: (i, f)),
        out_shape=jax.ShapeDtypeStruct((t, dff), BF16),
        scratch_shapes=[pltpu.VMEM((tm + 2 * CONV_HALO, d), BF16)],
        compiler_params=_params(("arbitrary", "arbitrary")),
        name="ffn_up",
    )(x2, x2, x2, mod, nw.reshape(1, d), w_up, w_up, conv_w, conv_w, cb, cb)
    return pl.pallas_call(
        functools.partial(_ffn_down_kernel, final_norm=final_norm),
        grid=(t // tm_down,),
        in_specs=[pl.BlockSpec((tm_down, dff), lambda i: (i, 0)),
                  pl.BlockSpec((tm_down, d), lambda i: (i, 0)),
                  mod_spec(tm_down),
                  pl.BlockSpec((dff, d), lambda i: (0, 0), pipeline_mode=pl.Buffered(1)),
                  pl.BlockSpec((1, d), lambda i: (0, 0))],
        out_specs=pl.BlockSpec((tm_down, d), lambda i: (i, 0)),
        out_shape=jax.ShapeDtypeStruct((t, d), F32),
        compiler_params=_params(("arbitrary",)),
        name="ffn_down",
    )(act, x2, mod, w_down, fnw.reshape(1, d))


def kernel(x_prompt, x_sample, state_hgrn, c, c_ctx, w_ada, b_ada, norm1_w, w_in, lb_param,
           hg_norm_w, w_pool, pool_scale, w_out, norm2_w, w_up, conv_w, conv_b, w_down,
           final_norm_w):
    depth = w_in.shape[0]
    bp, lp, d = x_prompt.shape
    bs, ls, _ = x_sample.shape
    hg_width = lb_param.shape[2]

    lb_all = jnp.cumsum(jax.nn.softmax(lb_param.astype(F32), axis=1), axis=1)
    n_c = 1 + bs
    pad = (-n_c) % 8
    cvecs = jnp.concatenate([c_ctx[None, :], c, jnp.zeros((pad, d), F32)], axis=0)

    xp = x_prompt.reshape(bp * lp, d)
    xs = x_sample.reshape(bs * ls, d)
    ctx_states = []
    for l in range(depth):
        last = l == depth - 1
        mods = _ada(cvecs, w_ada[l], b_ada[l])
        mod_p = mods[0:1].reshape(1, N_MOD, d)
        mod_s = mods[1:n_c].reshape(bs, N_MOD, d)
        w_in_l = w_in[l].astype(BF16)
        w_pool_l = w_pool[l].astype(BF16)
        w_out_l = w_out[l].astype(BF16)
        w_up_l = w_up[l].astype(BF16)
        w_down_l = w_down[l].astype(BF16)
        lb = lb_all[:, l]

        def layer(x2, mod, rows_per_mod, batch, seq, s0, on_grid, want_state):
            proj = _in_proj(x2, mod, rows_per_mod, norm1_w[l], w_in_l)
            o, st = _scan(proj, lb, hg_norm_w[l], s0, l, batch, seq, want_state)
            pm = _pool(proj, w_pool_l, pool_scale[l], 5 * hg_width, batch, seq, on_grid)
            x2 = _mix(o, pm, x2, mod, rows_per_mod, w_out_l)
            x2 = _ffn(x2, mod, rows_per_mod, seq, norm2_w[l], w_up_l, conv_w[l], conv_b[l],
                      w_down_l, final_norm_w, last)
            return x2, st

        xp, st = layer(xp, mod_p, bp * lp, bp, lp, None, False, True)
        ctx_states.append(st)
        xs, _ = layer(xs, mod_s, ls, bs, ls, state_hgrn, True, False)

    y_prompt = xp.reshape(bp, lp, d)
    y_sample = xs.reshape(bs, ls, d)
    new_state = jnp.stack(ctx_states, axis=1)
    return (y_prompt, y_sample, new_state)
```

```python
import functools

import jax
import jax.numpy as jnp
import numpy as np
from jax import lax
from jax.experimental import pallas as pl
from jax.experimental.pallas import tpu as pltpu

F32 = jnp.float32
BF16 = jnp.bfloat16
EPS = 1e-6

HEAD_DIM = 128
N_MOD = 6
POOL_WINDOWS = (2, 4, 8, 16)
GRID_W = 64
POOL_SEQ_PER_STEP = 8
CONV_HALO = 8
V7X_VMEM_BYTES = 64 * 1024 * 1024
VMEM_LIMIT = V7X_VMEM_BYTES - 8 * 1024 * 1024


def _params(sem):
    return pltpu.CompilerParams(dimension_semantics=sem, vmem_limit_bytes=VMEM_LIMIT)


def _silu(x):
    hx = 0.5 * x
    return hx + hx * jnp.tanh(hx)


def _norm_mod(x, nw, sc, sh):
    scale = nw * (1.0 + sc)
    return x * lax.rsqrt(jnp.mean(x * x, axis=-1, keepdims=True) + EPS) * scale + sh


def _ada_kernel(c_ref, w_ref, b_ref, o_ref):
    s = _silu(c_ref[...]).astype(BF16)
    o_ref[...] = jnp.dot(s, w_ref[...].astype(BF16), preferred_element_type=F32) + b_ref[...]


def _ada(cvecs, w_ada, b_ada, tn=1024):
    r, d = cvecs.shape
    n = w_ada.shape[1]
    return pl.pallas_call(
        _ada_kernel,
        grid=(n // tn,),
        in_specs=[pl.BlockSpec((r, d), lambda j: (0, 0)),
                  pl.BlockSpec((d, tn), lambda j: (0, j)),
                  pl.BlockSpec((1, tn), lambda j: (0, j))],
        out_specs=pl.BlockSpec((r, tn), lambda j: (0, j)),
        out_shape=jax.ShapeDtypeStruct((r, n), F32),
        compiler_params=_params(("arbitrary",)),
        name="ada",
    )(cvecs, w_ada, b_ada.reshape(1, n))


def _in_proj_kernel(x_ref, mod_ref, nw_ref, w_ref, o_ref):
    mod = mod_ref[0]
    h = _norm_mod(x_ref[...], nw_ref[...], mod[1:2], mod[0:1]).astype(BF16)
    o_ref[...] = jnp.dot(h, w_ref[...], preferred_element_type=F32)


def _in_proj(x2, mod, rows_per_mod, nw, w, tm=256):
    t, d = x2.shape
    n = w.shape[1]
    return pl.pallas_call(
        _in_proj_kernel,
        grid=(t // tm,),
        in_specs=[pl.BlockSpec((tm, d), lambda i: (i, 0)),
                  pl.BlockSpec((1, N_MOD, d), lambda i: ((i * tm) // rows_per_mod, 0, 0)),
                  pl.BlockSpec((1, d), lambda i: (0, 0)),
                  pl.BlockSpec((d, n), lambda i: (0, 0), pipeline_mode=pl.Buffered(1))],
        out_specs=pl.BlockSpec((tm, n), lambda i: (i, 0)),
        out_shape=jax.ShapeDtypeStruct((t, n), F32),
        compiler_params=_params(("arbitrary",)),
        name="in_proj",
    )(x2, mod, nw.reshape(1, d), w)


LOG2E = 1.4426950408889634
TOTAL_ROWS = 16
SHORT_SEQ_PER_STEP = 4
SCAN_INTERLEAVE = 2


def _decay_matrices(c):
    t = np.arange(c)
    blocks = []
    m = 2
    while m < c:
        mid = (t // (2 * m)) * (2 * m) + m
        second = (t & m) != 0
        qm = np.zeros((c, 2 * c), np.float32)
        km = np.zeros((c, 2 * c), np.float32)
        for i in range(c):
            if second[i]:
                qm[i, mid[i]:i + 1] = 1
                km[i, c + mid[i]:c + i] = 1
            else:
                qm[i, c + i:c + mid[i]] = 1
                km[i, i + 1:mid[i]] = 1
        blocks += [qm, km]
        m *= 2
    qf = np.zeros((c, 2 * c), np.float32)
    qb = np.zeros((c, 2 * c), np.float32)
    kf = np.zeros((c, 2 * c), np.float32)
    kb = np.zeros((c, c), np.float32)
    for i in range(c):
        qf[i, 0:i + 1] = 1
        qb[i, c + i:2 * c] = 1
        kf[i, i + 1:c] = 1
        kb[i, 0:i] = 1
    tf = np.zeros((TOTAL_ROWS, 2 * c), np.float32)
    tf[:, 0:c] = 1
    tb = np.ones((TOTAL_ROWS, c), np.float32)
    sweep = np.concatenate(blocks + [qf, qb, kf, tf], axis=0)
    back = np.concatenate([kb, tb], axis=0)
    return (jnp.asarray(np.concatenate([sweep, sweep], axis=1), BF16),
            jnp.asarray(np.concatenate([back, back], axis=1), BF16))


def _gates(z, lbd):
    f = lbd + (1.0 - lbd) * jax.nn.sigmoid(z)
    return jnp.log(f) * LOG2E, 1.0 - f


def _split_bf16(x):
    hi = x.astype(BF16)
    return hi, (x - hi.astype(F32)).astype(BF16)


def _dot_nt(a, b):
    return lax.dot_general(a, b, (((1,), (1,)), ((), ())), preferred_element_type=F32)


def _dot_tn(a, b):
    return lax.dot_general(a, b, (((0,), (0,)), ((), ())), preferred_element_type=F32)


def _scan_kernel(*refs, seq, spb, c, grp, grp_b, has_s0, want_state):
    q_ref, zf_ref, zb_ref, v_ref, g_ref, lb_ref, nw_ref, ms_ref, mb_ref = refs[:9]
    pos = 9
    s0_ref = None
    if has_s0:
        s0_ref = refs[pos]
        pos += 1
    o_ref = refs[pos]
    pos += 1
    sout_ref = None
    if want_state:
        sout_ref = refs[pos]
        pos += 1
    lfbh_scr, lfbl_scr, kb_scr, sb_scr, e_scr, qs_scr, kf_scr, gate_scr = refs[pos:pos + 8]

    gc = grp * c
    n_iters = seq // gc
    n_levels = c.bit_length() - 1
    base = 2 * (n_levels - 1) * c
    lb_f = lb_ref[0:1, :]
    lb_b = lb_ref[1:2, :]
    nw = nw_ref[...]
    row = lax.broadcasted_iota(jnp.int32, (c, HEAD_DIM), 0)
    pair = (lax.broadcasted_iota(jnp.int32, (c, c), 0)
            ^ lax.broadcasted_iota(jnp.int32, (c, c), 1))
    level = jnp.full((c, c), -1, jnp.int32)
    for k in range(n_levels):
        level = level + (pair >= (1 << k)).astype(jnp.int32)

    def initial_state(sq, direction):
        if has_s0:
            return s0_ref[sq, direction].T
        return jnp.zeros((HEAD_DIM, HEAD_DIM), F32)

    def chunk_rows(x, gi):
        return x[gi * c:(gi + 1) * c]

    def chunk_lanes(x, gi):
        return x[:, gi * HEAD_DIM:(gi + 1) * HEAD_DIM]

    def stack_chunks(parts, n):
        return jnp.concatenate(
            [jnp.concatenate([chunk_rows(p, gi) for p in parts], axis=0) for gi in range(n)],
            axis=1)

    gcb = grp_b * c

    def sweep_b(i, stb, sq):
        it = (sq + 1) * (seq // gcb) - 1 - i
        rows = pl.ds(pl.multiple_of(it * gcb, gcb), gcb)
        l2f_b, k_b = _gates(zb_ref[rows, :], lb_b)
        bh, bl = _split_bf16(l2f_b)
        lfbh_scr[rows, :] = bh
        lfbl_scr[rows, :] = bl
        kb_scr[rows, :] = k_b
        v_bf = v_ref[rows, :].astype(BF16)
        e = jnp.exp2(jnp.dot(mb_ref[...], stack_chunks((bh, bl), grp_b),
                             preferred_element_type=F32))
        kv = []
        for gi in range(grp_b):
            k_hat = (chunk_rows(k_b, gi) * chunk_lanes(e, gi)[0:c]).astype(BF16)
            kv.append(_dot_tn(chunk_rows(v_bf, gi), k_hat))
        for gi in reversed(range(grp_b)):
            sb_scr[it * grp_b + gi] = stb.astype(BF16)
            stb = stb * chunk_lanes(e, gi)[c:c + 1] + kv[gi]
        return stb

    stb_final = [lax.fori_loop(0, seq // gcb, functools.partial(sweep_b, sq=sq),
                               initial_state(sq, 1), unroll=True) for sq in range(spb)]

    def block_rows(blk):
        return pl.ds(pl.multiple_of(blk * gc, gc), gc)

    def stage_a_gates(blk):
        rows = block_rows(blk)
        qs = _silu(q_ref[rows, :])
        gate = _silu(g_ref[rows, :])
        l2f_f, k_f = _gates(zf_ref[rows, :], lb_f)
        fh, fl = _split_bf16(l2f_f)
        bh, bl = lfbh_scr[rows, :], lfbl_scr[rows, :]
        return qs, gate, k_f, stack_chunks((fh, bh, fl, bl), grp)

    def stage_a_exponents(slot, vals, chunks):
        lanes = slice(chunks[0] * HEAD_DIM, (chunks[-1] + 1) * HEAD_DIM)
        e_scr[slot, :, lanes] = jnp.exp2(jnp.dot(ms_ref[...], vals[3][:, lanes],
                                                 preferred_element_type=F32))

    def stage_a_store(slot, vals):
        qs_scr[slot], gate_scr[slot], kf_scr[slot] = vals[0], vals[1], vals[2]

    def stage_b_levels(blk, slot, chunks):
        rows = block_rows(blk)
        vbf_all = v_ref[rows, :].astype(BF16)
        kb_all = kb_scr[rows, :]
        scores, kv, dec = [], [], []
        for gi in chunks:
            lanes = slice(gi * HEAD_DIM, (gi + 1) * HEAD_DIM)
            crow = slice(gi * c, (gi + 1) * c)
            qs = qs_scr[slot, crow, :]
            k_f = kf_scr[slot, crow, :]
            k_b = chunk_rows(kb_all, gi)
            odd = (row & 1) != 0
            x = (qs * (1.0 - jnp.where(odd, k_f, k_b))).astype(BF16)
            y = jnp.where(odd, k_b, k_f).astype(BF16)
            a = jnp.where(level == 0, _dot_nt(x, y), 0.0)
            for lv in range(1, n_levels):
                second = (row & (1 << lv)) != 0
                r0 = 2 * (lv - 1) * c
                x = (qs * e_scr[slot, r0:r0 + c, lanes]).astype(BF16)
                y = (jnp.where(second, k_b, k_f)
                     * e_scr[slot, r0 + c:r0 + 2 * c, lanes]).astype(BF16)
                a = jnp.where(level == lv, _dot_nt(x, y), a)
            scores.append(a.astype(BF16))
            k_hat = (k_f * e_scr[slot, base + 2 * c:base + 3 * c, lanes]).astype(BF16)
            kv.append(_dot_tn(chunk_rows(vbf_all, gi), k_hat))
            dec.append(e_scr[slot, base + 3 * c:base + 3 * c + 1, lanes])
        return scores, kv, dec

    def stage_b_finish(blk, slot, stf, lv_out):
        scores, kv, dec = lv_out
        rows = block_rows(blk)
        v_all = v_ref[rows, :]
        vbf_all = v_all.astype(BF16)
        kb_all = kb_scr[rows, :]
        states = []
        for gi in range(grp):
            states.append(stf)
            stf = stf * dec[gi] + kv[gi]
        outs = []
        for gi in range(grp):
            lanes = slice(gi * HEAD_DIM, (gi + 1) * HEAD_DIM)
            crow = slice(gi * c, (gi + 1) * c)
            qs = qs_scr[slot, crow, :]
            q_fb = jnp.concatenate([qs * e_scr[slot, base:base + c, lanes],
                                    qs * e_scr[slot, base + c:base + 2 * c, lanes]],
                                   axis=1).astype(BF16)
            st_fb = jnp.concatenate([states[gi].astype(BF16), sb_scr[blk * grp + gi]], axis=1)
            o = jnp.dot(scores[gi], chunk_rows(vbf_all, gi), preferred_element_type=F32)
            o = o + jnp.sum(qs * (kf_scr[slot, crow, :] + chunk_rows(kb_all, gi)),
                            axis=-1, keepdims=True) * chunk_rows(v_all, gi)
            o = o + _dot_nt(q_fb, st_fb)
            o = o * lax.rsqrt(jnp.mean(o * o, axis=-1, keepdims=True) + EPS) * nw
            outs.append(o * gate_scr[slot, crow, :])
        o_ref[rows, :] = jnp.concatenate(outs, axis=0).astype(o_ref.dtype)
        return stf

    def step(cur, cur_slot, nxt, stf):
        ahead = stage_a_gates(nxt)
        stage_a_store(1 - cur_slot, ahead)
        lv_out = [[], [], []]
        for part in range(0, grp, SCAN_INTERLEAVE):
            chunks = range(part, min(part + SCAN_INTERLEAVE, grp))
            stage_a_exponents(1 - cur_slot, ahead, chunks)
            for acc, new in zip(lv_out, stage_b_levels(cur, cur_slot, chunks)):
                acc.extend(new)
        return stage_b_finish(cur, cur_slot, stf, lv_out)

    if n_iters == 1:
        ahead = [stage_a_gates(sq) for sq in range(spb)]
        for sq in range(spb):
            stage_a_store(sq, ahead[sq])
        stage_a_exponents(0, ahead[0], range(grp))
        lv_out = []
        for sq in range(spb):
            if sq + 1 < spb:
                stage_a_exponents(sq + 1, ahead[sq + 1], range(grp))
            lv_out.append(stage_b_levels(sq, sq, range(grp)))
        stf_final = [stage_b_finish(sq, sq, initial_state(sq, 0), lv_out[sq])
                     for sq in range(spb)]
    else:
        assert n_iters % 2 == 0 and spb == 1
        first = stage_a_gates(0)
        stage_a_store(0, first)
        stage_a_exponents(0, first, range(grp))

        def sweep_f(j, stf):
            stf = step(2 * j, 0, 2 * j + 1, stf)
            return step(2 * j + 1, 1, jnp.minimum(2 * j + 2, n_iters - 1), stf)

        stf_final = [lax.fori_loop(0, n_iters // 2, sweep_f, initial_state(0, 0))]

    if want_state:
        for sq in range(spb):
            sout_ref[sq, 0] = stf_final[sq].T
            sout_ref[sq, 1] = stb_final[sq].T


def _scan(proj, lb, hg_nw, s0, layer, batch, seq, want_state, c=64, grp=8, grp_b=8):
    t = proj.shape[0]
    n_heads = lb.shape[1] // HEAD_DIM
    has_s0 = s0 is not None
    grp = min(grp, seq // c)
    grp_b = min(grp_b, seq // c)
    spb = min(SHORT_SEQ_PER_STEP, batch) if seq == grp * c else 1
    assert batch % spb == 0
    rows = spb * seq
    m_sweep, m_back = _decay_matrices(c)

    def sec(k):
        return pl.BlockSpec((rows, HEAD_DIM), lambda b, h, k=k: (b, k * n_heads + h))

    state_block = (spb, None, 2, None, HEAD_DIM, HEAD_DIM)
    state_map = lambda b, h: (b, layer, 0, h, 0, 0)
    in_specs = [sec(0), sec(1), sec(2), sec(3), sec(4),
                pl.BlockSpec((2, HEAD_DIM), lambda b, h: (0, h)),
                pl.BlockSpec((1, HEAD_DIM), lambda b, h: (0, 0)),
                pl.BlockSpec(m_sweep.shape, lambda b, h: (0, 0)),
                pl.BlockSpec(m_back.shape, lambda b, h: (0, 0))]
    args = [proj, proj, proj, proj, proj, lb, hg_nw.reshape(1, HEAD_DIM), m_sweep, m_back]
    if has_s0:
        in_specs.append(pl.BlockSpec(state_block, state_map))
        args.append(s0)
    out_specs = [pl.BlockSpec((rows, HEAD_DIM), lambda b, h: (b, h))]
    out_shape = [jax.ShapeDtypeStruct((t, n_heads * HEAD_DIM), BF16)]
    if want_state:
        out_specs.append(pl.BlockSpec((spb, 2, None, HEAD_DIM, HEAD_DIM),
                                      lambda b, h: (b, 0, h, 0, 0)))
        out_shape.append(jax.ShapeDtypeStruct((batch, 2, n_heads, HEAD_DIM, HEAD_DIM), F32))
    gc = grp * c
    slots = max(2, spb)
    res = pl.pallas_call(
        functools.partial(_scan_kernel, seq=seq, spb=spb, c=c, grp=grp, grp_b=grp_b,
                          has_s0=has_s0, want_state=want_state),
        grid=(batch // spb, n_heads),
        in_specs=in_specs,
        out_specs=out_specs,
        out_shape=out_shape,
        scratch_shapes=[pltpu.VMEM((rows, HEAD_DIM), BF16),
                        pltpu.VMEM((rows, HEAD_DIM), BF16),
                        pltpu.VMEM((rows, HEAD_DIM), F32),
                        pltpu.VMEM((rows // c, HEAD_DIM, HEAD_DIM), BF16),
                        pltpu.VMEM((slots, m_sweep.shape[0], grp * HEAD_DIM), F32),
                        pltpu.VMEM((slots, gc, HEAD_DIM), F32),
                        pltpu.VMEM((slots, gc, HEAD_DIM), F32),
                        pltpu.VMEM((slots, gc, HEAD_DIM), F32)],
        compiler_params=_params(("arbitrary", "arbitrary")),
        name="scan",
    )(*args)
    return (res[0], res[1]) if want_state else (res[0], None)


def _tile_rows(small, nb):
    n, w = small.shape
    return jnp.broadcast_to(small[None], (nb, n, w)).reshape(nb * n, w)


def _window_counts(pos, w, n):
    half = w // 2
    return (jnp.minimum(pos + (w - half), n) - jnp.maximum(pos - half, 0)).astype(F32)


def _window_sum_rows(x, w, n, nb):
    rows, width = x.shape
    half = w // 2
    pos = lax.broadcasted_iota(jnp.int32, (n, width), 0)

    def keep(cond):
        return _tile_rows(cond.astype(F32), nb)

    fwd = x
    ln = 1
    while ln < half:
        fwd = fwd + pltpu.roll(fwd, rows - ln, 0) * keep(pos < n - ln)
        ln *= 2
    bwd = pltpu.roll(x, 1, 0) * keep(pos >= 1)
    ln = 1
    while ln < half:
        bwd = bwd + pltpu.roll(bwd, ln, 0) * keep(pos >= ln)
        ln *= 2
    return fwd + bwd


def _window_sum_bands(x3, w):
    half = w // 2

    def later(a, k):
        return jnp.concatenate([a[k:], jnp.zeros((k,) + a.shape[1:], a.dtype)], axis=0)

    def earlier(a, k):
        return jnp.concatenate([jnp.zeros((k,) + a.shape[1:], a.dtype), a[:-k]], axis=0)

    fwd = x3
    ln = 1
    while ln < half:
        fwd = fwd + later(fwd, ln)
        ln *= 2
    bwd = earlier(x3, 1)
    ln = 1
    while ln < half:
        bwd = bwd + earlier(bwd, ln)
        ln *= 2
    return fwd + bwd


def _pool_kernel(p_ref, w_ref, sc_ref, o_ref, *, seq, on_grid):
    grp = pl.program_id(1)
    rows, width = p_ref.shape
    for gi, w in enumerate(POOL_WINDOWS):
        @pl.when(grp == gi)
        def _(w=w):
            pg = p_ref[...]
            if on_grid:
                nb = rows // GRID_W
                pos_c = lax.broadcasted_iota(jnp.int32, (GRID_W, width), 0)
                pos_r = lax.broadcasted_iota(jnp.int32, (nb, 1, width), 0)
                m = _window_sum_rows(pg, w, GRID_W, nb)
                m = m * _tile_rows(1.0 / _window_counts(pos_c, w, GRID_W), nb)
                m3 = _window_sum_bands(m.reshape(nb, GRID_W, width), w)
                m = (m3 * (1.0 / _window_counts(pos_r, w, nb))).reshape(rows, width)
            else:
                nb = rows // seq
                pos = lax.broadcasted_iota(jnp.int32, (seq, width), 0)
                m = _window_sum_rows(pg, w, seq, nb)
                m = m * _tile_rows(1.0 / _window_counts(pos, w, seq), nb)
            d = (m - pg).astype(BF16)
            y = jnp.dot(d, w_ref[...], preferred_element_type=F32) * sc_ref[...]
            o_ref[...] = y.astype(o_ref.dtype)


def _pool(proj, w_pool_l, pool_scale_l, col0, batch, seq, on_grid):
    t = proj.shape[0]
    n_grp, gw, _ = w_pool_l.shape
    blk0 = col0 // gw
    spb = 1 if on_grid else min(POOL_SEQ_PER_STEP, batch)
    assert batch % spb == 0
    rows = spb * seq
    return pl.pallas_call(
        functools.partial(_pool_kernel, seq=seq, on_grid=on_grid),
        grid=(batch // spb, n_grp),
        in_specs=[pl.BlockSpec((rows, gw), lambda b, g: (b, blk0 + g)),
                  pl.BlockSpec((None, gw, gw), lambda b, g: (g, 0, 0)),
                  pl.BlockSpec((1, gw), lambda b, g: (0, g))],
        out_specs=pl.BlockSpec((rows, gw), lambda b, g: (b, g)),
        out_shape=jax.ShapeDtypeStruct((t, n_grp * gw), BF16),
        compiler_params=_params(("arbitrary", "arbitrary")),
        name="pool",
    )(proj, w_pool_l, pool_scale_l.reshape(1, n_grp * gw))


def _mix_kernel(o_ref, pm_ref, x_ref, mod_ref, w_ref, out_ref):
    k = o_ref.shape[1]
    mix = (jnp.dot(o_ref[...], w_ref[0:k, :], preferred_element_type=F32)
           + jnp.dot(pm_ref[...], w_ref[k:, :], preferred_element_type=F32))
    out_ref[...] = x_ref[...] + mod_ref[0][2:3] * mix


def _mix(o, pm, x2, mod, rows_per_mod, w_out, tm=512):
    t, d = x2.shape
    return pl.pallas_call(
        _mix_kernel,
        grid=(t // tm,),
        in_specs=[pl.BlockSpec((tm, o.shape[1]), lambda i: (i, 0)),
                  pl.BlockSpec((tm, pm.shape[1]), lambda i: (i, 0)),
                  pl.BlockSpec((tm, d), lambda i: (i, 0)),
                  pl.BlockSpec((1, N_MOD, d), lambda i: ((i * tm) // rows_per_mod, 0, 0)),
                  pl.BlockSpec(w_out.shape, lambda i: (0, 0))],
        out_specs=pl.BlockSpec((tm, d), lambda i: (i, 0)),
        out_shape=jax.ShapeDtypeStruct((t, d), F32),
        compiler_params=_params(("arbitrary",)),
        name="mix",
    )(o, pm, x2, mod, w_out)


def _ffn_up_kernel(x_ref, xp_ref, xn_ref, mod_ref, nw_ref, w_ref, cwa_ref, cwb_ref,
                   cba_ref, cbb_ref, o_ref, h_ref, *, tm, seq):
    i = pl.program_id(0)

    @pl.when(pl.program_id(1) == 0)
    def _():
        mod = mod_ref[0]
        nw = nw_ref[...]
        sc, sh = mod[4:5], mod[3:4]
        starts_seq = (i * tm) % seq == 0
        ends_seq = ((i + 1) * tm) % seq == 0
        above = jnp.where(starts_seq, 0.0, _norm_mod(xp_ref[...], nw, sc, sh))
        below = jnp.where(ends_seq, 0.0, _norm_mod(xn_ref[...], nw, sc, sh))
        h_ref[0:CONV_HALO, :] = above.astype(BF16)
        h_ref[CONV_HALO:CONV_HALO + tm, :] = _norm_mod(x_ref[...], nw, sc, sh).astype(BF16)
        h_ref[CONV_HALO + tm:, :] = below.astype(BF16)

    ext = tm + 2 * CONV_HALO
    h = h_ref[...]

    tf = o_ref.shape[1]

    def conv(cols, cw_ref, cb_ref):
        u = jnp.dot(h, w_ref[:, cols], preferred_element_type=F32)
        um = u[CONV_HALO:CONV_HALO + tm]
        up = pltpu.roll(u, 1, 0)[CONV_HALO:CONV_HALO + tm]
        un = pltpu.roll(u, ext - 1, 0)[CONV_HALO:CONV_HALO + tm]
        if tm > seq:
            in_seq = lax.broadcasted_iota(jnp.int32, um.shape, 0) % seq
            up = jnp.where(in_seq == 0, 0.0, up)
            un = jnp.where(in_seq == seq - 1, 0.0, un)
        cw = cw_ref[...]
        return up * cw[0:1] + um * cw[1:2] + un * cw[2:3] + cb_ref[...]

    a = conv(slice(0, tf), cwa_ref, cba_ref)
    b = conv(slice(tf, 2 * tf), cwb_ref, cbb_ref)
    o_ref[...] = (_silu(a) * b).astype(o_ref.dtype)


def _ffn_down_kernel(a_ref, x_ref, mod_ref, w_ref, fnw_ref, o_ref, *, final_norm):
    y = x_ref[...] + mod_ref[0][5:6] * jnp.dot(a_ref[...], w_ref[...],
                                               preferred_element_type=F32)
    if final_norm:
        y = y * lax.rsqrt(jnp.mean(y * y, axis=-1, keepdims=True) + EPS) * fnw_ref[...]
    o_ref[...] = y


def _ffn(x2, mod, rows_per_mod, seq, nw, w_up, conv_w, conv_b, w_down, fnw, final_norm,
         tm=1024, tf=512, tm_down=256):
    t, d = x2.shape
    dff = w_down.shape[0]
    nf = dff // tf
    assert seq % tm == 0 or tm % seq == 0
    hb = tm // CONV_HALO
    last_hb = t // CONV_HALO - 1
    cb = conv_b.reshape(1, 2 * dff)
    w_up_tiles = w_up.reshape(d, 2, nf, tf).transpose(2, 0, 1, 3).reshape(nf, d, 2 * tf)
    mod_spec = lambda tile: pl.BlockSpec((1, N_MOD, d),
                                         lambda i, *_: ((i * tile) // rows_per_mod, 0, 0))
    act = pl.pallas_call(
        functools.partial(_ffn_up_kernel, tm=tm, seq=seq),
        grid=(t // tm, nf),
        in_specs=[pl.BlockSpec((tm, d), lambda i, f: (i, 0)),
                  pl.BlockSpec((CONV_HALO, d), lambda i, f: (jnp.maximum(i * hb - 1, 0), 0)),
                  pl.BlockSpec((CONV_HALO, d), lambda i, f: (jnp.minimum((i + 1) * hb, last_hb), 0)),
                  mod_spec(tm),
                  pl.BlockSpec((1, d), lambda i, f: (0, 0)),
                  pl.BlockSpec((None, d, 2 * tf), lambda i, f: (f, 0, 0)),
                  pl.BlockSpec((3, tf), lambda i, f: (0, f)),
                  pl.BlockSpec((3, tf), lambda i, f: (0, f + nf)),
                  pl.BlockSpec((1, tf), lambda i, f: (0, f)),
                  pl.BlockSpec((1, tf), lambda i, f: (0, f + nf))],
        out_specs=pl.BlockSpec((tm, tf), lambda i, f: (i, f)),
        out_shape=jax.ShapeDtypeStruct((t, dff), BF16),
        scratch_shapes=[pltpu.VMEM((tm + 2 * CONV_HALO, d), BF16)],
        compiler_params=_params(("arbitrary", "arbitrary")),
        name="ffn_up",
    )(x2, x2, x2, mod, nw.reshape(1, d), w_up_tiles, conv_w, conv_w, cb, cb)
    return pl.pallas_call(
        functools.partial(_ffn_down_kernel, final_norm=final_norm),
        grid=(t // tm_down,),
        in_specs=[pl.BlockSpec((tm_down, dff), lambda i: (i, 0)),
                  pl.BlockSpec((tm_down, d), lambda i: (i, 0)),
                  mod_spec(tm_down),
                  pl.BlockSpec((dff, d), lambda i: (0, 0), pipeline_mode=pl.Buffered(1)),
                  pl.BlockSpec((1, d), lambda i: (0, 0))],
        out_specs=pl.BlockSpec((tm_down, d), lambda i: (i, 0)),
        out_shape=jax.ShapeDtypeStruct((t, d), F32),
        compiler_params=_params(("arbitrary",)),
        name="ffn_down",
    )(act, x2, mod, w_down, fnw.reshape(1, d))


def kernel(x_prompt, x_sample, state_hgrn, c, c_ctx, w_ada, b_ada, norm1_w, w_in, lb_param,
           hg_norm_w, w_pool, pool_scale, w_out, norm2_w, w_up, conv_w, conv_b, w_down,
           final_norm_w):
    depth = w_in.shape[0]
    bp, lp, d = x_prompt.shape
    bs, ls, _ = x_sample.shape
    hg_width = lb_param.shape[2]

    lb_all = jnp.cumsum(jax.nn.softmax(lb_param.astype(F32), axis=1), axis=1)
    n_c = 1 + bs
    pad = (-n_c) % 8
    cvecs = jnp.concatenate([c_ctx[None, :], c, jnp.zeros((pad, d), F32)], axis=0)

    xp = x_prompt.reshape(bp * lp, d)
    xs = x_sample.reshape(bs * ls, d)
    ctx_states = []
    for l in range(depth):
        last = l == depth - 1
        mods = _ada(cvecs, w_ada[l], b_ada[l])
        mod_p = mods[0:1].reshape(1, N_MOD, d)
        mod_s = mods[1:n_c].reshape(bs, N_MOD, d)
        w_in_l = w_in[l].astype(BF16)
        w_pool_l = w_pool[l].astype(BF16)
        w_out_l = w_out[l].astype(BF16)
        w_up_l = w_up[l].astype(BF16)
        w_down_l = w_down[l].astype(BF16)
        lb = lb_all[:, l]

        def layer(x2, mod, rows_per_mod, batch, seq, s0, on_grid, want_state):
            proj = _in_proj(x2, mod, rows_per_mod, norm1_w[l], w_in_l)
            o, st = _scan(proj, lb, hg_norm_w[l], s0, l, batch, seq, want_state)
            pm = _pool(proj, w_pool_l, pool_scale[l], 5 * hg_width, batch, seq, on_grid)
            x2 = _mix(o, pm, x2, mod, rows_per_mod, w_out_l)
            x2 = _ffn(x2, mod, rows_per_mod, seq, norm2_w[l], w_up_l, conv_w[l], conv_b[l],
                      w_down_l, final_norm_w, last)
            return x2, st

        xp, st = layer(xp, mod_p, bp * lp, bp, lp, None, False, True)
        ctx_states.append(st)
        xs, _ = layer(xs, mod_s, ls, bs, ls, state_hgrn, True, False)

    y_prompt = xp.reshape(bp, lp, d)
    y_sample = xs.reshape(bs, ls, d)
    new_state = jnp.stack(ctx_states, axis=1)
    return (y_prompt, y_sample, new_state)
```

```python
import functools

import jax
import jax.numpy as jnp
import numpy as np
from jax import lax
from jax.experimental import pallas as pl
from jax.experimental.pallas import tpu as pltpu

F32 = jnp.float32
BF16 = jnp.bfloat16
EPS = 1e-6

HEAD_DIM = 128
N_MOD = 6
POOL_WINDOWS = (2, 4, 8, 16)
GRID_W = 64
POOL_SEQ_PER_STEP = 8
CONV_HALO = 8
V7X_VMEM_BYTES = 64 * 1024 * 1024
VMEM_LIMIT = V7X_VMEM_BYTES - 8 * 1024 * 1024


def _params(sem):
    return pltpu.CompilerParams(dimension_semantics=sem, vmem_limit_bytes=VMEM_LIMIT)


def _silu(x):
    hx = 0.5 * x
    return hx + hx * jnp.tanh(hx)


def _norm_mod(x, nw, sc, sh):
    scale = nw * (1.0 + sc)
    return x * lax.rsqrt(jnp.mean(x * x, axis=-1, keepdims=True) + EPS) * scale + sh


def _ada_kernel(c_ref, w_ref, b_ref, o_ref):
    s = _silu(c_ref[...]).astype(BF16)
    o_ref[...] = jnp.dot(s, w_ref[...].astype(BF16), preferred_element_type=F32) + b_ref[...]


def _ada(cvecs, w_ada, b_ada, tn=1024):
    r, d = cvecs.shape
    n = w_ada.shape[1]
    return pl.pallas_call(
        _ada_kernel,
        grid=(n // tn,),
        in_specs=[pl.BlockSpec((r, d), lambda j: (0, 0)),
                  pl.BlockSpec((d, tn), lambda j: (0, j)),
                  pl.BlockSpec((1, tn), lambda j: (0, j))],
        out_specs=pl.BlockSpec((r, tn), lambda j: (0, j)),
        out_shape=jax.ShapeDtypeStruct((r, n), F32),
        compiler_params=_params(("arbitrary",)),
        name="ada",
    )(cvecs, w_ada, b_ada.reshape(1, n))


def _in_proj_kernel(x_ref, mod_ref, nw_ref, w_ref, o_ref):
    mod = mod_ref[0]
    h = _norm_mod(x_ref[...], nw_ref[...], mod[1:2], mod[0:1]).astype(BF16)
    o_ref[...] = jnp.dot(h, w_ref[...], preferred_element_type=F32)


def _in_proj(x2, mod, rows_per_mod, nw, w, tm=256):
    t, d = x2.shape
    n = w.shape[1]
    return pl.pallas_call(
        _in_proj_kernel,
        grid=(t // tm,),
        in_specs=[pl.BlockSpec((tm, d), lambda i: (i, 0)),
                  pl.BlockSpec((1, N_MOD, d), lambda i: ((i * tm) // rows_per_mod, 0, 0)),
                  pl.BlockSpec((1, d), lambda i: (0, 0)),
                  pl.BlockSpec((d, n), lambda i: (0, 0), pipeline_mode=pl.Buffered(1))],
        out_specs=pl.BlockSpec((tm, n), lambda i: (i, 0)),
        out_shape=jax.ShapeDtypeStruct((t, n), F32),
        compiler_params=_params(("arbitrary",)),
        name="in_proj",
    )(x2, mod, nw.reshape(1, d), w)


LOG2E = 1.4426950408889634
TOTAL_ROWS = 16
SHORT_SEQ_PER_STEP = 4
SCAN_INTERLEAVE = 2


def _decay_matrices(c):
    t = np.arange(c)
    blocks = []
    m = 2
    while m < c:
        mid = (t // (2 * m)) * (2 * m) + m
        second = (t & m) != 0
        qm = np.zeros((c, 2 * c), np.float32)
        km = np.zeros((c, 2 * c), np.float32)
        for i in range(c):
            if second[i]:
                qm[i, mid[i]:i + 1] = 1
                km[i, c + mid[i]:c + i] = 1
            else:
                qm[i, c + i:c + mid[i]] = 1
                km[i, i + 1:mid[i]] = 1
        blocks += [qm, km]
        m *= 2
    qf = np.zeros((c, 2 * c), np.float32)
    qb = np.zeros((c, 2 * c), np.float32)
    kf = np.zeros((c, 2 * c), np.float32)
    kb = np.zeros((c, c), np.float32)
    for i in range(c):
        qf[i, 0:i + 1] = 1
        qb[i, c + i:2 * c] = 1
        kf[i, i + 1:c] = 1
        kb[i, 0:i] = 1
    tf = np.zeros((TOTAL_ROWS, 2 * c), np.float32)
    tf[:, 0:c] = 1
    tb = np.ones((TOTAL_ROWS, c), np.float32)
    sweep = np.concatenate(blocks + [qf, qb, kf, tf], axis=0)
    back = np.concatenate([kb, tb], axis=0)
    return (jnp.asarray(np.concatenate([sweep, sweep], axis=1), BF16),
            jnp.asarray(np.concatenate([back, back], axis=1), BF16))


def _gates(z, lbd):
    f = lbd + (1.0 - lbd) * jax.nn.sigmoid(z)
    return jnp.log(f) * LOG2E, 1.0 - f


def _split_bf16(x):
    hi = x.astype(BF16)
    return hi, (x - hi.astype(F32)).astype(BF16)


def _dot_nt(a, b):
    return lax.dot_general(a, b, (((1,), (1,)), ((), ())), preferred_element_type=F32)


def _dot_tn(a, b):
    return lax.dot_general(a, b, (((0,), (0,)), ((), ())), preferred_element_type=F32)


def _scan_kernel(*refs, seq, spb, c, grp, grp_b, has_s0, want_state):
    q_ref, zf_ref, zb_ref, v_ref, g_ref, lb_ref, nw_ref, ms_ref, mb_ref = refs[:9]
    pos = 9
    s0_ref = None
    if has_s0:
        s0_ref = refs[pos]
        pos += 1
    o_ref = refs[pos]
    pos += 1
    sout_ref = None
    if want_state:
        sout_ref = refs[pos]
        pos += 1
    lfbh_scr, lfbl_scr, kb_scr, sb_scr, e_scr, qs_scr, kf_scr, gate_scr = refs[pos:pos + 8]

    gc = grp * c
    n_iters = seq // gc
    n_levels = c.bit_length() - 1
    base = 2 * (n_levels - 1) * c
    lb_f = lb_ref[0:1, :]
    lb_b = lb_ref[1:2, :]
    nw = nw_ref[...]
    row = lax.broadcasted_iota(jnp.int32, (c, HEAD_DIM), 0)
    pair = (lax.broadcasted_iota(jnp.int32, (c, c), 0)
            ^ lax.broadcasted_iota(jnp.int32, (c, c), 1))
    level = jnp.full((c, c), -1, jnp.int32)
    for k in range(n_levels):
        level = level + (pair >= (1 << k)).astype(jnp.int32)

    def initial_state(sq, direction):
        if has_s0:
            return s0_ref[sq, direction].T
        return jnp.zeros((HEAD_DIM, HEAD_DIM), F32)

    def chunk_rows(x, gi):
        return x[gi * c:(gi + 1) * c]

    def chunk_lanes(x, gi):
        return x[:, gi * HEAD_DIM:(gi + 1) * HEAD_DIM]

    def stack_chunks(parts, n):
        return jnp.concatenate(
            [jnp.concatenate([chunk_rows(p, gi) for p in parts], axis=0) for gi in range(n)],
            axis=1)

    gcb = grp_b * c

    def sweep_b(i, stb, sq):
        it = (sq + 1) * (seq // gcb) - 1 - i
        rows = pl.ds(pl.multiple_of(it * gcb, gcb), gcb)
        l2f_b, k_b = _gates(zb_ref[rows, :], lb_b)
        bh, bl = _split_bf16(l2f_b)
        lfbh_scr[rows, :] = bh
        lfbl_scr[rows, :] = bl
        kb_scr[rows, :] = k_b
        v_bf = v_ref[rows, :].astype(BF16)
        e = jnp.exp2(jnp.dot(mb_ref[...], stack_chunks((bh, bl), grp_b),
                             preferred_element_type=F32))
        kv = []
        for gi in range(grp_b):
            k_hat = (chunk_rows(k_b, gi) * chunk_lanes(e, gi)[0:c]).astype(BF16)
            kv.append(_dot_tn(chunk_rows(v_bf, gi), k_hat))
        for gi in reversed(range(grp_b)):
            sb_scr[it * grp_b + gi] = stb.astype(BF16)
            stb = stb * chunk_lanes(e, gi)[c:c + 1] + kv[gi]
        return stb

    stb_final = [lax.fori_loop(0, seq // gcb, functools.partial(sweep_b, sq=sq),
                               initial_state(sq, 1), unroll=True) for sq in range(spb)]

    def block_rows(blk):
        return pl.ds(pl.multiple_of(blk * gc, gc), gc)

    def stage_a_gates(blk):
        rows = block_rows(blk)
        qs = _silu(q_ref[rows, :])
        gate = _silu(g_ref[rows, :])
        l2f_f, k_f = _gates(zf_ref[rows, :], lb_f)
        fh, fl = _split_bf16(l2f_f)
        bh, bl = lfbh_scr[rows, :], lfbl_scr[rows, :]
        return qs, gate, k_f, stack_chunks((fh, bh, fl, bl), grp)

    def stage_a_exponents(slot, vals, chunks):
        lanes = slice(chunks[0] * HEAD_DIM, (chunks[-1] + 1) * HEAD_DIM)
        e_scr[slot, :, lanes] = jnp.exp2(jnp.dot(ms_ref[...], vals[3][:, lanes],
                                                 preferred_element_type=F32))

    def stage_a_store(slot, vals):
        qs_scr[slot], gate_scr[slot], kf_scr[slot] = vals[0], vals[1], vals[2]

    def stage_b_levels(blk, slot, chunks):
        rows = block_rows(blk)
        vbf_all = v_ref[rows, :].astype(BF16)
        kb_all = kb_scr[rows, :]
        scores, kv, dec = [], [], []
        for gi in chunks:
            lanes = slice(gi * HEAD_DIM, (gi + 1) * HEAD_DIM)
            crow = slice(gi * c, (gi + 1) * c)
            qs = qs_scr[slot, crow, :]
            k_f = kf_scr[slot, crow, :]
            k_b = chunk_rows(kb_all, gi)
            odd = (row & 1) != 0
            x = (qs * (1.0 - jnp.where(odd, k_f, k_b))).astype(BF16)
            y = jnp.where(odd, k_b, k_f).astype(BF16)
            a = jnp.where(level == 0, _dot_nt(x, y), 0.0)
            for lv in range(1, n_levels):
                second = (row & (1 << lv)) != 0
                r0 = 2 * (lv - 1) * c
                x = (qs * e_scr[slot, r0:r0 + c, lanes]).astype(BF16)
                y = (jnp.where(second, k_b, k_f)
                     * e_scr[slot, r0 + c:r0 + 2 * c, lanes]).astype(BF16)
                a = jnp.where(level == lv, _dot_nt(x, y), a)
            scores.append(a.astype(BF16))
            k_hat = (k_f * e_scr[slot, base + 2 * c:base + 3 * c, lanes]).astype(BF16)
            kv.append(_dot_tn(chunk_rows(vbf_all, gi), k_hat))
            dec.append(e_scr[slot, base + 3 * c:base + 3 * c + 1, lanes])
        return scores, kv, dec

    def stage_b_finish(blk, slot, stf, lv_out):
        scores, kv, dec = lv_out
        rows = block_rows(blk)
        v_all = v_ref[rows, :]
        vbf_all = v_all.astype(BF16)
        kb_all = kb_scr[rows, :]
        states = []
        for gi in range(grp):
            states.append(stf)
            stf = stf * dec[gi] + kv[gi]
        outs = []
        for gi in range(grp):
            lanes = slice(gi * HEAD_DIM, (gi + 1) * HEAD_DIM)
            crow = slice(gi * c, (gi + 1) * c)
            qs = qs_scr[slot, crow, :]
            q_fb = jnp.concatenate([qs * e_scr[slot, base:base + c, lanes],
                                    qs * e_scr[slot, base + c:base + 2 * c, lanes]],
                                   axis=1).astype(BF16)
            st_fb = jnp.concatenate([states[gi].astype(BF16), sb_scr[blk * grp + gi]], axis=1)
            o = jnp.dot(scores[gi], chunk_rows(vbf_all, gi), preferred_element_type=F32)
            o = o + jnp.sum(qs * (kf_scr[slot, crow, :] + chunk_rows(kb_all, gi)),
                            axis=-1, keepdims=True) * chunk_rows(v_all, gi)
            o = o + _dot_nt(q_fb, st_fb)
            o = o * lax.rsqrt(jnp.mean(o * o, axis=-1, keepdims=True) + EPS) * nw
            outs.append(o * gate_scr[slot, crow, :])
        o_ref[rows, :] = jnp.concatenate(outs, axis=0).astype(o_ref.dtype)
        return stf

    def step(cur, cur_slot, nxt, stf):
        ahead = stage_a_gates(nxt)
        stage_a_store(1 - cur_slot, ahead)
        lv_out = [[], [], []]
        for part in range(0, grp, SCAN_INTERLEAVE):
            chunks = range(part, min(part + SCAN_INTERLEAVE, grp))
            stage_a_exponents(1 - cur_slot, ahead, chunks)
            for acc, new in zip(lv_out, stage_b_levels(cur, cur_slot, chunks)):
                acc.extend(new)
        return stage_b_finish(cur, cur_slot, stf, lv_out)

    if n_iters == 1:
        ahead = [stage_a_gates(sq) for sq in range(spb)]
        for sq in range(spb):
            stage_a_store(sq, ahead[sq])
        stage_a_exponents(0, ahead[0], range(grp))
        lv_out = []
        for sq in range(spb):
            if sq + 1 < spb:
                stage_a_exponents(sq + 1, ahead[sq + 1], range(grp))
            lv_out.append(stage_b_levels(sq, sq, range(grp)))
        stf_final = [stage_b_finish(sq, sq, initial_state(sq, 0), lv_out[sq])
                     for sq in range(spb)]
    else:
        assert n_iters % 2 == 0 and spb == 1
        first = stage_a_gates(0)
        stage_a_store(0, first)
        stage_a_exponents(0, first, range(grp))

        def sweep_f(j, stf):
            stf = step(2 * j, 0, 2 * j + 1, stf)
            return step(2 * j + 1, 1, jnp.minimum(2 * j + 2, n_iters - 1), stf)

        stf_final = [lax.fori_loop(0, n_iters // 2, sweep_f, initial_state(0, 0))]

    if want_state:
        for sq in range(spb):
            sout_ref[sq, 0] = stf_final[sq].T
            sout_ref[sq, 1] = stb_final[sq].T


def _scan(proj, lb, hg_nw, s0, layer, batch, seq, want_state, c=64, grp=8, grp_b=8):
    t = proj.shape[0]
    n_heads = lb.shape[1] // HEAD_DIM
    has_s0 = s0 is not None
    grp = min(grp, seq // c)
    grp_b = min(grp_b, seq // c)
    spb = min(SHORT_SEQ_PER_STEP, batch) if seq == grp * c else 1
    assert batch % spb == 0
    rows = spb * seq
    m_sweep, m_back = _decay_matrices(c)

    def sec(k):
        return pl.BlockSpec((rows, HEAD_DIM), lambda b, h, k=k: (b, k * n_heads + h))

    state_block = (spb, None, 2, None, HEAD_DIM, HEAD_DIM)
    state_map = lambda b, h: (b, layer, 0, h, 0, 0)
    in_specs = [sec(0), sec(1), sec(2), sec(3), sec(4),
                pl.BlockSpec((2, HEAD_DIM), lambda b, h: (0, h)),
                pl.BlockSpec((1, HEAD_DIM), lambda b, h: (0, 0)),
                pl.BlockSpec(m_sweep.shape, lambda b, h: (0, 0)),
                pl.BlockSpec(m_back.shape, lambda b, h: (0, 0))]
    args = [proj, proj, proj, proj, proj, lb, hg_nw.reshape(1, HEAD_DIM), m_sweep, m_back]
    if has_s0:
        in_specs.append(pl.BlockSpec(state_block, state_map))
        args.append(s0)
    out_specs = [pl.BlockSpec((rows, HEAD_DIM), lambda b, h: (b, h))]
    out_shape = [jax.ShapeDtypeStruct((t, n_heads * HEAD_DIM), BF16)]
    if want_state:
        out_specs.append(pl.BlockSpec((spb, 2, None, HEAD_DIM, HEAD_DIM),
                                      lambda b, h: (b, 0, h, 0, 0)))
        out_shape.append(jax.ShapeDtypeStruct((batch, 2, n_heads, HEAD_DIM, HEAD_DIM), F32))
    gc = grp * c
    slots = max(2, spb)
    res = pl.pallas_call(
        functools.partial(_scan_kernel, seq=seq, spb=spb, c=c, grp=grp, grp_b=grp_b,
                          has_s0=has_s0, want_state=want_state),
        grid=(batch // spb, n_heads),
        in_specs=in_specs,
        out_specs=out_specs,
        out_shape=out_shape,
        scratch_shapes=[pltpu.VMEM((rows, HEAD_DIM), BF16),
                        pltpu.VMEM((rows, HEAD_DIM), BF16),
                        pltpu.VMEM((rows, HEAD_DIM), F32),
                        pltpu.VMEM((rows // c, HEAD_DIM, HEAD_DIM), BF16),
                        pltpu.VMEM((slots, m_sweep.shape[0], grp * HEAD_DIM), F32),
                        pltpu.VMEM((slots, gc, HEAD_DIM), F32),
                        pltpu.VMEM((slots, gc, HEAD_DIM), F32),
                        pltpu.VMEM((slots, gc, HEAD_DIM), F32)],
        compiler_params=_params(("arbitrary", "arbitrary")),
        name="scan",
    )(*args)
    return (res[0], res[1]) if want_state else (res[0], None)


def _tile_rows(small, nb):
    n, w = small.shape
    return jnp.broadcast_to(small[None], (nb, n, w)).reshape(nb * n, w)


def _window_counts(pos, w, n):
    half = w // 2
    return (jnp.minimum(pos + (w - half), n) - jnp.maximum(pos - half, 0)).astype(F32)


def _window_band(w, n):
    i = np.arange(n)[:, None]
    j = np.arange(n)[None, :]
    band = ((j >= i - w // 2) & (j < i + w - w // 2)).astype(np.float32)
    return np.concatenate([band, band], axis=1)


def _band_window_sums(x, band2, n, nb):
    hi, lo = _split_bf16(x)
    outs = []
    for b in range(nb):
        r = slice(b * n, (b + 1) * n)
        outs.append(jnp.dot(band2, jnp.concatenate([hi[r], lo[r]], axis=0),
                            preferred_element_type=F32))
    return jnp.concatenate(outs, axis=0)


def _window_sum_bands(x3, w):
    half = w // 2

    def later(a, k):
        return jnp.concatenate([a[k:], jnp.zeros((k,) + a.shape[1:], a.dtype)], axis=0)

    def earlier(a, k):
        return jnp.concatenate([jnp.zeros((k,) + a.shape[1:], a.dtype), a[:-k]], axis=0)

    fwd = x3
    ln = 1
    while ln < half:
        fwd = fwd + later(fwd, ln)
        ln *= 2
    bwd = earlier(x3, 1)
    ln = 1
    while ln < half:
        bwd = bwd + earlier(bwd, ln)
        ln *= 2
    return fwd + bwd


def _pool_kernel(p_ref, band_ref, w_ref, sc_ref, o_ref, *, seq, on_grid):
    grp = pl.program_id(1)
    rows, width = p_ref.shape
    for gi, w in enumerate(POOL_WINDOWS):
        @pl.when(grp == gi)
        def _(w=w):
            pg = p_ref[...]
            if on_grid:
                nb = rows // GRID_W
                pos_c = lax.broadcasted_iota(jnp.int32, (GRID_W, width), 0)
                pos_r = lax.broadcasted_iota(jnp.int32, (nb, 1, width), 0)
                m = _band_window_sums(pg, band_ref[...], GRID_W, nb)
                m = m * _tile_rows(1.0 / _window_counts(pos_c, w, GRID_W), nb)
                m3 = _window_sum_bands(m.reshape(nb, GRID_W, width), w)
                m = (m3 * (1.0 / _window_counts(pos_r, w, nb))).reshape(rows, width)
            else:
                nb = rows // seq
                pos = lax.broadcasted_iota(jnp.int32, (seq, width), 0)
                m = _band_window_sums(pg, band_ref[...], seq, nb)
                m = m * _tile_rows(1.0 / _window_counts(pos, w, seq), nb)
            d = (m - pg).astype(BF16)
            y = jnp.dot(d, w_ref[...], preferred_element_type=F32) * sc_ref[...]
            o_ref[...] = y.astype(o_ref.dtype)


def _pool(proj, w_pool_l, pool_scale_l, col0, batch, seq, on_grid):
    t = proj.shape[0]
    n_grp, gw, _ = w_pool_l.shape
    blk0 = col0 // gw
    spb = 1 if on_grid else min(POOL_SEQ_PER_STEP, batch)
    assert batch % spb == 0
    rows = spb * seq
    n = GRID_W if on_grid else seq
    bands = jnp.asarray(np.stack([_window_band(w, n) for w in POOL_WINDOWS]), BF16)
    return pl.pallas_call(
        functools.partial(_pool_kernel, seq=seq, on_grid=on_grid),
        grid=(batch // spb, n_grp),
        in_specs=[pl.BlockSpec((rows, gw), lambda b, g: (b, blk0 + g)),
                  pl.BlockSpec((None, n, 2 * n), lambda b, g: (g, 0, 0)),
                  pl.BlockSpec((None, gw, gw), lambda b, g: (g, 0, 0)),
                  pl.BlockSpec((1, gw), lambda b, g: (0, g))],
        out_specs=pl.BlockSpec((rows, gw), lambda b, g: (b, g)),
        out_shape=jax.ShapeDtypeStruct((t, n_grp * gw), BF16),
        compiler_params=_params(("arbitrary", "arbitrary")),
        name="pool",
    )(proj, bands, w_pool_l, pool_scale_l.reshape(1, n_grp * gw))


def _mix_kernel(o_ref, pm_ref, x_ref, mod_ref, w_ref, out_ref):
    k = o_ref.shape[1]
    mix = (jnp.dot(o_ref[...], w_ref[0:k, :], preferred_element_type=F32)
           + jnp.dot(pm_ref[...], w_ref[k:, :], preferred_element_type=F32))
    out_ref[...] = x_ref[...] + mod_ref[0][2:3] * mix


def _mix(o, pm, x2, mod, rows_per_mod, w_out, tm=512):
    t, d = x2.shape
    return pl.pallas_call(
        _mix_kernel,
        grid=(t // tm,),
        in_specs=[pl.BlockSpec((tm, o.shape[1]), lambda i: (i, 0)),
                  pl.BlockSpec((tm, pm.shape[1]), lambda i: (i, 0)),
                  pl.BlockSpec((tm, d), lambda i: (i, 0)),
                  pl.BlockSpec((1, N_MOD, d), lambda i: ((i * tm) // rows_per_mod, 0, 0)),
                  pl.BlockSpec(w_out.shape, lambda i: (0, 0))],
        out_specs=pl.BlockSpec((tm, d), lambda i: (i, 0)),
        out_shape=jax.ShapeDtypeStruct((t, d), F32),
        compiler_params=_params(("arbitrary",)),
        name="mix",
    )(o, pm, x2, mod, w_out)


def _ffn_up_kernel(x_ref, xp_ref, xn_ref, mod_ref, nw_ref, wa_ref, wb_ref, cwa_ref, cwb_ref,
                   cba_ref, cbb_ref, o_ref, h_ref, *, tm, seq):
    i = pl.program_id(0)

    @pl.when(pl.program_id(1) == 0)
    def _():
        mod = mod_ref[0]
        nw = nw_ref[...]
        sc, sh = mod[4:5], mod[3:4]
        starts_seq = (i * tm) % seq == 0
        ends_seq = ((i + 1) * tm) % seq == 0
        above = jnp.where(starts_seq, 0.0, _norm_mod(xp_ref[...], nw, sc, sh))
        below = jnp.where(ends_seq, 0.0, _norm_mod(xn_ref[...], nw, sc, sh))
        h_ref[0:CONV_HALO, :] = above.astype(BF16)
        h_ref[CONV_HALO:CONV_HALO + tm, :] = _norm_mod(x_ref[...], nw, sc, sh).astype(BF16)
        h_ref[CONV_HALO + tm:, :] = below.astype(BF16)

    ext = tm + 2 * CONV_HALO
    h = h_ref[...]

    def conv(w_ref, cw_ref, cb_ref):
        u = jnp.dot(h, w_ref[...], preferred_element_type=F32)
        um = u[CONV_HALO:CONV_HALO + tm]
        up = pltpu.roll(u, 1, 0)[CONV_HALO:CONV_HALO + tm]
        un = pltpu.roll(u, ext - 1, 0)[CONV_HALO:CONV_HALO + tm]
        if tm > seq:
            in_seq = lax.broadcasted_iota(jnp.int32, um.shape, 0) % seq
            up = jnp.where(in_seq == 0, 0.0, up)
            un = jnp.where(in_seq == seq - 1, 0.0, un)
        cw = cw_ref[...]
        return up * cw[0:1] + um * cw[1:2] + un * cw[2:3] + cb_ref[...]

    a = conv(wa_ref, cwa_ref, cba_ref)
    b = conv(wb_ref, cwb_ref, cbb_ref)
    o_ref[...] = (_silu(a) * b).astype(o_ref.dtype)


def _ffn_down_kernel(a_ref, x_ref, mod_ref, w_ref, fnw_ref, o_ref, *, final_norm):
    y = x_ref[...] + mod_ref[0][5:6] * jnp.dot(a_ref[...], w_ref[...],
                                               preferred_element_type=F32)
    if final_norm:
        y = y * lax.rsqrt(jnp.mean(y * y, axis=-1, keepdims=True) + EPS) * fnw_ref[...]
    o_ref[...] = y


def _ffn(x2, mod, rows_per_mod, seq, nw, w_up, conv_w, conv_b, w_down, fnw, final_norm,
         tm=1024, tf=512, tm_down=256):
    t, d = x2.shape
    dff = w_down.shape[0]
    nf = dff // tf
    assert seq % tm == 0 or tm % seq == 0
    hb = tm // CONV_HALO
    last_hb = t // CONV_HALO - 1
    cb = conv_b.reshape(1, 2 * dff)
    mod_spec = lambda tile: pl.BlockSpec((1, N_MOD, d),
                                         lambda i, *_: ((i * tile) // rows_per_mod, 0, 0))
    act = pl.pallas_call(
        functools.partial(_ffn_up_kernel, tm=tm, seq=seq),
        grid=(t // tm, nf),
        in_specs=[pl.BlockSpec((tm, d), lambda i, f: (i, 0)),
                  pl.BlockSpec((CONV_HALO, d), lambda i, f: (jnp.maximum(i * hb - 1, 0), 0)),
                  pl.BlockSpec((CONV_HALO, d), lambda i, f: (jnp.minimum((i + 1) * hb, last_hb), 0)),
                  mod_spec(tm),
                  pl.BlockSpec((1, d), lambda i, f: (0, 0)),
                  pl.BlockSpec((d, tf), lambda i, f: (0, f)),
                  pl.BlockSpec((d, tf), lambda i, f: (0, f + nf)),
                  pl.BlockSpec((3, tf), lambda i, f: (0, f)),
                  pl.BlockSpec((3, tf), lambda i, f: (0, f + nf)),
                  pl.BlockSpec((1, tf), lambda i, f: (0, f)),
                  pl.BlockSpec((1, tf), lambda i, f: (0, f + nf))],
        out_specs=pl.BlockSpec((tm, tf), lambda i, f: (i, f)),
        out_shape=jax.ShapeDtypeStruct((t, dff), BF16),
        scratch_shapes=[pltpu.VMEM((tm + 2 * CONV_HALO, d), BF16)],
        compiler_params=_params(("arbitrary", "arbitrary")),
        name="ffn_up",
    )(x2, x2, x2, mod, nw.reshape(1, d), w_up, w_up, conv_w, conv_w, cb, cb)
    return pl.pallas_call(
        functools.partial(_ffn_down_kernel, final_norm=final_norm),
        grid=(t // tm_down,),
        in_specs=[pl.BlockSpec((tm_down, dff), lambda i: (i, 0)),
                  pl.BlockSpec((tm_down, d), lambda i: (i, 0)),
                  mod_spec(tm_down),
                  pl.BlockSpec((dff, d), lambda i: (0, 0), pipeline_mode=pl.Buffered(1)),
                  pl.BlockSpec((1, d), lambda i: (0, 0))],
        out_specs=pl.BlockSpec((tm_down, d), lambda i: (i, 0)),
        out_shape=jax.ShapeDtypeStruct((t, d), F32),
        compiler_params=_params(("arbitrary",)),
        name="ffn_down",
    )(act, x2, mod, w_down, fnw.reshape(1, d))


def kernel(x_prompt, x_sample, state_hgrn, c, c_ctx, w_ada, b_ada, norm1_w, w_in, lb_param,
           hg_norm_w, w_pool, pool_scale, w_out, norm2_w, w_up, conv_w, conv_b, w_down,
           final_norm_w):
    depth = w_in.shape[0]
    bp, lp, d = x_prompt.shape
    bs, ls, _ = x_sample.shape
    hg_width = lb_param.shape[2]

    lb_all = jnp.cumsum(jax.nn.softmax(lb_param.astype(F32), axis=1), axis=1)
    n_c = 1 + bs
    pad = (-n_c) % 8
    cvecs = jnp.concatenate([c_ctx[None, :], c, jnp.zeros((pad, d), F32)], axis=0)

    xp = x_prompt.reshape(bp * lp, d)
    xs = x_sample.reshape(bs * ls, d)
    ctx_states = []
    for l in range(depth):
        last = l == depth - 1
        mods = _ada(cvecs, w_ada[l], b_ada[l])
        mod_p = mods[0:1].reshape(1, N_MOD, d)
        mod_s = mods[1:n_c].reshape(bs, N_MOD, d)
        w_in_l = w_in[l].astype(BF16)
        w_pool_l = w_pool[l].astype(BF16)
        w_out_l = w_out[l].astype(BF16)
        w_up_l = w_up[l].astype(BF16)
        w_down_l = w_down[l].astype(BF16)
        lb = lb_all[:, l]

        def layer(x2, mod, rows_per_mod, batch, seq, s0, on_grid, want_state):
            proj = _in_proj(x2, mod, rows_per_mod, norm1_w[l], w_in_l)
            o, st = _scan(proj, lb, hg_norm_w[l], s0, l, batch, seq, want_state)
            pm = _pool(proj, w_pool_l, pool_scale[l], 5 * hg_width, batch, seq, on_grid)
            x2 = _mix(o, pm, x2, mod, rows_per_mod, w_out_l)
            x2 = _ffn(x2, mod, rows_per_mod, seq, norm2_w[l], w_up_l, conv_w[l], conv_b[l],
                      w_down_l, final_norm_w, last)
            return x2, st

        xp, st = layer(xp, mod_p, bp * lp, bp, lp, None, False, True)
        ctx_states.append(st)
        xs, _ = layer(xs, mod_s, ls, bs, ls, state_hgrn, True, False)

    y_prompt = xp.reshape(bp, lp, d)
    y_sample = xs.reshape(bs, ls, d)
    new_state = jnp.stack(ctx_states, axis=1)
    return (y_prompt, y_sample, new_state)
```
